```python
import math
import jax
import jax.numpy as jnp
from jax import lax
import numpy as np

D_MODEL = 1024
BATCH = 2
SEQ = 16384
DEPTH = 1

HEAD_DIM = 64
HEADS_PER_GROUP = 4
ATTN_GROUPS = ((128, 1), (512, 4), (2048, 16))
N_ATTN_HEADS = HEADS_PER_GROUP * len(ATTN_GROUPS)
ATTN_WIDTH = N_ATTN_HEADS * HEAD_DIM
ATTN_OUT_WIDTH = HEADS_PER_GROUP * HEAD_DIM
ROPE_THETA = 10000.0
SSM_WIDTH = D_MODEL // 2
SSM_GROUP = 16
SSM_GROUPS = SSM_WIDTH // SSM_GROUP
SSM_STATE = 64
DT_MIN = 0.001
DT_MAX = 0.1
N_BRANCHES = 2
IN_WIDTH = 3 * ATTN_WIDTH + SSM_WIDTH + N_BRANCHES * D_MODEL
N_EXPERTS = 32
TOP_K = 4
D_EXPERT = D_MODEL
SWIGLU_ALPHA = 1.702
SWIGLU_LIMIT = 7.0
MOE_BLOCK = 256
NORM_EPS = 1e-6
MASK_VALUE = -1e30

kernel_name = 'hybrid_dilated_attn_s5_moe_block'


def _rmsnorm(x, gain):
    x32 = x.astype(jnp.float32)
    y = x32 * lax.rsqrt(jnp.mean(x32 * x32, axis=-1, keepdims=True) + NORM_EPS)
    return (y * gain.astype(jnp.float32)).astype(x.dtype)


def _rope(t):
    seq, dh = t.shape[1], t.shape[-1]
    inv_freq = ROPE_THETA ** (-jnp.arange(0, dh, 2, dtype=jnp.float32) / dh)
    ang = jnp.arange(seq, dtype=jnp.float32)[:, None] * inv_freq[None, :]
    cos = jnp.cos(ang)[None, :, None, :]
    sin = jnp.sin(ang)[None, :, None, :]
    t32 = t.astype(jnp.float32)
    t1, t2 = t32[..., : dh // 2], t32[..., dh // 2:]
    return jnp.concatenate([t1 * cos - t2 * sin, t2 * cos + t1 * sin], axis=-1).astype(t.dtype)


def _banded_attention(q, k, v, half):
    n, length, heads, dh = q.shape
    nb = -(-length // half)
    pad = nb * half - length
    qb = jnp.pad(q, ((0, 0), (0, pad), (0, 0), (0, 0))).reshape(n, nb, half, heads, dh)

    def windows(t):
        tp = jnp.pad(t, ((0, 0), (half, pad + half), (0, 0), (0, 0))).reshape(n, nb + 2, half, heads, dh)
        return jnp.concatenate([tp[:, :-2], tp[:, 1:-1], tp[:, 2:]], axis=2)

    kw = windows(k)
    vw = windows(v)
    scores = jnp.einsum('nbqhd,nbkhd->nbhqk', qb, kw).astype(jnp.float32) * (dh ** -0.5)
    blk = jnp.arange(nb)[:, None]
    qpos = blk * half + jnp.arange(half)[None, :]
    kpos = (blk - 1) * half + jnp.arange(3 * half)[None, :]
    rel = kpos[:, None, :] - qpos[:, :, None]
    valid = (jnp.abs(rel) <= half) & (kpos[:, None, :] >= 0) & (kpos[:, None, :] < length)
    scores = jnp.where(valid[None, :, None], scores, MASK_VALUE)
    m = jnp.max(scores, axis=-1, keepdims=True)
    p = jnp.exp(scores - m)
    denom = jnp.sum(p, axis=-1)
    o = jnp.einsum('nbhqk,nbkhd->nbqhd', p, vw.astype(jnp.float32)) / jnp.swapaxes(denom, 2, 3)[..., None]
    lse = jnp.swapaxes(m[..., 0] + jnp.log(denom), 2, 3)
    o = o.reshape(n, nb * half, heads, dh)[:, :length]
    lse = lse.reshape(n, nb * half, heads)[:, :length]
    return o, lse


def _dilated_group(q, k, v, window, dilation):
    bsz, seq, heads, dh = q.shape
    length = seq // dilation
    half = window // (2 * dilation)

    def split(t):
        return t.reshape(bsz, length, dilation, heads, dh).transpose(0, 2, 1, 3, 4).reshape(bsz * dilation, length, heads, dh)

    o, lse = _banded_attention(split(q), split(k), split(v), half)
    o = o.reshape(bsz, dilation, length, heads, dh).transpose(0, 2, 1, 3, 4).reshape(bsz, seq, heads, dh)
    lse = lse.reshape(bsz, dilation, length, heads).transpose(0, 2, 1, 3).reshape(bsz, seq, heads)
    return o, lse


def _dilated_attention_mixer(q, k, v):
    bsz, seq = q.shape[0], q.shape[1]
    outs = []
    lses = []
    for g, (window, dilation) in enumerate(ATTN_GROUPS):
        sl = slice(g * HEADS_PER_GROUP, (g + 1) * HEADS_PER_GROUP)
        o, lse = _dilated_group(q[:, :, sl], k[:, :, sl], v[:, :, sl], window, dilation)
        outs.append(o)
        lses.append(lse)
    o = jnp.stack(outs)
    weights = jax.nn.softmax(jnp.stack(lses), axis=0)
    y = jnp.sum(weights[..., None] * o, axis=0)
    return y.reshape(bsz, seq, ATTN_OUT_WIDTH).astype(q.dtype)


def _complex_linear_combine(e1, e2):
    a1r, a1i, b1r, b1i = e1
    a2r, a2i, b2r, b2i = e2
    ar = a2r * a1r - a2i * a1i
    ai = a2r * a1i + a2i * a1r
    br = a2r * b1r - a2i * b1i + b2r
    bi = a2r * b1i + a2i * b1r + b2i
    return (ar, ai, br, bi)


def _s5_mixer(u, lam_re, lam_im, log_dt, b_re, b_im, c_re, c_im, d_skip, w_glu, b_glu):
    bsz, seq, _ = u.shape
    ug = u.reshape(bsz, seq, SSM_GROUPS, SSM_GROUP).astype(jnp.float32)
    y = d_skip.astype(jnp.float32) * ug
    for direction in range(2):
        lr = lam_re[direction].astype(jnp.float32)
        li = lam_im[direction].astype(jnp.float32)
        dt = jnp.exp(log_dt[direction].astype(jnp.float32))[:, None]
        mag = jnp.exp(lr * dt)
        ab_re = mag * jnp.cos(li * dt)
        ab_im = mag * jnp.sin(li * dt)
        den = lr * lr + li * li
        nr = ab_re - 1.0
        f_re = (nr * lr + ab_im * li) / den
        f_im = (ab_im * lr - nr * li) / den
        bu_re = jnp.einsum('bsgc,gpc->bsgp', ug, b_re[direction].astype(jnp.float32))
        bu_im = jnp.einsum('bsgc,gpc->bsgp', ug, b_im[direction].astype(jnp.float32))
        x_re = f_re * bu_re - f_im * bu_im
        x_im = f_re * bu_im + f_im * bu_re
        a_re = jnp.broadcast_to(ab_re, x_re.shape)
        a_im = jnp.broadcast_to(ab_im, x_im.shape)
        _, _, s_re, s_im = lax.associative_scan(_complex_linear_combine, (a_re, a_im, x_re, x_im), axis=1, reverse=(direction == 1))
        y = y + jnp.einsum('bsgp,gcp->bsgc', s_re, c_re[direction].astype(jnp.float32)) - jnp.einsum('bsgp,gcp->bsgc', s_im, c_im[direction].astype(jnp.float32))
    z = jax.nn.gelu(y.reshape(bsz, seq, SSM_WIDTH)).astype(u.dtype)
    return z * jax.nn.sigmoid(z @ w_glu + b_glu)


def _moe(h, w_router, b_router, w_gate_up, b_gate_up, w_down, b_down):
    bsz, seq, d = h.shape
    t = bsz * seq
    xt = h.reshape(t, d)
    logits = (xt @ w_router + b_router).astype(jnp.float32)
    top_vals, top_idx = lax.top_k(logits, TOP_K)
    gate_w = jax.nn.softmax(top_vals, axis=-1)
    e_flat = top_idx.reshape(-1)
    w_flat = gate_w.reshape(-1)
    tok = jnp.repeat(jnp.arange(t, dtype=jnp.int32), TOP_K)
    order = jnp.argsort(e_flat)
    e_sorted = e_flat[order]
    counts = jnp.bincount(e_flat, length=N_EXPERTS)
    padded = ((counts + MOE_BLOCK - 1) // MOE_BLOCK) * MOE_BLOCK
    pad_end = jnp.cumsum(padded)
    pad_start = pad_end - padded
    start = jnp.cumsum(counts) - counts
    rank = jnp.arange(t * TOP_K) - start[e_sorted]
    dest = pad_start[e_sorted] + rank
    n_rows = (-(-(t * TOP_K) // MOE_BLOCK) + N_EXPERTS) * MOE_BLOCK
    n_blocks = n_rows // MOE_BLOCK
    row_tok = jnp.full((n_rows,), t, dtype=jnp.int32).at[dest].set(tok[order])
    row_w = jnp.zeros((n_rows,), dtype=h.dtype).at[dest].set(w_flat[order].astype(h.dtype))
    block_exp = jnp.minimum(jnp.searchsorted(pad_end, jnp.arange(n_blocks) * MOE_BLOCK, side='right'), N_EXPERTS - 1)
    x_pad = jnp.concatenate([xt, jnp.zeros((1, d), xt.dtype)], axis=0)
    xs = x_pad[row_tok].reshape(n_blocks, MOE_BLOCK, d)

    def expert_block(args):
        xb, e = args
        gu = xb @ w_gate_up[e] + b_gate_up[e]
        gate, up = gu[:, :D_EXPERT], gu[:, D_EXPERT:]
        gate = jnp.minimum(gate, SWIGLU_LIMIT)
        up = jnp.clip(up, -SWIGLU_LIMIT, SWIGLU_LIMIT)
        act = gate * jax.nn.sigmoid(SWIGLU_ALPHA * gate) * (up + 1.0)
        return act @ w_down[e] + b_down[e]

    ys = lax.map(expert_block, (xs, block_exp)).reshape(n_rows, d)
    out = jnp.zeros((t + 1, d), h.dtype).at[row_tok].add(ys * row_w[:, None])[:t]
    return out.reshape(bsz, seq, d)


def setup_inputs(seed: int = 0) -> dict:
    key = jax.random.key(seed)
    ks = jax.random.split(key, 26)
    f32 = jnp.float32
    G, P, C = SSM_GROUPS, SSM_STATE, SSM_GROUP

    def nrm(k, shape, scale):
        return jax.random.normal(k, shape, f32) * scale

    return {
        'x': nrm(ks[0], (BATCH, SEQ, D_MODEL), 1.0),
        'norm1': 1.0 + nrm(ks[1], (DEPTH, D_MODEL), 0.02),
        'w_in': nrm(ks[2], (DEPTH, D_MODEL, IN_WIDTH), D_MODEL ** -0.5),
        'q_norm': 1.0 + nrm(ks[3], (DEPTH, N_ATTN_HEADS, HEAD_DIM), 0.02),
        'k_norm': 1.0 + nrm(ks[4], (DEPTH, N_ATTN_HEADS, HEAD_DIM), 0.02),
        'lam_re': -0.5 * jnp.exp(nrm(ks[5], (DEPTH, 2, G, P), 0.05)),
        'lam_im': jnp.pi * jnp.arange(P, dtype=f32) + nrm(ks[6], (DEPTH, 2, G, P), 0.01),
        'log_dt': jax.random.uniform(ks[7], (DEPTH, 2, G), f32, math.log(DT_MIN), math.log(DT_MAX)),
        'b_re': nrm(ks[8], (DEPTH, 2, G, P, C), (2 * C) ** -0.5),
        'b_im': nrm(ks[9], (DEPTH, 2, G, P, C), (2 * C) ** -0.5),
        'c_re': nrm(ks[10], (DEPTH, 2, G, C, P), P ** -0.5),
        'c_im': nrm(ks[11], (DEPTH, 2, G, C, P), P ** -0.5),
        'd_skip': nrm(ks[12], (DEPTH, G, C), 1.0),
        'w_glu': nrm(ks[13], (DEPTH, SSM_WIDTH, SSM_WIDTH), SSM_WIDTH ** -0.5),
        'b_glu': nrm(ks[14], (DEPTH, SSM_WIDTH), 0.01),
        'w_attn_branch': nrm(ks[15], (DEPTH, ATTN_OUT_WIDTH, D_MODEL), ATTN_OUT_WIDTH ** -0.5),
        'w_ssm_branch': nrm(ks[16], (DEPTH, SSM_WIDTH, D_MODEL), SSM_WIDTH ** -0.5),
        'w_out': nrm(ks[17], (DEPTH, D_MODEL, D_MODEL), D_MODEL ** -0.5),
        'norm2': 1.0 + nrm(ks[18], (DEPTH, D_MODEL), 0.02),
        'w_router': nrm(ks[19], (DEPTH, D_MODEL, N_EXPERTS), D_MODEL ** -0.5),
        'b_router': nrm(ks[20], (DEPTH, N_EXPERTS), 0.01),
        'w_gate_up': nrm(ks[21], (DEPTH, N_EXPERTS, D_MODEL, 2 * D_EXPERT), D_MODEL ** -0.5),
        'b_gate_up': nrm(ks[22], (DEPTH, N_EXPERTS, 2 * D_EXPERT), 0.01),
        'w_down': nrm(ks[23], (DEPTH, N_EXPERTS, D_EXPERT, D_MODEL), D_EXPERT ** -0.5),
        'b_down': nrm(ks[24], (DEPTH, N_EXPERTS, D_MODEL), 0.01),
    }


def reference(x, norm1, w_in, q_norm, k_norm, lam_re, lam_im, log_dt, b_re, b_im, c_re, c_im, d_skip, w_glu, b_glu, w_attn_branch, w_ssm_branch, w_out, norm2, w_router, b_router, w_gate_up, b_gate_up, w_down, b_down):
    bsz, seq, _ = x.shape
    o_k = ATTN_WIDTH
    o_v = 2 * ATTN_WIDTH
    o_u = 3 * ATTN_WIDTH
    o_g = o_u + SSM_WIDTH
    for l in range(DEPTH):
        h = _rmsnorm(x, norm1[l])
        proj = h @ w_in[l]
        q = proj[..., :o_k].reshape(bsz, seq, N_ATTN_HEADS, HEAD_DIM)
        k = proj[..., o_k:o_v].reshape(bsz, seq, N_ATTN_HEADS, HEAD_DIM)
        v = proj[..., o_v:o_u].reshape(bsz, seq, N_ATTN_HEADS, HEAD_DIM)
        u = proj[..., o_u:o_g]
        gates = jax.nn.sigmoid(proj[..., o_g:].astype(jnp.float32)).astype(x.dtype).reshape(bsz, seq, N_BRANCHES, D_MODEL)
        q = _rope(_rmsnorm(q, q_norm[l]))
        k = _rope(_rmsnorm(k, k_norm[l]))
        y_attn = _dilated_attention_mixer(q, k, v) @ w_attn_branch[l]
        y_ssm = _s5_mixer(u, lam_re[l], lam_im[l], log_dt[l], b_re[l], b_im[l], c_re[l], c_im[l], d_skip[l], w_glu[l], b_glu[l]) @ w_ssm_branch[l]
        mixed = gates[:, :, 0] * y_attn + gates[:, :, 1] * y_ssm
        x = x + mixed @ w_out[l]
        x = x + _moe(_rmsnorm(x, norm2[l]), w_router[l], b_router[l], w_gate_up[l], b_gate_up[l], w_down[l], b_down[l])
    return x
```

```python
import functools
import math

import numpy as np
import jax
import jax.numpy as jnp
from jax import lax
from jax.experimental import pallas as pl
from jax.experimental.pallas import tpu as pltpu

F32 = jnp.float32
BF16 = jnp.bfloat16

HEAD_DIM = 64
HEADS_PER_GROUP = 4
ATTN_GROUPS = ((128, 1), (512, 4), (2048, 16))
GROUP_WIDTH = HEADS_PER_GROUP * HEAD_DIM
ATTN_WIDTH = GROUP_WIDTH * len(ATTN_GROUPS)
BAND = 64
ROPE_THETA = 10000.0
SSM_GROUP = 16
SSM_STATE = 64
N_EXPERTS = 32
TOP_K = 4
SWIGLU_ALPHA = 1.702
SWIGLU_LIMIT = 7.0
NORM_EPS = 1e-6
MASK_VALUE = -1e30

ROW_TILE = 512
ATTN_Q_TILE = 512
ATTN_Q_SUB = 128
SSM_CHUNK = 64
ROUTE_TILE = 256
EXPERT_ROWS = 512
VMEM_LIMIT = 56 * 1024 * 1024
LANES = 128


def _params(*sem):
    return pltpu.CompilerParams(dimension_semantics=sem, vmem_limit_bytes=VMEM_LIMIT)


def _const_spec(shape):
    return pl.BlockSpec(shape, lambda *_: (0,) * len(shape), pipeline_mode=pl.Buffered(1))


def _qk_column_order():
    src = np.zeros(ATTN_WIDTH, np.int32)
    for head in range(ATTN_WIDTH // HEAD_DIM):
        g, hh = divmod(head, HEADS_PER_GROUP)
        for d in range(HEAD_DIM):
            half, f = divmod(d, HEAD_DIM // 2)
            src[g * GROUP_WIDTH + half * (GROUP_WIDTH // 2) + hh * (HEAD_DIM // 2) + f] = head * HEAD_DIM + d
    return src


def _in_proj_kernel(x_ref, g1_ref, w_ref, gq_ref, gk_ref, cos_ref, sin_ref, seg_ref,
                    qkv0_ref, qkv1_ref, qkv2_ref, u_ref, gate_ref, scr_ref):
    x = x_ref[...]
    ms = jnp.mean(x * x, axis=-1, keepdims=True)
    h = (x * lax.rsqrt(ms + NORM_EPS) * g1_ref[...]).astype(BF16)
    cos = cos_ref[...]
    sin = sin_ref[...]
    seg = seg_ref[...]
    half = GROUP_WIDTH // 2
    tm = x.shape[0]

    def proj(c0, width):
        return jnp.dot(h, w_ref[:, c0:c0 + width], preferred_element_type=F32)

    def norm_rope(t, gain_ref, g):
        t1, t2 = t[:, :half], t[:, half:]
        ss = (t1 * t1 + t2 * t2).astype(BF16)
        tot = jnp.dot(ss, seg, preferred_element_type=F32)
        r = lax.rsqrt(tot * (1.0 / HEAD_DIM) + NORM_EPS)
        a = t1 * r * gain_ref[:, g * GROUP_WIDTH:g * GROUP_WIDTH + half]
        b = t2 * r * gain_ref[:, g * GROUP_WIDTH + half:(g + 1) * GROUP_WIDTH]
        return a * cos - b * sin, b * cos + a * sin

    outs = (qkv0_ref, qkv1_ref, qkv2_ref)
    for g, (_, dil) in enumerate(ATTN_GROUPS):
        q1, q2 = norm_rope(proj(g * GROUP_WIDTH, GROUP_WIDTH), gq_ref, g)
        k1, k2 = norm_rope(proj(ATTN_WIDTH + g * GROUP_WIDTH, GROUP_WIDTH), gk_ref, g)
        v = proj(2 * ATTN_WIDTH + g * GROUP_WIDTH, GROUP_WIDTH)
        qkv = jnp.concatenate([q1, q2, k1, k2, v], axis=-1)
        if dil == 1:
            outs[g][...] = qkv.astype(BF16)
        else:
            for c in range(qkv.shape[1] // LANES):
                scr_ref[c] = qkv[:, c * LANES:(c + 1) * LANES]
            for r in range(dil):
                for c in range(qkv.shape[1] // LANES):
                    outs[g][r, :, c * LANES:(c + 1) * LANES] = (
                        scr_ref[c, pl.ds(r, tm // dil, stride=dil), :].astype(BF16))
    o_u = 3 * ATTN_WIDTH
    u_w = u_ref.shape[-1]
    u_ref[...] = proj(o_u, u_w).astype(BF16)
    gate_w = gate_ref.shape[-1]
    for c in range(0, gate_w, 1024):
        gate_ref[:, c:c + 1024] = jax.nn.sigmoid(proj(o_u + u_w + c, 1024)).astype(BF16)


def _in_proj(x2, norm1, w_in, q_norm, k_norm, seq):
    t, d = x2.shape
    tm = ROW_TILE
    in_w = w_in.shape[1]
    ssm_w = d // 2
    gate_w = in_w - 3 * ATTN_WIDTH - ssm_w
    src = _qk_column_order()
    w_p = jnp.concatenate([w_in[:, :ATTN_WIDTH][:, src], w_in[:, ATTN_WIDTH:2 * ATTN_WIDTH][:, src],
                           w_in[:, 2 * ATTN_WIDTH:]], axis=1).astype(BF16)
    gq = (q_norm.reshape(-1)[src] * (HEAD_DIM ** -0.5)).reshape(1, ATTN_WIDTH)
    gk = k_norm.reshape(-1)[src].reshape(1, ATTN_WIDTH)
    inv_freq = ROPE_THETA ** (-jnp.arange(0, HEAD_DIM, 2, dtype=F32) / HEAD_DIM)
    ang = jnp.arange(seq, dtype=F32)[:, None] * inv_freq[None, :]
    cos = jnp.tile(jnp.cos(ang), (1, HEADS_PER_GROUP))
    sin = jnp.tile(jnp.sin(ang), (1, HEADS_PER_GROUP))
    lane = np.arange(GROUP_WIDTH // 2)
    seg = jnp.asarray(lane[:, None] // (HEAD_DIM // 2) == lane[None, :] // (HEAD_DIM // 2), BF16)
    bsz = t // seq
    tiles_per_seq = seq // tm
    qkv_w = 3 * GROUP_WIDTH
    out_shape = [jax.ShapeDtypeStruct((t, qkv_w), BF16)]
    out_specs = [pl.BlockSpec((tm, qkv_w), lambda i: (i, 0))]
    for _, dil in ATTN_GROUPS[1:]:
        out_shape.append(jax.ShapeDtypeStruct((bsz * dil, seq // dil, qkv_w), BF16))
        out_specs.append(pl.BlockSpec((dil, tm // dil, qkv_w),
                                      lambda i: (i // tiles_per_seq, i % tiles_per_seq, 0)))
    out_shape += [jax.ShapeDtypeStruct((t, ssm_w), BF16), jax.ShapeDtypeStruct((t, gate_w), BF16)]
    out_specs += [pl.BlockSpec((tm, ssm_w), lambda i: (i, 0)), pl.BlockSpec((tm, gate_w), lambda i: (i, 0))]
    return pl.pallas_call(
        _in_proj_kernel,
        grid=(t // tm,),
        in_specs=[
            pl.BlockSpec((tm, d), lambda i: (i, 0)),
            _const_spec((1, d)),
            _const_spec((d, in_w)),
            _const_spec((1, ATTN_WIDTH)),
            _const_spec((1, ATTN_WIDTH)),
            pl.BlockSpec((tm, GROUP_WIDTH // 2), lambda i: (i % tiles_per_seq, 0)),
            pl.BlockSpec((tm, GROUP_WIDTH // 2), lambda i: (i % tiles_per_seq, 0)),
            _const_spec((GROUP_WIDTH // 2, GROUP_WIDTH // 2)),
        ],
        out_specs=out_specs,
        out_shape=out_shape,
        scratch_shapes=[pltpu.VMEM((qkv_w // LANES, tm, LANES), F32)],
        compiler_params=_params("parallel"),
    )(x2, norm1.reshape(1, d), w_p, gq, gk, cos, sin, seg)


def _attn_kernel(c_ref, kl_ref, vl_ref, kr_ref, vr_ref, o_ref, lse_ref, kcat_ref, vcat_ref, *, length):
    tq = c_ref.shape[1]
    gw = GROUP_WIDTH
    j = pl.program_id(1)
    kcat_ref[0:BAND] = kl_ref[0]
    kcat_ref[BAND:BAND + tq] = c_ref[0, :, gw:2 * gw]
    kcat_ref[BAND + tq:] = kr_ref[0]
    vcat_ref[0:BAND] = vl_ref[0]
    vcat_ref[BAND:BAND + tq] = c_ref[0, :, 2 * gw:]
    vcat_ref[BAND + tq:] = vr_ref[0]
    sub = min(ATTN_Q_SUB, tq)
    nk = sub + 2 * BAND
    row = lax.broadcasted_iota(jnp.int32, (sub, nk), 0)
    col = lax.broadcasted_iota(jnp.int32, (sub, nk), 1)
    in_band = jnp.abs(col - BAND - row) <= BAND
    q_lane = lax.broadcasted_iota(jnp.int32, (1, gw), 1)
    q_head = (q_lane % (gw // 2)) // (HEAD_DIM // 2)
    v_head = q_lane // HEAD_DIM
    for s in range(tq // sub):
        q_s = c_ref[0, s * sub:(s + 1) * sub, 0:gw]
        k_s = kcat_ref[s * sub:s * sub + nk]
        v_s = vcat_ref[s * sub:s * sub + nk]
        kpos = j * tq + s * sub - BAND + col
        valid = in_band & (kpos >= 0) & (kpos < length)
        out = jnp.zeros((sub, gw), F32)
        lse = jnp.zeros((sub, gw), F32)
        for hh in range(HEADS_PER_GROUP):
            qm = jnp.where(q_head == hh, q_s, jnp.zeros_like(q_s))
            sc = lax.dot_general(qm, k_s, (((1,), (1,)), ((), ())), preferred_element_type=F32)
            sc = jnp.where(valid, sc, MASK_VALUE)
            m = jnp.max(sc, axis=-1, keepdims=True)
            p = jnp.exp(sc - m)
            den = jnp.sum(p, axis=-1, keepdims=True)
            oh = jnp.dot(p.astype(BF16), v_s, preferred_element_type=F32)
            out = jnp.where(v_head == hh, oh / den, out)
            lse = jnp.where(v_head == hh, m + jnp.log(den), lse)
        o_ref[0, s * sub:(s + 1) * sub] = out.astype(BF16)
        lse_ref[0, s * sub:(s + 1) * sub] = lse


def _attention(qkv):
    n, length, _ = qkv.shape
    tq = min(ATTN_Q_TILE, length)
    gw = GROUP_WIDTH
    hb = tq // BAND
    last = length // BAND - 1

    def left(lane_block):
        return pl.BlockSpec((1, BAND, gw), lambda b, j: (b, jnp.maximum(j * hb - 1, 0), lane_block))

    def right(lane_block):
        return pl.BlockSpec((1, BAND, gw), lambda b, j: (b, jnp.minimum((j + 1) * hb, last), lane_block))

    return pl.pallas_call(
        functools.partial(_attn_kernel, length=length),
        grid=(n, length // tq),
        in_specs=[pl.BlockSpec((1, tq, 3 * gw), lambda b, j: (b, j, 0)), left(1), left(2), right(1), right(2)],
        out_specs=[pl.BlockSpec((1, tq, gw), lambda b, j: (b, j, 0)),
                   pl.BlockSpec((1, tq, gw), lambda b, j: (b, j, 0))],
        out_shape=[jax.ShapeDtypeStruct((n, length, gw), BF16), jax.ShapeDtypeStruct((n, length, gw), F32)],
        scratch_shapes=[pltpu.VMEM((tq + 2 * BAND, gw), BF16), pltpu.VMEM((tq + 2 * BAND, gw), BF16)],
        compiler_params=_params("parallel", "parallel"),
    )(qkv, qkv, qkv, qkv, qkv)


def _ssm_operators(lam_re, lam_im, log_dt, b_re, b_im, c_re, c_im, d_skip):
    L = SSM_CHUNK
    G, P = lam_re.shape[1:]
    C = d_skip.shape[-1]
    hp = lax.Precision.HIGHEST
    n = jnp.arange(L + 1, dtype=F32)[:, None, None]
    kern, w_st, v_io, consts = [], [], [], []
    for direction in range(2):
        lr, li = lam_re[direction].astype(F32), lam_im[direction].astype(F32)
        dt = jnp.exp(log_dt[direction].astype(F32))[:, None]
        mag = jnp.exp(lr * dt)
        ab_re, ab_im = mag * jnp.cos(li * dt), mag * jnp.sin(li * dt)
        den = lr * lr + li * li
        nr = ab_re - 1.0
        f_re = (nr * lr + ab_im * li) / den
        f_im = (ab_im * lr - nr * li) / den
        pmag = jnp.exp(n * (lr * dt)[None])
        p_re, p_im = pmag * jnp.cos(n * (li * dt)[None]), pmag * jnp.sin(n * (li * dt)[None])
        br, bi = b_re[direction].astype(F32), b_im[direction].astype(F32)
        fb_re = f_re[..., None] * br - f_im[..., None] * bi
        fb_im = f_re[..., None] * bi + f_im[..., None] * br
        w_re = p_re[..., None] * fb_re[None] - p_im[..., None] * fb_im[None]
        w_im = p_re[..., None] * fb_im[None] + p_im[..., None] * fb_re[None]
        cr, ci = c_re[direction].astype(F32), c_im[direction].astype(F32)
        kern.append(jnp.einsum('gcp,ngpd->ngcd', cr, w_re[:L], precision=hp)
                    - jnp.einsum('gcp,ngpd->ngcd', ci, w_im[:L], precision=hp))
        sel = slice(L - 1, None, -1) if direction == 0 else slice(0, L)
        st_re = w_re[sel].transpose(1, 0, 3, 2).reshape(G, L * C, P)
        st_im = w_im[sel].transpose(1, 0, 3, 2).reshape(G, L * C, P)
        w_st.append(jnp.concatenate([st_re, st_im, st_im, st_re], axis=-1))
        pw = slice(1, L + 1) if direction == 0 else slice(L, 0, -1)
        e_re = cr[None] * p_re[pw][:, :, None, :] - ci[None] * p_im[pw][:, :, None, :]
        e_im = cr[None] * p_im[pw][:, :, None, :] + ci[None] * p_re[pw][:, :, None, :]
        v_io.append(jnp.concatenate([e_re.transpose(1, 3, 0, 2).reshape(G, P, L * C),
                                     -e_im.transpose(1, 3, 0, 2).reshape(G, P, L * C)], axis=1))
        a_re, a_im = p_re[L], p_im[L]
        consts.append(jnp.concatenate([a_re, a_re, a_re, a_re], axis=-1))
        consts.append(jnp.concatenate([-a_im, a_im, a_im, -a_im], axis=-1))
    kf, kb = kern
    zero = kf[0] + kb[0] + jnp.eye(C, dtype=F32)[None] * d_skip.astype(F32)[:, :, None]
    kfull = jnp.concatenate([kb[:0:-1], zero[None], kf[1:]], axis=0)
    lag = np.arange(L)[None, :] - np.arange(L)[:, None] + (L - 1)
    toe = kfull[lag]
    toe = toe.transpose(2, 0, 4, 1, 3).reshape(G, L * C, L * C)
    consts = jnp.stack(consts + consts, axis=1)
    return (toe.astype(BF16), jnp.concatenate(w_st, axis=-1).astype(BF16),
            jnp.concatenate(v_io, axis=1).astype(BF16), consts)


def _ssm_kernel(u_ref, toe_ref, wst_ref, vio_ref, cst_ref, z_ref, f_ref, s_ref, *, bsz):
    u = u_ref[0]
    nc = u.shape[0]
    sw = cst_ref.shape[-1]
    hw = sw // 2
    f_ref[...] = jnp.dot(u, wst_ref[0], preferred_element_type=F32)
    cst = cst_ref[0]
    a1f, a2f, a1b, a2b = cst[0:1], cst[1:2], cst[2:3], cst[3:4]
    per = nc // bsz

    def swap(v):
        return jnp.concatenate([v[:, hw:], v[:, :hw]], axis=-1)

    rows = 8

    def step(i, carry):
        new = []
        for b in range(bsz):
            sf, sb = carry[2 * b], carry[2 * b + 1]
            base_f = pl.multiple_of(b * per + i * rows, rows)
            base_b = pl.multiple_of(b * per + per - rows - i * rows, rows)
            f_blk = f_ref[pl.ds(base_f, rows), 0:sw]
            b_blk = f_ref[pl.ds(base_b, rows), sw:2 * sw]
            f_rows, b_rows = [], []
            for r in range(rows):
                f_rows.append(sf[:, :hw])
                sf = a1f * sf + a2f * swap(sf) + f_blk[r:r + 1]
                b_rows.append(sb[:, :hw])
                sb = a1b * sb + a2b * swap(sb) + b_blk[rows - 1 - r:rows - r]
            s_ref[pl.ds(base_f, rows), 0:hw] = jnp.concatenate(f_rows, axis=0)
            s_ref[pl.ds(base_b, rows), hw:sw] = jnp.concatenate(b_rows[::-1], axis=0)
            new += [sf, sb]
        return tuple(new)

    lax.fori_loop(0, per // rows, step, tuple(jnp.zeros((1, sw), F32) for _ in range(2 * bsz)))
    y = jnp.dot(u, toe_ref[0], preferred_element_type=F32)
    y = y + jnp.dot(s_ref[...].astype(BF16), vio_ref[0], preferred_element_type=F32)
    z_ref[0] = jax.nn.gelu(y).astype(BF16)


def _ssm(u, bsz, lam_re, lam_im, log_dt, b_re, b_im, c_re, c_im, d_skip):
    t, width = u.shape
    L, C = SSM_CHUNK, d_skip.shape[-1]
    G = width // C
    nc = t // L
    toe, w_st, v_io, consts = _ssm_operators(lam_re, lam_im, log_dt, b_re, b_im, c_re, c_im, d_skip)
    sw = consts.shape[-1]
    ug = u.reshape(nc, L, G, C).transpose(2, 0, 1, 3).reshape(G, nc, L * C)
    z = pl.pallas_call(
        functools.partial(_ssm_kernel, bsz=bsz),
        grid=(G,),
        in_specs=[pl.BlockSpec((1, nc, L * C), lambda g: (g, 0, 0)),
                  pl.BlockSpec((1, L * C, L * C), lambda g: (g, 0, 0)),
                  pl.BlockSpec((1, L * C, 2 * sw), lambda g: (g, 0, 0)),
                  pl.BlockSpec((1, sw, L * C), lambda g: (g, 0, 0)),
                  pl.BlockSpec((1, 8, sw), lambda g: (g, 0, 0))],
        out_specs=pl.BlockSpec((1, nc, L * C), lambda g: (g, 0, 0)),
        out_shape=jax.ShapeDtypeStruct((G, nc, L * C), BF16),
        scratch_shapes=[pltpu.VMEM((nc, 2 * sw), F32), pltpu.VMEM((nc, sw), F32)],
        compiler_params=_params("parallel"),
    )(ug, toe, w_st, v_io, consts)
    return z.reshape(G, nc, L, C).transpose(1, 2, 0, 3).reshape(t, width)


def _post_kernel(o0_ref, l0_ref, o1_ref, l1_ref, o2_ref, l2_ref, z_ref, gate_ref, x_ref,
                 wa_ref, wglu_ref, bglu_ref, ws_ref, wo_ref, g2_ref, wr_ref, wrlo_ref, br_ref, tri_ref,
                 x1_ref, hp_ref, idx_ref, rank_ref, wgt_ref, cnt_ref, o_scr, l_scr, carry_ref):
    tm, d = x_ref.shape

    @pl.when(pl.program_id(0) == 0)
    def _():
        carry_ref[...] = jnp.zeros_like(carry_ref)

    outs = [o0_ref[...].astype(F32)]
    lses = [l0_ref[...]]
    for (o_ref, l_ref), (_, dil) in zip(((o1_ref, l1_ref), (o2_ref, l2_ref)), ATTN_GROUPS[1:]):
        nlb = o_scr.shape[0]
        for r in range(dil):
            for c in range(nlb):
                o_scr[c, pl.ds(r, tm // dil, stride=dil), :] = o_ref[r, :, c * LANES:(c + 1) * LANES].astype(F32)
                l_scr[c, pl.ds(r, tm // dil, stride=dil), :] = l_ref[r, :, c * LANES:(c + 1) * LANES]
        outs.append(jnp.concatenate([o_scr[c] for c in range(nlb)], axis=-1))
        lses.append(jnp.concatenate([l_scr[c] for c in range(nlb)], axis=-1))
    top = jnp.maximum(jnp.maximum(lses[0], lses[1]), lses[2])
    es = [jnp.exp(l - top) for l in lses]
    y_attn = (es[0] * outs[0] + es[1] * outs[1] + es[2] * outs[2]) / (es[0] + es[1] + es[2])
    ya = jnp.dot(y_attn.astype(BF16), wa_ref[...], preferred_element_type=F32)
    z = z_ref[...]
    glu = jnp.dot(z, wglu_ref[...], preferred_element_type=F32) + bglu_ref[...]
    zz = z.astype(F32) * jax.nn.sigmoid(glu)
    ys = jnp.dot(zz.astype(BF16), ws_ref[...], preferred_element_type=F32)
    mixed = gate_ref[:, :d].astype(F32) * ya + gate_ref[:, d:].astype(F32) * ys
    x1 = x_ref[...] + jnp.dot(mixed.astype(BF16), wo_ref[...], preferred_element_type=F32)
    x1_ref[...] = x1
    ms = jnp.mean(x1 * x1, axis=-1, keepdims=True)
    h2 = x1 * lax.rsqrt(ms + NORM_EPS) * g2_ref[...]
    hb = h2.astype(BF16)
    bits = lax.bitcast_convert_type(hb.astype(F32), jnp.uint32)
    hp_ref[...] = bits[:, :d // 2] | (bits[:, d // 2:] >> 16)

    h_lo = (h2 - hb.astype(F32)).astype(BF16)
    nt = (((1,), (1,)), ((), ()))
    logits = (lax.dot_general(wr_ref[...], hb, nt, preferred_element_type=F32)
              + lax.dot_general(wrlo_ref[...], hb, nt, preferred_element_type=F32)
              + lax.dot_general(wr_ref[...], h_lo, nt, preferred_element_type=F32) + br_ref[...])
    ne = logits.shape[0]
    e_iota = lax.broadcasted_iota(jnp.int32, logits.shape, 0)
    vals, idxs = [], []
    work = logits
    for _ in range(TOP_K):
        m = jnp.max(work, axis=0, keepdims=True)
        i = jnp.min(jnp.where(work == m, e_iota, ne), axis=0, keepdims=True)
        vals.append(m)
        idxs.append(i)
        work = jnp.where(e_iota == i, -jnp.inf, work)
    ex = [jnp.exp(v - vals[0]) for v in vals]
    tot = ex[0] + ex[1] + ex[2] + ex[3]
    onehot = jnp.zeros(logits.shape, F32)
    for i in idxs:
        onehot = onehot + (e_iota == i).astype(F32)
    before = jnp.dot(onehot.astype(BF16), tri_ref[...], preferred_element_type=F32) + carry_ref[:, 0:1]
    for k in range(TOP_K):
        idx_ref[k:k + 1, :] = idxs[k]
        rank_ref[k:k + 1, :] = jnp.sum(jnp.where(e_iota == idxs[k], before, 0.0), axis=0,
                                       keepdims=True).astype(jnp.int32)
        wgt_ref[k:k + 1, :] = ex[k] / tot
    carry_ref[...] = carry_ref[...] + jnp.sum(onehot, axis=1, keepdims=True)
    cnt_ref[...] = carry_ref[...].astype(jnp.int32)


def _post_mix(x2, seq, attn, z, gates, w_attn_branch, w_glu, b_glu, w_ssm_branch, w_out, norm2, w_router, b_router):
    t, d = x2.shape
    tm = ROW_TILE
    gw = GROUP_WIDTH
    tiles_per_seq = seq // tm
    ne = w_router.shape[1]
    ssm_w = z.shape[1]
    in_specs, args = [], []
    for (o, lse), (_, dil) in zip(attn, ATTN_GROUPS):
        if dil == 1:
            spec = pl.BlockSpec((tm, gw), lambda i: (i, 0))
            o, lse = o.reshape(t, gw), lse.reshape(t, gw)
        else:
            spec = pl.BlockSpec((dil, tm // dil, gw), lambda i: (i // tiles_per_seq, i % tiles_per_seq, 0))
        in_specs += [spec, spec]
        args += [o, lse]
    tri = jnp.asarray(np.arange(tm)[:, None] < np.arange(tm)[None, :], BF16)
    in_specs += [pl.BlockSpec((tm, ssm_w), lambda i: (i, 0)),
                 pl.BlockSpec((tm, 2 * d), lambda i: (i, 0)),
                 pl.BlockSpec((tm, d), lambda i: (i, 0)),
                 _const_spec((gw, d)), _const_spec((ssm_w, ssm_w)), _const_spec((1, ssm_w)),
                 _const_spec((ssm_w, d)), _const_spec((d, d)), _const_spec((1, d)),
                 _const_spec((ne, d)), _const_spec((ne, d)), _const_spec((ne, 1)), _const_spec((tm, tm))]
    wr_t = w_router.T.astype(F32)
    wr_hi = wr_t.astype(BF16)
    wr_lo = (wr_t - wr_hi.astype(F32)).astype(BF16)
    args += [z, gates, x2, w_attn_branch.astype(BF16), w_glu.astype(BF16), b_glu.reshape(1, ssm_w),
             w_ssm_branch.astype(BF16), w_out.astype(BF16), norm2.reshape(1, d),
             wr_hi, wr_lo, b_router.reshape(ne, 1), tri]
    tok_spec = pl.BlockSpec((TOP_K, tm), lambda i: (0, i))
    return pl.pallas_call(
        _post_kernel,
        grid=(t // tm,),
        in_specs=in_specs,
        out_specs=[pl.BlockSpec((tm, d), lambda i: (i, 0)), pl.BlockSpec((tm, d // 2), lambda i: (i, 0)),
                   tok_spec, tok_spec, tok_spec, pl.BlockSpec((ne, LANES), lambda i: (0, 0))],
        out_shape=[jax.ShapeDtypeStruct((t, d), F32), jax.ShapeDtypeStruct((t, d // 2), jnp.uint32),
                   jax.ShapeDtypeStruct((TOP_K, t), jnp.int32), jax.ShapeDtypeStruct((TOP_K, t), jnp.int32),
                   jax.ShapeDtypeStruct((TOP_K, t), F32), jax.ShapeDtypeStruct((ne, LANES), jnp.int32)],
        scratch_shapes=[pltpu.VMEM((gw // LANES, tm, LANES), F32), pltpu.VMEM((gw // LANES, tm, LANES), F32),
                        pltpu.VMEM((ne, LANES), F32)],
        compiler_params=_params("arbitrary"),
    )(*args)


def _row_copy(src_ref, src_row, dst_ref, dst_row, sem):
    return pltpu.make_async_copy(src_ref.at[pl.ds(src_row, 1)], dst_ref.at[pl.ds(dst_row, 1)], sem)


def _dispatch_kernel(dest_ref, h_ref, xs_in_ref, xs_ref, sem):
    del xs_in_ref
    tm = h_ref.shape[0]

    def issue(t, _):
        for k in range(TOP_K):
            _row_copy(h_ref, t, xs_ref, dest_ref[k, t], sem).start()
        return 0

    lax.fori_loop(0, tm, issue, 0)

    def drain(t, _):
        for k in range(TOP_K):
            _row_copy(h_ref, 0, xs_ref, 0, sem).wait()
        return 0

    lax.fori_loop(0, tm, drain, 0)


def _dispatch(dest, hp, n_rows):
    t, w = hp.shape
    tm = ROUTE_TILE
    return pl.pallas_call(
        _dispatch_kernel,
        grid=(t // tm,),
        in_specs=[pl.BlockSpec((TOP_K, tm), lambda i: (0, i), memory_space=pltpu.SMEM),
                  pl.BlockSpec((tm, w), lambda i: (i, 0)),
                  pl.BlockSpec(memory_space=pl.ANY)],
        out_specs=pl.BlockSpec(memory_space=pl.ANY),
        out_shape=jax.ShapeDtypeStruct((n_rows, w), jnp.uint32),
        scratch_shapes=[pltpu.SemaphoreType.DMA(())],
        input_output_aliases={2: 0},
        compiler_params=_params("arbitrary"),
    )(dest, hp, jnp.zeros((n_rows, w), jnp.uint32))


def _expert_kernel(be_ref, nu_ref, xs_ref, wgu_ref, bgu_ref, wd_ref, bd_ref, ys_ref):
    @pl.when(pl.program_id(0) < nu_ref[0])
    def _():
        packed = xs_ref[...]
        half = packed.shape[1]
        hi = lax.bitcast_convert_type(packed & jnp.uint32(0xFFFF0000), F32).astype(BF16)
        lo = lax.bitcast_convert_type(packed << 16, F32).astype(BF16)
        gu = (jnp.dot(hi, wgu_ref[0, :half], preferred_element_type=F32)
              + jnp.dot(lo, wgu_ref[0, half:], preferred_element_type=F32) + bgu_ref[0])
        de = gu.shape[1] // 2
        gate = jnp.minimum(gu[:, :de], SWIGLU_LIMIT)
        up = jnp.clip(gu[:, de:], -SWIGLU_LIMIT, SWIGLU_LIMIT)
        act = gate * jax.nn.sigmoid(SWIGLU_ALPHA * gate) * (up + 1.0)
        ys_ref[...] = jnp.dot(act.astype(BF16), wd_ref[0], preferred_element_type=F32) + bd_ref[0]

    @pl.when(pl.program_id(0) >= nu_ref[0])
    def _():
        ys_ref[...] = jnp.zeros_like(ys_ref)


def _experts(block_exp, n_used, xs, w_gate_up, b_gate_up, w_down, b_down):
    n_rows, half = xs.shape
    ne, d, de2 = w_gate_up.shape
    tb = EXPERT_ROWS

    def row_block(i, be, nu):
        return (jnp.minimum(i, nu[0] - 1), 0)

    def expert_block(i, be, nu):
        return (be[jnp.minimum(i, nu[0] - 1)], 0, 0)

    grid_spec = pltpu.PrefetchScalarGridSpec(
        num_scalar_prefetch=2,
        grid=(n_rows // tb,),
        in_specs=[pl.BlockSpec((tb, half), row_block),
                  pl.BlockSpec((1, d, de2), expert_block),
                  pl.BlockSpec((1, 1, de2), expert_block),
                  pl.BlockSpec((1, de2 // 2, d), expert_block),
                  pl.BlockSpec((1, 1, d), expert_block)],
        out_specs=pl.BlockSpec((tb, d), lambda i, be, nu: (i, 0)),
    )
    return pl.pallas_call(
        _expert_kernel,
        grid_spec=grid_spec,
        out_shape=jax.ShapeDtypeStruct((n_rows, d), F32),
        compiler_params=_params("arbitrary"),
    )(block_exp, n_used, xs, w_gate_up.astype(BF16), b_gate_up.reshape(ne, 1, de2),
      w_down.astype(BF16), b_down.reshape(ne, 1, d))


def _combine_kernel(dest_ref, wgt_ref, x1_ref, ys_ref, out_ref, buf_ref, sem):
    tm = x1_ref.shape[0]

    def issue(t, _):
        for k in range(TOP_K):
            _row_copy(ys_ref, dest_ref[k, t], buf_ref.at[k], t, sem).start()
        return 0

    lax.fori_loop(0, tm, issue, 0)

    def drain(t, _):
        for k in range(TOP_K):
            _row_copy(ys_ref, 0, buf_ref.at[k], 0, sem).wait()
        return 0

    lax.fori_loop(0, tm, drain, 0)
    acc = x1_ref[...]
    for k in range(TOP_K):
        acc = acc + wgt_ref[:, k:k + 1] * buf_ref[k]
    out_ref[...] = acc


def _combine(dest, wgt_t, x1, ys):
    t, d = x1.shape
    tm = ROUTE_TILE
    return pl.pallas_call(
        _combine_kernel,
        grid=(t // tm,),
        in_specs=[pl.BlockSpec((TOP_K, tm), lambda i: (0, i), memory_space=pltpu.SMEM),
                  pl.BlockSpec((tm, TOP_K), lambda i: (i, 0)),
                  pl.BlockSpec((tm, d), lambda i: (i, 0)),
                  pl.BlockSpec(memory_space=pl.ANY)],
        out_specs=pl.BlockSpec((tm, d), lambda i: (i, 0)),
        out_shape=jax.ShapeDtypeStruct((t, d), F32),
        scratch_shapes=[pltpu.VMEM((TOP_K, tm, d), F32), pltpu.SemaphoreType.DMA(())],
        compiler_params=_params("arbitrary"),
    )(dest, wgt_t, x1, ys)


def _moe(x1, hp, idx, rank, wgt, counts, w_gate_up, b_gate_up, w_down, b_down):
    t = x1.shape[0]
    tb = EXPERT_ROWS
    ne = w_gate_up.shape[0]
    n_blocks = -(-(t * TOP_K) // tb) + ne
    padded = ((counts + tb - 1) // tb) * tb
    pad_end = jnp.cumsum(padded)
    pad_start = pad_end - padded
    dest = pad_start[idx] + rank
    block_start = jnp.arange(n_blocks, dtype=jnp.int32) * tb
    block_exp = jnp.minimum(jnp.sum((pad_end[None, :] <= block_start[:, None]).astype(jnp.int32), axis=1), ne - 1)
    n_used = (pad_end[-1:] // tb).astype(jnp.int32)
    xs = _dispatch(dest, hp, n_blocks * tb)
    ys = _experts(block_exp, n_used, xs, w_gate_up, b_gate_up, w_down, b_down)
    return _combine(dest, wgt.T, x1, ys)


def kernel(x, norm1, w_in, q_norm, k_norm, lam_re, lam_im, log_dt, b_re, b_im, c_re, c_im, d_skip, w_glu, b_glu, w_attn_branch, w_ssm_branch, w_out, norm2, w_router, b_router, w_gate_up, b_gate_up, w_down, b_down):
    bsz, seq, d = x.shape
    t = bsz * seq
    x2 = x.reshape(t, d)
    for l in range(norm1.shape[0]):
        qkv0, qkv1, qkv2, u, gates = _in_proj(x2, norm1[l], w_in[l], q_norm[l], k_norm[l], seq)
        attn = [_attention(qkv0.reshape(bsz, seq, -1)), _attention(qkv1), _attention(qkv2)]
        z = _ssm(u, bsz, lam_re[l], lam_im[l], log_dt[l], b_re[l], b_im[l], c_re[l], c_im[l], d_skip[l])
        x1, hp, idx, rank, wgt, counts = _post_mix(
            x2, seq, attn, z, gates, w_attn_branch[l], w_glu[l], b_glu[l], w_ssm_branch[l], w_out[l], norm2[l],
            w_router[l], b_router[l])
        x2 = _moe(x1, hp, idx, rank, wgt, counts[:, 0], w_gate_up[l], b_gate_up[l], w_down[l], b_down[l])
    return x2.reshape(bsz, seq, d)
```

```python
import functools
import math

import numpy as np
import jax
import jax.numpy as jnp
from jax import lax
from jax.experimental import pallas as pl
from jax.experimental.pallas import tpu as pltpu

F32 = jnp.float32
BF16 = jnp.bfloat16

HEAD_DIM = 64
HEADS_PER_GROUP = 4
ATTN_GROUPS = ((128, 1), (512, 4), (2048, 16))
GROUP_WIDTH = HEADS_PER_GROUP * HEAD_DIM
ATTN_WIDTH = GROUP_WIDTH * len(ATTN_GROUPS)
BAND = 64
ROPE_THETA = 10000.0
SSM_GROUP = 16
SSM_STATE = 64
N_EXPERTS = 32
TOP_K = 4
SWIGLU_ALPHA = 1.702
SWIGLU_LIMIT = 7.0
NORM_EPS = 1e-6
MASK_VALUE = -1e30

ROW_TILE = 512
ATTN_Q_TILE = 512
ATTN_Q_SUB = 128
SSM_CHUNK = 64
ROUTE_TILE = 256
EXPERT_ROWS = 512
VMEM_LIMIT = 56 * 1024 * 1024
LANES = 128


def _params(*sem):
    return pltpu.CompilerParams(dimension_semantics=sem, vmem_limit_bytes=VMEM_LIMIT)


def _const_spec(shape):
    return pl.BlockSpec(shape, lambda *_: (0,) * len(shape), pipeline_mode=pl.Buffered(1))


def _permute_qk(a):
    lead = a.shape[:-1]
    n = len(lead)
    a = a.reshape(*lead, len(ATTN_GROUPS), HEADS_PER_GROUP, 2, HEAD_DIM // 2)
    return a.transpose(*range(n), n, n + 2, n + 1, n + 3).reshape(*lead, ATTN_WIDTH)


def _in_proj_kernel(x_ref, g1_ref, w_ref, gq_ref, gk_ref, cos_ref, sin_ref, seg_ref,
                    qkv0_ref, qkv1_ref, qkv2_ref, u_ref, gate_ref, scr_ref):
    x = x_ref[...]
    ms = jnp.mean(x * x, axis=-1, keepdims=True)
    h = (x * lax.rsqrt(ms + NORM_EPS) * g1_ref[...]).astype(BF16)
    cos = cos_ref[...]
    sin = sin_ref[...]
    seg = seg_ref[...]
    half = GROUP_WIDTH // 2
    tm = x.shape[0]

    def proj(c0, width):
        return jnp.dot(h, w_ref[:, c0:c0 + width], preferred_element_type=F32)

    def norm_rope(t, gain_ref, g):
        t1, t2 = t[:, :half], t[:, half:]
        ss = (t1 * t1 + t2 * t2).astype(BF16)
        tot = jnp.dot(ss, seg, preferred_element_type=F32)
        r = lax.rsqrt(tot * (1.0 / HEAD_DIM) + NORM_EPS)
        a = t1 * r * gain_ref[:, g * GROUP_WIDTH:g * GROUP_WIDTH + half]
        b = t2 * r * gain_ref[:, g * GROUP_WIDTH + half:(g + 1) * GROUP_WIDTH]
        return a * cos - b * sin, b * cos + a * sin

    outs = (qkv0_ref, qkv1_ref, qkv2_ref)
    for g, (_, dil) in enumerate(ATTN_GROUPS):
        q1, q2 = norm_rope(proj(g * GROUP_WIDTH, GROUP_WIDTH), gq_ref, g)
        k1, k2 = norm_rope(proj(ATTN_WIDTH + g * GROUP_WIDTH, GROUP_WIDTH), gk_ref, g)
        v = proj(2 * ATTN_WIDTH + g * GROUP_WIDTH, GROUP_WIDTH)
        qkv = jnp.concatenate([q1, q2, k1, k2, v], axis=-1)
        if dil == 1:
            outs[g][...] = qkv.astype(BF16)
        else:
            for c in range(qkv.shape[1] // LANES):
                scr_ref[c] = qkv[:, c * LANES:(c + 1) * LANES]
            for r in range(dil):
                for c in range(qkv.shape[1] // LANES):
                    outs[g][r, :, c * LANES:(c + 1) * LANES] = (
                        scr_ref[c, pl.ds(r, tm // dil, stride=dil), :].astype(BF16))
    o_u = 3 * ATTN_WIDTH
    u_w = u_ref.shape[-1]
    u_ref[...] = proj(o_u, u_w).astype(BF16)
    gate_w = gate_ref.shape[-1]
    for c in range(0, gate_w, 1024):
        gate_ref[:, c:c + 1024] = jax.nn.sigmoid(proj(o_u + u_w + c, 1024)).astype(BF16)


def _in_proj(x2, norm1, w_in, q_norm, k_norm, seq):
    t, d = x2.shape
    tm = ROW_TILE
    in_w = w_in.shape[1]
    ssm_w = d // 2
    gate_w = in_w - 3 * ATTN_WIDTH - ssm_w
    w_p = jnp.concatenate([_permute_qk(w_in[:, :ATTN_WIDTH].astype(BF16)),
                           _permute_qk(w_in[:, ATTN_WIDTH:2 * ATTN_WIDTH].astype(BF16)),
                           w_in[:, 2 * ATTN_WIDTH:].astype(BF16)], axis=1)
    gq = (_permute_qk(q_norm.reshape(-1)) * (HEAD_DIM ** -0.5)).reshape(1, ATTN_WIDTH)
    gk = _permute_qk(k_norm.reshape(-1)).reshape(1, ATTN_WIDTH)
    inv_freq = ROPE_THETA ** (-jnp.arange(0, HEAD_DIM, 2, dtype=F32) / HEAD_DIM)
    ang = jnp.arange(seq, dtype=F32)[:, None] * inv_freq[None, :]
    cos = jnp.tile(jnp.cos(ang), (1, HEADS_PER_GROUP))
    sin = jnp.tile(jnp.sin(ang), (1, HEADS_PER_GROUP))
    lane = np.arange(GROUP_WIDTH // 2)
    seg = jnp.asarray(lane[:, None] // (HEAD_DIM // 2) == lane[None, :] // (HEAD_DIM // 2), BF16)
    bsz = t // seq
    tiles_per_seq = seq // tm
    qkv_w = 3 * GROUP_WIDTH
    out_shape = [jax.ShapeDtypeStruct((t, qkv_w), BF16)]
    out_specs = [pl.BlockSpec((tm, qkv_w), lambda i: (i, 0))]
    for _, dil in ATTN_GROUPS[1:]:
        out_shape.append(jax.ShapeDtypeStruct((bsz * dil, seq // dil, qkv_w), BF16))
        out_specs.append(pl.BlockSpec((dil, tm // dil, qkv_w),
                                      lambda i: (i // tiles_per_seq, i % tiles_per_seq, 0)))
    out_shape += [jax.ShapeDtypeStruct((t, ssm_w), BF16), jax.ShapeDtypeStruct((t, gate_w), BF16)]
    out_specs += [pl.BlockSpec((tm, ssm_w), lambda i: (i, 0)), pl.BlockSpec((tm, gate_w), lambda i: (i, 0))]
    return pl.pallas_call(
        _in_proj_kernel,
        grid=(t // tm,),
        in_specs=[
            pl.BlockSpec((tm, d), lambda i: (i, 0)),
            _const_spec((1, d)),
            _const_spec((d, in_w)),
            _const_spec((1, ATTN_WIDTH)),
            _const_spec((1, ATTN_WIDTH)),
            pl.BlockSpec((tm, GROUP_WIDTH // 2), lambda i: (i % tiles_per_seq, 0)),
            pl.BlockSpec((tm, GROUP_WIDTH // 2), lambda i: (i % tiles_per_seq, 0)),
            _const_spec((GROUP_WIDTH // 2, GROUP_WIDTH // 2)),
        ],
        out_specs=out_specs,
        out_shape=out_shape,
        scratch_shapes=[pltpu.VMEM((qkv_w // LANES, tm, LANES), F32)],
        compiler_params=_params("parallel"),
    )(x2, norm1.reshape(1, d), w_p, gq, gk, cos, sin, seg)


def _attn_kernel(c_ref, kl_ref, vl_ref, kr_ref, vr_ref, o_ref, lse_ref, kcat_ref, vcat_ref, *, length):
    tq = c_ref.shape[1]
    gw = GROUP_WIDTH
    j = pl.program_id(1)
    kcat_ref[0:BAND] = kl_ref[0]
    kcat_ref[BAND:BAND + tq] = c_ref[0, :, gw:2 * gw]
    kcat_ref[BAND + tq:] = kr_ref[0]
    vcat_ref[0:BAND] = vl_ref[0]
    vcat_ref[BAND:BAND + tq] = c_ref[0, :, 2 * gw:]
    vcat_ref[BAND + tq:] = vr_ref[0]
    sub = min(ATTN_Q_SUB, tq)
    nk = sub + 2 * BAND
    nh = HEADS_PER_GROUP
    row = lax.broadcasted_iota(jnp.int32, (nh * sub, nk), 0) & (sub - 1)
    col = lax.broadcasted_iota(jnp.int32, (nh * sub, nk), 1)
    in_band = jnp.abs(col - BAND - row) <= BAND
    q_lane = lax.broadcasted_iota(jnp.int32, (1, gw), 1)
    q_head = (q_lane % (gw // 2)) // (HEAD_DIM // 2)
    v_head = q_lane // HEAD_DIM
    for s in range(tq // sub):
        q_s = c_ref[0, s * sub:(s + 1) * sub, 0:gw]
        k_s = kcat_ref[s * sub:s * sub + nk]
        v_s = vcat_ref[s * sub:s * sub + nk]
        kpos = j * tq + s * sub - BAND + col
        valid = in_band & (kpos >= 0) & (kpos < length)
        q4 = jnp.concatenate([jnp.where(q_head == hh, q_s, jnp.zeros_like(q_s)) for hh in range(nh)], axis=0)
        sc = lax.dot_general(q4, k_s, (((1,), (1,)), ((), ())), preferred_element_type=F32)
        sc = jnp.where(valid, sc, MASK_VALUE)
        m = jnp.max(sc, axis=-1, keepdims=True)
        p = jnp.exp(sc - m)
        den = jnp.sum(p, axis=-1, keepdims=True)
        o4 = jnp.dot(p.astype(BF16), v_s, preferred_element_type=F32) * (1.0 / den)
        l4 = m + jnp.log(den)
        out = o4[0:sub]
        lse = jnp.broadcast_to(l4[0:sub], (sub, gw))
        for hh in range(1, nh):
            out = jnp.where(v_head == hh, o4[hh * sub:(hh + 1) * sub], out)
            lse = jnp.where(v_head == hh, l4[hh * sub:(hh + 1) * sub], lse)
        o_ref[0, s * sub:(s + 1) * sub] = out.astype(BF16)
        lse_ref[0, s * sub:(s + 1) * sub] = lse


def _attention(qkv):
    n, length, _ = qkv.shape
    tq = min(ATTN_Q_TILE, length)
    gw = GROUP_WIDTH
    hb = tq // BAND
    last = length // BAND - 1

    def left(lane_block):
        return pl.BlockSpec((1, BAND, gw), lambda b, j: (b, jnp.maximum(j * hb - 1, 0), lane_block))

    def right(lane_block):
        return pl.BlockSpec((1, BAND, gw), lambda b, j: (b, jnp.minimum((j + 1) * hb, last), lane_block))

    return pl.pallas_call(
        functools.partial(_attn_kernel, length=length),
        grid=(n, length // tq),
        in_specs=[pl.BlockSpec((1, tq, 3 * gw), lambda b, j: (b, j, 0)), left(1), left(2), right(1), right(2)],
        out_specs=[pl.BlockSpec((1, tq, gw), lambda b, j: (b, j, 0)),
                   pl.BlockSpec((1, tq, gw), lambda b, j: (b, j, 0))],
        out_shape=[jax.ShapeDtypeStruct((n, length, gw), BF16), jax.ShapeDtypeStruct((n, length, gw), F32)],
        scratch_shapes=[pltpu.VMEM((tq + 2 * BAND, gw), BF16), pltpu.VMEM((tq + 2 * BAND, gw), BF16)],
        compiler_params=_params("parallel", "parallel"),
    )(qkv, qkv, qkv, qkv, qkv)


def _ssm_operators(lam_re, lam_im, log_dt, b_re, b_im, c_re, c_im, d_skip):
    L = SSM_CHUNK
    G, P = lam_re.shape[1:]
    C = d_skip.shape[-1]
    hp = lax.Precision.HIGHEST
    n = jnp.arange(L + 1, dtype=F32)[:, None, None]
    kern, w_st, v_io, consts = [], [], [], []
    for direction in range(2):
        lr, li = lam_re[direction].astype(F32), lam_im[direction].astype(F32)
        dt = jnp.exp(log_dt[direction].astype(F32))[:, None]
        mag = jnp.exp(lr * dt)
        ab_re, ab_im = mag * jnp.cos(li * dt), mag * jnp.sin(li * dt)
        den = lr * lr + li * li
        nr = ab_re - 1.0
        f_re = (nr * lr + ab_im * li) / den
        f_im = (ab_im * lr - nr * li) / den
        pmag = jnp.exp(n * (lr * dt)[None])
        p_re, p_im = pmag * jnp.cos(n * (li * dt)[None]), pmag * jnp.sin(n * (li * dt)[None])
        br, bi = b_re[direction].astype(F32), b_im[direction].astype(F32)
        fb_re = f_re[..., None] * br - f_im[..., None] * bi
        fb_im = f_re[..., None] * bi + f_im[..., None] * br
        w_re = p_re[..., None] * fb_re[None] - p_im[..., None] * fb_im[None]
        w_im = p_re[..., None] * fb_im[None] + p_im[..., None] * fb_re[None]
        cr, ci = c_re[direction].astype(F32), c_im[direction].astype(F32)
        kern.append(jnp.einsum('gcp,ngpd->ngcd', cr, w_re[:L], precision=hp)
                    - jnp.einsum('gcp,ngpd->ngcd', ci, w_im[:L], precision=hp))
        sel = slice(L - 1, None, -1) if direction == 0 else slice(0, L)
        st_re = w_re[sel].transpose(1, 0, 3, 2).reshape(G, L * C, P)
        st_im = w_im[sel].transpose(1, 0, 3, 2).reshape(G, L * C, P)
        w_st.append(jnp.concatenate([st_re, st_im, st_im, st_re], axis=-1))
        pw = slice(1, L + 1) if direction == 0 else slice(L, 0, -1)
        e_re = cr[None] * p_re[pw][:, :, None, :] - ci[None] * p_im[pw][:, :, None, :]
        e_im = cr[None] * p_im[pw][:, :, None, :] + ci[None] * p_re[pw][:, :, None, :]
        v_io.append(jnp.concatenate([e_re.transpose(1, 3, 0, 2).reshape(G, P, L * C),
                                     -e_im.transpose(1, 3, 0, 2).reshape(G, P, L * C)], axis=1))
        a_re, a_im = p_re[L], p_im[L]
        consts.append(jnp.concatenate([a_re, a_re, a_re, a_re], axis=-1))
        consts.append(jnp.concatenate([-a_im, a_im, a_im, -a_im], axis=-1))
    kf, kb = kern
    zero = kf[0] + kb[0] + jnp.eye(C, dtype=F32)[None] * d_skip.astype(F32)[:, :, None]
    kfull = jnp.concatenate([kb[:0:-1], zero[None], kf[1:]], axis=0)
    flat = kfull.astype(BF16).transpose(1, 3, 0, 2).reshape(G, C, (2 * L - 1) * C)
    toe = jnp.stack([flat[:, :, (L - 1 - t) * C:(2 * L - 1 - t) * C] for t in range(L)], axis=1)
    toe = toe.reshape(G, L * C, L * C)
    consts = jnp.stack(consts + consts, axis=1)
    return (toe.astype(BF16), jnp.concatenate(w_st, axis=-1).astype(BF16),
            jnp.concatenate(v_io, axis=1).astype(BF16), consts)


def _ssm_kernel(u_ref, toe_ref, wst_ref, vio_ref, cst_ref, z_ref, f_ref, s_ref, *, bsz):
    u = u_ref[0]
    nc = u.shape[0]
    sw = cst_ref.shape[-1]
    hw = sw // 2
    f_ref[...] = jnp.dot(u, wst_ref[0], preferred_element_type=F32)
    cst = cst_ref[0]
    a1f, a2f, a1b, a2b = cst[0:1], cst[1:2], cst[2:3], cst[3:4]
    per = nc // bsz

    def swap(v):
        return jnp.concatenate([v[:, hw:], v[:, :hw]], axis=-1)

    rows = 8

    def step(i, carry):
        new = []
        for b in range(bsz):
            sf, sb = carry[2 * b], carry[2 * b + 1]
            base_f = pl.multiple_of(b * per + i * rows, rows)
            base_b = pl.multiple_of(b * per + per - rows - i * rows, rows)
            f_blk = f_ref[pl.ds(base_f, rows), 0:sw]
            b_blk = f_ref[pl.ds(base_b, rows), sw:2 * sw]
            f_rows, b_rows = [], []
            for r in range(rows):
                f_rows.append(sf[:, :hw])
                sf = a1f * sf + a2f * swap(sf) + f_blk[r:r + 1]
                b_rows.append(sb[:, :hw])
                sb = a1b * sb + a2b * swap(sb) + b_blk[rows - 1 - r:rows - r]
            s_ref[pl.ds(base_f, rows), 0:hw] = jnp.concatenate(f_rows, axis=0)
            s_ref[pl.ds(base_b, rows), hw:sw] = jnp.concatenate(b_rows[::-1], axis=0)
            new += [sf, sb]
        return tuple(new)

    lax.fori_loop(0, per // rows, step, tuple(jnp.zeros((1, sw), F32) for _ in range(2 * bsz)))
    y = jnp.dot(u, toe_ref[0], preferred_element_type=F32)
    y = y + jnp.dot(s_ref[...].astype(BF16), vio_ref[0], preferred_element_type=F32)
    z_ref[0] = jax.nn.gelu(y).astype(BF16)


def _ssm(u, bsz, lam_re, lam_im, log_dt, b_re, b_im, c_re, c_im, d_skip):
    t, width = u.shape
    L, C = SSM_CHUNK, d_skip.shape[-1]
    G = width // C
    nc = t // L
    toe, w_st, v_io, consts = _ssm_operators(lam_re, lam_im, log_dt, b_re, b_im, c_re, c_im, d_skip)
    sw = consts.shape[-1]
    ug = u.reshape(nc, L, G, C).transpose(2, 0, 1, 3).reshape(G, nc, L * C)
    z = pl.pallas_call(
        functools.partial(_ssm_kernel, bsz=bsz),
        grid=(G,),
        in_specs=[pl.BlockSpec((1, nc, L * C), lambda g: (g, 0, 0)),
                  pl.BlockSpec((1, L * C, L * C), lambda g: (g, 0, 0)),
                  pl.BlockSpec((1, L * C, 2 * sw), lambda g: (g, 0, 0)),
                  pl.BlockSpec((1, sw, L * C), lambda g: (g, 0, 0)),
                  pl.BlockSpec((1, 8, sw), lambda g: (g, 0, 0))],
        out_specs=pl.BlockSpec((1, nc, L * C), lambda g: (g, 0, 0)),
        out_shape=jax.ShapeDtypeStruct((G, nc, L * C), BF16),
        scratch_shapes=[pltpu.VMEM((nc, 2 * sw), F32), pltpu.VMEM((nc, sw), F32)],
        compiler_params=_params("parallel"),
    )(ug, toe, w_st, v_io, consts)
    return z.reshape(G, nc, L, C).transpose(1, 2, 0, 3).reshape(t, width)


def _post_kernel(o0_ref, l0_ref, o1_ref, l1_ref, o2_ref, l2_ref, z_ref, gate_ref, x_ref,
                 wa_ref, wglu_ref, bglu_ref, ws_ref, wo_ref, g2_ref, wr_ref, wrlo_ref, br_ref, tri_ref,
                 x1_ref, hp_ref, idx_ref, rank_ref, wgt_ref, cnt_ref, o_scr, l_scr, carry_ref):
    tm, d = x_ref.shape

    @pl.when(pl.program_id(0) == 0)
    def _():
        carry_ref[...] = jnp.zeros_like(carry_ref)

    outs = [o0_ref[...].astype(F32)]
    lses = [l0_ref[...]]
    for (o_ref, l_ref), (_, dil) in zip(((o1_ref, l1_ref), (o2_ref, l2_ref)), ATTN_GROUPS[1:]):
        nlb = o_scr.shape[0]
        for r in range(dil):
            for c in range(nlb):
                o_scr[c, pl.ds(r, tm // dil, stride=dil), :] = o_ref[r, :, c * LANES:(c + 1) * LANES].astype(F32)
                l_scr[c, pl.ds(r, tm // dil, stride=dil), :] = l_ref[r, :, c * LANES:(c + 1) * LANES]
        outs.append(jnp.concatenate([o_scr[c] for c in range(nlb)], axis=-1))
        lses.append(jnp.concatenate([l_scr[c] for c in range(nlb)], axis=-1))
    top = jnp.maximum(jnp.maximum(lses[0], lses[1]), lses[2])
    es = [jnp.exp(l - top) for l in lses]
    y_attn = (es[0] * outs[0] + es[1] * outs[1] + es[2] * outs[2]) / (es[0] + es[1] + es[2])
    ya = jnp.dot(y_attn.astype(BF16), wa_ref[...], preferred_element_type=F32)
    z = z_ref[...]
    glu = jnp.dot(z, wglu_ref[...], preferred_element_type=F32) + bglu_ref[...]
    zz = z.astype(F32) * jax.nn.sigmoid(glu)
    ys = jnp.dot(zz.astype(BF16), ws_ref[...], preferred_element_type=F32)
    mixed = gate_ref[:, :d].astype(F32) * ya + gate_ref[:, d:].astype(F32) * ys
    x1 = x_ref[...] + jnp.dot(mixed.astype(BF16), wo_ref[...], preferred_element_type=F32)
    x1_ref[...] = x1
    ms = jnp.mean(x1 * x1, axis=-1, keepdims=True)
    h2 = x1 * lax.rsqrt(ms + NORM_EPS) * g2_ref[...]
    hb = h2.astype(BF16)
    bits = lax.bitcast_convert_type(hb.astype(F32), jnp.uint32)
    hp_ref[...] = bits[:, :d // 2] | (bits[:, d // 2:] >> 16)

    h_lo = (h2 - hb.astype(F32)).astype(BF16)
    nt = (((1,), (1,)), ((), ()))
    logits = (lax.dot_general(wr_ref[...], hb, nt, preferred_element_type=F32)
              + lax.dot_general(wrlo_ref[...], hb, nt, preferred_element_type=F32)
              + lax.dot_general(wr_ref[...], h_lo, nt, preferred_element_type=F32) + br_ref[...])
    ne = logits.shape[0]
    e_iota = lax.broadcasted_iota(jnp.int32, logits.shape, 0)
    vals, idxs = [], []
    work = logits
    for _ in range(TOP_K):
        m = jnp.max(work, axis=0, keepdims=True)
        i = jnp.min(jnp.where(work == m, e_iota, ne), axis=0, keepdims=True)
        vals.append(m)
        idxs.append(i)
        work = jnp.where(e_iota == i, -jnp.inf, work)
    ex = [jnp.exp(v - vals[0]) for v in vals]
    tot = ex[0] + ex[1] + ex[2] + ex[3]
    onehot = jnp.zeros(logits.shape, F32)
    for i in idxs:
        onehot = onehot + (e_iota == i).astype(F32)
    before = jnp.dot(onehot.astype(BF16), tri_ref[...], preferred_element_type=F32) + carry_ref[:, 0:1]
    for k in range(TOP_K):
        idx_ref[k:k + 1, :] = idxs[k]
        rank_ref[k:k + 1, :] = jnp.sum(jnp.where(e_iota == idxs[k], before, 0.0), axis=0,
                                       keepdims=True).astype(jnp.int32)
        wgt_ref[k:k + 1, :] = ex[k] / tot
    carry_ref[...] = carry_ref[...] + jnp.sum(onehot, axis=1, keepdims=True)
    cnt_ref[...] = carry_ref[...].astype(jnp.int32)


def _post_mix(x2, seq, attn, z, gates, w_attn_branch, w_glu, b_glu, w_ssm_branch, w_out, norm2, w_router, b_router):
    t, d = x2.shape
    tm = ROW_TILE
    gw = GROUP_WIDTH
    tiles_per_seq = seq // tm
    ne = w_router.shape[1]
    ssm_w = z.shape[1]
    in_specs, args = [], []
    for (o, lse), (_, dil) in zip(attn, ATTN_GROUPS):
        if dil == 1:
            spec = pl.BlockSpec((tm, gw), lambda i: (i, 0))
            o, lse = o.reshape(t, gw), lse.reshape(t, gw)
        else:
            spec = pl.BlockSpec((dil, tm // dil, gw), lambda i: (i // tiles_per_seq, i % tiles_per_seq, 0))
        in_specs += [spec, spec]
        args += [o, lse]
    tri = jnp.asarray(np.arange(tm)[:, None] < np.arange(tm)[None, :], BF16)
    in_specs += [pl.BlockSpec((tm, ssm_w), lambda i: (i, 0)),
                 pl.BlockSpec((tm, 2 * d), lambda i: (i, 0)),
                 pl.BlockSpec((tm, d), lambda i: (i, 0)),
                 _const_spec((gw, d)), _const_spec((ssm_w, ssm_w)), _const_spec((1, ssm_w)),
                 _const_spec((ssm_w, d)), _const_spec((d, d)), _const_spec((1, d)),
                 _const_spec((ne, d)), _const_spec((ne, d)), _const_spec((ne, 1)), _const_spec((tm, tm))]
    wr_t = w_router.T.astype(F32)
    wr_hi = wr_t.astype(BF16)
    wr_lo = (wr_t - wr_hi.astype(F32)).astype(BF16)
    args += [z, gates, x2, w_attn_branch.astype(BF16), w_glu.astype(BF16), b_glu.reshape(1, ssm_w),
             w_ssm_branch.astype(BF16), w_out.astype(BF16), norm2.reshape(1, d),
             wr_hi, wr_lo, b_router.reshape(ne, 1), tri]
    tok_spec = pl.BlockSpec((TOP_K, tm), lambda i: (0, i))
    return pl.pallas_call(
        _post_kernel,
        grid=(t // tm,),
        in_specs=in_specs,
        out_specs=[pl.BlockSpec((tm, d), lambda i: (i, 0)), pl.BlockSpec((tm, d // 2), lambda i: (i, 0)),
                   tok_spec, tok_spec, tok_spec, pl.BlockSpec((ne, LANES), lambda i: (0, 0))],
        out_shape=[jax.ShapeDtypeStruct((t, d), F32), jax.ShapeDtypeStruct((t, d // 2), jnp.uint32),
                   jax.ShapeDtypeStruct((TOP_K, t), jnp.int32), jax.ShapeDtypeStruct((TOP_K, t), jnp.int32),
                   jax.ShapeDtypeStruct((TOP_K, t), F32), jax.ShapeDtypeStruct((ne, LANES), jnp.int32)],
        scratch_shapes=[pltpu.VMEM((gw // LANES, tm, LANES), F32), pltpu.VMEM((gw // LANES, tm, LANES), F32),
                        pltpu.VMEM((ne, LANES), F32)],
        compiler_params=_params("arbitrary"),
    )(*args)


def _row_copy(src_ref, src_row, dst_ref, dst_row, sem):
    return pltpu.make_async_copy(src_ref.at[pl.ds(src_row, 1)], dst_ref.at[pl.ds(dst_row, 1)], sem)


def _dispatch_kernel(dest_ref, h_ref, xs_in_ref, xs_ref, sem):
    del xs_in_ref
    tm = h_ref.shape[0]

    def issue(t, _):
        for k in range(TOP_K):
            _row_copy(h_ref, t, xs_ref, dest_ref[k, t], sem).start()
        return 0

    lax.fori_loop(0, tm, issue, 0)

    def drain(t, _):
        for k in range(TOP_K):
            _row_copy(h_ref, 0, xs_ref, 0, sem).wait()
        return 0

    lax.fori_loop(0, tm, drain, 0)


def _dispatch(dest, hp, n_rows):
    t, w = hp.shape
    tm = ROUTE_TILE
    return pl.pallas_call(
        _dispatch_kernel,
        grid=(t // tm,),
        in_specs=[pl.BlockSpec((TOP_K, tm), lambda i: (0, i), memory_space=pltpu.SMEM),
                  pl.BlockSpec((tm, w), lambda i: (i, 0)),
                  pl.BlockSpec(memory_space=pl.ANY)],
        out_specs=pl.BlockSpec(memory_space=pl.ANY),
        out_shape=jax.ShapeDtypeStruct((n_rows, w), jnp.uint32),
        scratch_shapes=[pltpu.SemaphoreType.DMA(())],
        input_output_aliases={2: 0},
        compiler_params=_params("arbitrary"),
    )(dest, hp, jnp.zeros((n_rows, w), jnp.uint32))


def _expert_kernel(be_ref, nu_ref, xs_ref, wgu_ref, bgu_ref, wd_ref, bd_ref, ys_ref, wgu_bf, wd_bf):
    i = pl.program_id(0)
    active = i < nu_ref[0]

    @pl.when(active & ((i == 0) | (be_ref[i] != be_ref[jnp.maximum(i - 1, 0)])))
    def _():
        wgu_bf[...] = wgu_ref[0].astype(BF16)
        wd_bf[...] = wd_ref[0].astype(BF16)

    @pl.when(active)
    def _():
        packed = xs_ref[...]
        half = packed.shape[1]
        hi = lax.bitcast_convert_type(packed & jnp.uint32(0xFFFF0000), F32).astype(BF16)
        lo = lax.bitcast_convert_type(packed << 16, F32).astype(BF16)
        gu = (jnp.dot(hi, wgu_bf[:half], preferred_element_type=F32)
              + jnp.dot(lo, wgu_bf[half:], preferred_element_type=F32) + bgu_ref[0])
        de = gu.shape[1] // 2
        gate = jnp.minimum(gu[:, :de], SWIGLU_LIMIT)
        up = jnp.clip(gu[:, de:], -SWIGLU_LIMIT, SWIGLU_LIMIT)
        act = gate * jax.nn.sigmoid(SWIGLU_ALPHA * gate) * (up + 1.0)
        ys_ref[...] = jnp.dot(act.astype(BF16), wd_bf[...], preferred_element_type=F32) + bd_ref[0]

    @pl.when(i >= nu_ref[0])
    def _():
        ys_ref[...] = jnp.zeros_like(ys_ref)


def _experts(block_exp, n_used, xs, w_gate_up, b_gate_up, w_down, b_down):
    n_rows, half = xs.shape
    ne, d, de2 = w_gate_up.shape
    tb = EXPERT_ROWS

    def row_block(i, be, nu):
        return (jnp.minimum(i, nu[0] - 1), 0)

    def expert_block(i, be, nu):
        return (be[jnp.minimum(i, nu[0] - 1)], 0, 0)

    grid_spec = pltpu.PrefetchScalarGridSpec(
        num_scalar_prefetch=2,
        grid=(n_rows // tb,),
        in_specs=[pl.BlockSpec((tb, half), row_block),
                  pl.BlockSpec((1, d, de2), expert_block),
                  pl.BlockSpec((1, 1, de2), expert_block),
                  pl.BlockSpec((1, de2 // 2, d), expert_block),
                  pl.BlockSpec((1, 1, d), expert_block)],
        out_specs=pl.BlockSpec((tb, d), lambda i, be, nu: (i, 0)),
        scratch_shapes=[pltpu.VMEM((d, de2), BF16), pltpu.VMEM((de2 // 2, d), BF16)],
    )
    return pl.pallas_call(
        _expert_kernel,
        grid_spec=grid_spec,
        out_shape=jax.ShapeDtypeStruct((n_rows, d), F32),
        compiler_params=_params("arbitrary"),
    )(block_exp, n_used, xs, w_gate_up, b_gate_up.reshape(ne, 1, de2), w_down, b_down.reshape(ne, 1, d))


def _combine_kernel(dest_ref, wgt_ref, x1_ref, ys_ref, out_ref, buf_ref, sem):
    tm = x1_ref.shape[0]

    def issue(t, _):
        for k in range(TOP_K):
            _row_copy(ys_ref, dest_ref[k, t], buf_ref.at[k], t, sem).start()
        return 0

    lax.fori_loop(0, tm, issue, 0)

    def drain(t, _):
        for k in range(TOP_K):
            _row_copy(ys_ref, 0, buf_ref.at[k], 0, sem).wait()
        return 0

    lax.fori_loop(0, tm, drain, 0)
    acc = x1_ref[...]
    for k in range(TOP_K):
        acc = acc + wgt_ref[:, k:k + 1] * buf_ref[k]
    out_ref[...] = acc


def _combine(dest, wgt_t, x1, ys):
    t, d = x1.shape
    tm = ROUTE_TILE
    return pl.pallas_call(
        _combine_kernel,
        grid=(t // tm,),
        in_specs=[pl.BlockSpec((TOP_K, tm), lambda i: (0, i), memory_space=pltpu.SMEM),
                  pl.BlockSpec((tm, TOP_K), lambda i: (i, 0)),
                  pl.BlockSpec((tm, d), lambda i: (i, 0)),
                  pl.BlockSpec(memory_space=pl.ANY)],
        out_specs=pl.BlockSpec((tm, d), lambda i: (i, 0)),
        out_shape=jax.ShapeDtypeStruct((t, d), F32),
        scratch_shapes=[pltpu.VMEM((TOP_K, tm, d), F32), pltpu.SemaphoreType.DMA(())],
        compiler_params=_params("arbitrary"),
    )(dest, wgt_t, x1, ys)


def _moe(x1, hp, idx, rank, wgt, counts, w_gate_up, b_gate_up, w_down, b_down):
    t = x1.shape[0]
    tb = EXPERT_ROWS
    ne = w_gate_up.shape[0]
    n_blocks = -(-(t * TOP_K) // tb) + ne
    padded = ((counts + tb - 1) // tb) * tb
    pad_end = jnp.cumsum(padded)
    pad_start = pad_end - padded
    dest = pad_start[idx] + rank
    block_start = jnp.arange(n_blocks, dtype=jnp.int32) * tb
    block_exp = jnp.minimum(jnp.sum((pad_end[None, :] <= block_start[:, None]).astype(jnp.int32), axis=1), ne - 1)
    n_used = (pad_end[-1:] // tb).astype(jnp.int32)
    xs = _dispatch(dest, hp, n_blocks * tb)
    ys = _experts(block_exp, n_used, xs, w_gate_up, b_gate_up, w_down, b_down)
    return _combine(dest, wgt.T, x1, ys)


def kernel(x, norm1, w_in, q_norm, k_norm, lam_re, lam_im, log_dt, b_re, b_im, c_re, c_im, d_skip, w_glu, b_glu, w_attn_branch, w_ssm_branch, w_out, norm2, w_router, b_router, w_gate_up, b_gate_up, w_down, b_down):
    bsz, seq, d = x.shape
    t = bsz * seq
    x2 = x.reshape(t, d)
    for l in range(norm1.shape[0]):
        qkv0, qkv1, qkv2, u, gates = _in_proj(x2, norm1[l], w_in[l], q_norm[l], k_norm[l], seq)
        attn = [_attention(qkv0.reshape(bsz, seq, -1)), _attention(qkv1), _attention(qkv2)]
        z = _ssm(u, bsz, lam_re[l], lam_im[l], log_dt[l], b_re[l], b_im[l], c_re[l], c_im[l], d_skip[l])
        x1, hp, idx, rank, wgt, counts = _post_mix(
            x2, seq, attn, z, gates, w_attn_branch[l], w_glu[l], b_glu[l], w_ssm_branch[l], w_out[l], norm2[l],
            w_router[l], b_router[l])
        x2 = _moe(x1, hp, idx, rank, wgt, counts[:, 0], w_gate_up[l], b_gate_up[l], w_down[l], b_down[l])
    return x2.reshape(bsz, seq, d)
```

```python
import functools
import math

import numpy as np
import jax
import jax.numpy as jnp
from jax import lax
from jax.experimental import pallas as pl
from jax.experimental.pallas import tpu as pltpu

F32 = jnp.float32
BF16 = jnp.bfloat16

HEAD_DIM = 64
HEADS_PER_GROUP = 4
ATTN_GROUPS = ((128, 1), (512, 4), (2048, 16))
GROUP_WIDTH = HEADS_PER_GROUP * HEAD_DIM
ATTN_WIDTH = GROUP_WIDTH * len(ATTN_GROUPS)
BAND = 64
ROPE_THETA = 10000.0
SSM_GROUP = 16
SSM_STATE = 64
N_EXPERTS = 32
TOP_K = 4
SWIGLU_ALPHA = 1.702
SWIGLU_LIMIT = 7.0
NORM_EPS = 1e-6
MASK_VALUE = -1e30

ROW_TILE = 512
ATTN_Q_TILE = 512
ATTN_Q_SUB = 128
SSM_CHUNK = 64
SSM_POWER_ROWS = -(-(SSM_CHUNK + 1) // 8) * 8
DISPATCH_TILE = 512
ROUTE_TILE = 256
EXPERT_ROWS = 512
VMEM_LIMIT = 56 * 1024 * 1024
LANES = 128


def _params(*sem):
    return pltpu.CompilerParams(dimension_semantics=sem, vmem_limit_bytes=VMEM_LIMIT)


def _const_spec(shape):
    return pl.BlockSpec(shape, lambda *_: (0,) * len(shape), pipeline_mode=pl.Buffered(1))


def _permute_qk(a):
    lead = a.shape[:-1]
    n = len(lead)
    a = a.reshape(*lead, len(ATTN_GROUPS), HEADS_PER_GROUP, 2, HEAD_DIM // 2)
    return a.transpose(*range(n), n, n + 2, n + 1, n + 3).reshape(*lead, ATTN_WIDTH)


def _permute_group_lanes(piece):
    hd, hh = HEAD_DIM, HEAD_DIM // 2
    firsts = [piece[:, h * hd:h * hd + hh] for h in range(HEADS_PER_GROUP)]
    seconds = [piece[:, h * hd + hh:(h + 1) * hd] for h in range(HEADS_PER_GROUP)]
    return jnp.concatenate(firsts + seconds, axis=-1)


def _in_proj_kernel(x_ref, g1_ref, wf_ref, gq_ref, gk_ref, cos_ref, sin_ref, seg_ref,
                    qkv0_ref, qkv1_ref, qkv2_ref, u_ref, gate_ref, w_ref, scr_ref):
    @pl.when(pl.program_id(0) == 0)
    def _():
        rows_per_step = 128

        def cast_rows(i, _):
            rows = pl.ds(pl.multiple_of(i * rows_per_step, rows_per_step), rows_per_step)
            for c0 in range(0, 2 * ATTN_WIDTH, GROUP_WIDTH):
                w_ref[rows, c0:c0 + GROUP_WIDTH] = _permute_group_lanes(wf_ref[rows, c0:c0 + GROUP_WIDTH]).astype(BF16)
            for c0 in range(2 * ATTN_WIDTH, wf_ref.shape[1], GROUP_WIDTH):
                w_ref[rows, c0:c0 + GROUP_WIDTH] = wf_ref[rows, c0:c0 + GROUP_WIDTH].astype(BF16)
            return 0

        lax.fori_loop(0, wf_ref.shape[0] // rows_per_step, cast_rows, 0)

    x = x_ref[...]
    ms = jnp.mean(x * x, axis=-1, keepdims=True)
    h = (x * lax.rsqrt(ms + NORM_EPS) * g1_ref[...]).astype(BF16)
    cos = cos_ref[...]
    sin = sin_ref[...]
    seg = seg_ref[...]
    half = GROUP_WIDTH // 2
    tm = x.shape[0]

    def proj(c0, width):
        return jnp.dot(h, w_ref[:, c0:c0 + width], preferred_element_type=F32)

    def norm_rope(t, gain_ref, g):
        t1, t2 = t[:, :half], t[:, half:]
        ss = (t1 * t1 + t2 * t2).astype(BF16)
        tot = jnp.dot(ss, seg, preferred_element_type=F32)
        r = lax.rsqrt(tot * (1.0 / HEAD_DIM) + NORM_EPS)
        a = t1 * r * gain_ref[:, g * GROUP_WIDTH:g * GROUP_WIDTH + half]
        b = t2 * r * gain_ref[:, g * GROUP_WIDTH + half:(g + 1) * GROUP_WIDTH]
        return a * cos - b * sin, b * cos + a * sin

    outs = (qkv0_ref, qkv1_ref, qkv2_ref)
    for g, (_, dil) in enumerate(ATTN_GROUPS):
        q1, q2 = norm_rope(proj(g * GROUP_WIDTH, GROUP_WIDTH), gq_ref, g)
        k1, k2 = norm_rope(proj(ATTN_WIDTH + g * GROUP_WIDTH, GROUP_WIDTH), gk_ref, g)
        v = proj(2 * ATTN_WIDTH + g * GROUP_WIDTH, GROUP_WIDTH)
        qkv = jnp.concatenate([q1, q2, k1, k2, v], axis=-1)
        if dil == 1:
            outs[g][...] = qkv.astype(BF16)
        else:
            for c in range(qkv.shape[1] // LANES):
                scr_ref[c] = qkv[:, c * LANES:(c + 1) * LANES]
            for r in range(dil):
                for c in range(qkv.shape[1] // LANES):
                    outs[g][r, :, c * LANES:(c + 1) * LANES] = (
                        scr_ref[c, pl.ds(r, tm // dil, stride=dil), :].astype(BF16))
    o_u = 3 * ATTN_WIDTH
    u_w = u_ref.shape[-1]
    u_ref[...] = proj(o_u, u_w).astype(BF16)
    gate_w = gate_ref.shape[-1]
    for c in range(0, gate_w, 1024):
        gate_ref[:, c:c + 1024] = jax.nn.sigmoid(proj(o_u + u_w + c, 1024)).astype(BF16)


def _in_proj(x2, norm1, w_in, q_norm, k_norm, seq):
    t, d = x2.shape
    tm = ROW_TILE
    in_w = w_in.shape[1]
    ssm_w = d // 2
    gate_w = in_w - 3 * ATTN_WIDTH - ssm_w
    gq = (_permute_qk(q_norm.reshape(-1)) * (HEAD_DIM ** -0.5)).reshape(1, ATTN_WIDTH)
    gk = _permute_qk(k_norm.reshape(-1)).reshape(1, ATTN_WIDTH)
    inv_freq = ROPE_THETA ** (-jnp.arange(0, HEAD_DIM, 2, dtype=F32) / HEAD_DIM)
    ang = jnp.arange(seq, dtype=F32)[:, None] * inv_freq[None, :]
    cos = jnp.tile(jnp.cos(ang), (1, HEADS_PER_GROUP))
    sin = jnp.tile(jnp.sin(ang), (1, HEADS_PER_GROUP))
    lane = np.arange(GROUP_WIDTH // 2)
    seg = jnp.asarray(lane[:, None] // (HEAD_DIM // 2) == lane[None, :] // (HEAD_DIM // 2), BF16)
    bsz = t // seq
    tiles_per_seq = seq // tm
    qkv_w = 3 * GROUP_WIDTH
    out_shape = [jax.ShapeDtypeStruct((t, qkv_w), BF16)]
    out_specs = [pl.BlockSpec((tm, qkv_w), lambda i: (i, 0))]
    for _, dil in ATTN_GROUPS[1:]:
        out_shape.append(jax.ShapeDtypeStruct((bsz * dil, seq // dil, qkv_w), BF16))
        out_specs.append(pl.BlockSpec((dil, tm // dil, qkv_w),
                                      lambda i: (i // tiles_per_seq, i % tiles_per_seq, 0)))
    out_shape += [jax.ShapeDtypeStruct((t, ssm_w), BF16), jax.ShapeDtypeStruct((t, gate_w), BF16)]
    out_specs += [pl.BlockSpec((tm, ssm_w), lambda i: (i, 0)), pl.BlockSpec((tm, gate_w), lambda i: (i, 0))]
    return pl.pallas_call(
        _in_proj_kernel,
        grid=(t // tm,),
        in_specs=[
            pl.BlockSpec((tm, d), lambda i: (i, 0)),
            _const_spec((1, d)),
            _const_spec((d, in_w)),
            _const_spec((1, ATTN_WIDTH)),
            _const_spec((1, ATTN_WIDTH)),
            pl.BlockSpec((tm, GROUP_WIDTH // 2), lambda i: (i % tiles_per_seq, 0)),
            pl.BlockSpec((tm, GROUP_WIDTH // 2), lambda i: (i % tiles_per_seq, 0)),
            _const_spec((GROUP_WIDTH // 2, GROUP_WIDTH // 2)),
        ],
        out_specs=out_specs,
        out_shape=out_shape,
        scratch_shapes=[pltpu.VMEM((d, in_w), BF16), pltpu.VMEM((qkv_w // LANES, tm, LANES), F32)],
        compiler_params=_params("arbitrary"),
    )(x2, norm1.reshape(1, d), w_in, gq, gk, cos, sin, seg)


def _attn_kernel(c_ref, kl_ref, vl_ref, kr_ref, vr_ref, o_ref, lse_ref, kcat_ref, vcat_ref, *, length):
    tq = c_ref.shape[1]
    gw = GROUP_WIDTH
    j = pl.program_id(1)
    kcat_ref[0:BAND] = kl_ref[0]
    kcat_ref[BAND:BAND + tq] = c_ref[0, :, gw:2 * gw]
    kcat_ref[BAND + tq:] = kr_ref[0]
    vcat_ref[0:BAND] = vl_ref[0]
    vcat_ref[BAND:BAND + tq] = c_ref[0, :, 2 * gw:]
    vcat_ref[BAND + tq:] = vr_ref[0]
    sub = min(ATTN_Q_SUB, tq)
    nk = sub + 2 * BAND
    nh = HEADS_PER_GROUP
    row = lax.broadcasted_iota(jnp.int32, (nh * sub, nk), 0) & (sub - 1)
    col = lax.broadcasted_iota(jnp.int32, (nh * sub, nk), 1)
    in_band = jnp.abs(col - BAND - row) <= BAND
    q_lane = lax.broadcasted_iota(jnp.int32, (1, gw), 1)
    q_head = (q_lane % (gw // 2)) // (HEAD_DIM // 2)
    v_head = q_lane // HEAD_DIM
    for s in range(tq // sub):
        q_s = c_ref[0, s * sub:(s + 1) * sub, 0:gw]
        k_s = kcat_ref[s * sub:s * sub + nk]
        v_s = vcat_ref[s * sub:s * sub + nk]
        kpos = j * tq + s * sub - BAND + col
        valid = in_band & (kpos >= 0) & (kpos < length)
        q4 = jnp.concatenate([jnp.where(q_head == hh, q_s, jnp.zeros_like(q_s)) for hh in range(nh)], axis=0)
        sc = lax.dot_general(q4, k_s, (((1,), (1,)), ((), ())), preferred_element_type=F32)
        sc = jnp.where(valid, sc, MASK_VALUE)
        m = jnp.max(sc, axis=-1, keepdims=True)
        p = jnp.exp(sc - m)
        den = jnp.sum(p, axis=-1, keepdims=True)
        o4 = jnp.dot(p.astype(BF16), v_s, preferred_element_type=F32) * (1.0 / den)
        l4 = m + jnp.log(den)
        out = o4[0:sub]
        lse = jnp.broadcast_to(l4[0:sub], (sub, gw))
        for hh in range(1, nh):
            out = jnp.where(v_head == hh, o4[hh * sub:(hh + 1) * sub], out)
            lse = jnp.where(v_head == hh, l4[hh * sub:(hh + 1) * sub], lse)
        o_ref[0, s * sub:(s + 1) * sub] = out.astype(BF16)
        lse_ref[0, s * sub:(s + 1) * sub] = lse


def _attention(qkv):
    n, length, _ = qkv.shape
    tq = min(ATTN_Q_TILE, length)
    gw = GROUP_WIDTH
    hb = tq // BAND
    last = length // BAND - 1

    def left(lane_block):
        return pl.BlockSpec((1, BAND, gw), lambda b, j: (b, jnp.maximum(j * hb - 1, 0), lane_block))

    def right(lane_block):
        return pl.BlockSpec((1, BAND, gw), lambda b, j: (b, jnp.minimum((j + 1) * hb, last), lane_block))

    return pl.pallas_call(
        functools.partial(_attn_kernel, length=length),
        grid=(n, length // tq),
        in_specs=[pl.BlockSpec((1, tq, 3 * gw), lambda b, j: (b, j, 0)), left(1), left(2), right(1), right(2)],
        out_specs=[pl.BlockSpec((1, tq, gw), lambda b, j: (b, j, 0)),
                   pl.BlockSpec((1, tq, gw), lambda b, j: (b, j, 0))],
        out_shape=[jax.ShapeDtypeStruct((n, length, gw), BF16), jax.ShapeDtypeStruct((n, length, gw), F32)],
        scratch_shapes=[pltpu.VMEM((tq + 2 * BAND, gw), BF16), pltpu.VMEM((tq + 2 * BAND, gw), BF16)],
        compiler_params=_params("parallel", "parallel"),
    )(qkv, qkv, qkv, qkv, qkv)


def _ssm_operators(lam_re, lam_im, log_dt, b_re, b_im, c_re, c_im, d_skip):
    L = SSM_CHUNK
    G, P = lam_re.shape[1:]
    C = d_skip.shape[-1]
    hp = lax.Precision.HIGHEST
    n = jnp.arange(L + 1, dtype=F32)[:, None, None]
    pad = SSM_POWER_ROWS - (L + 1)
    kern, pw, fbt, ct, consts = [], [], [], [], []
    for direction in range(2):
        lr, li = lam_re[direction].astype(F32), lam_im[direction].astype(F32)
        dt = jnp.exp(log_dt[direction].astype(F32))[:, None]
        mag = jnp.exp(lr * dt)
        ab_re, ab_im = mag * jnp.cos(li * dt), mag * jnp.sin(li * dt)
        den = lr * lr + li * li
        nr = ab_re - 1.0
        f_re = (nr * lr + ab_im * li) / den
        f_im = (ab_im * lr - nr * li) / den
        pmag = jnp.exp(n * (lr * dt)[None])
        p_re, p_im = pmag * jnp.cos(n * (li * dt)[None]), pmag * jnp.sin(n * (li * dt)[None])
        br, bi = b_re[direction].astype(F32), b_im[direction].astype(F32)
        fb_re = f_re[..., None] * br - f_im[..., None] * bi
        fb_im = f_re[..., None] * bi + f_im[..., None] * br
        w_re = p_re[:L, ..., None] * fb_re[None] - p_im[:L, ..., None] * fb_im[None]
        w_im = p_re[:L, ..., None] * fb_im[None] + p_im[:L, ..., None] * fb_re[None]
        cr, ci = c_re[direction].astype(F32), c_im[direction].astype(F32)
        kern.append(jnp.einsum('gcp,ngpd->ngcd', cr, w_re, precision=hp)
                    - jnp.einsum('gcp,ngpd->ngcd', ci, w_im, precision=hp))
        for p in (p_re, p_im):
            p = jnp.pad(p.transpose(1, 0, 2), ((0, 0), (0, pad), (0, 0)))
            pw.append(jnp.concatenate([p, p, p, p], axis=-1))
        tr, ti = fb_re.transpose(0, 2, 1), fb_im.transpose(0, 2, 1)
        fbt.append(jnp.concatenate([tr, ti, ti, tr], axis=-1))
        fbt.append(jnp.concatenate([-ti, tr, tr, -ti], axis=-1))
        ct.append(jnp.concatenate([cr, -ci], axis=-1))
        ct.append(jnp.concatenate([-ci, -cr], axis=-1))
        a_re, a_im = p_re[L], p_im[L]
        consts.append(jnp.concatenate([a_re, a_re, a_re, a_re], axis=-1))
        consts.append(jnp.concatenate([-a_im, a_im, a_im, -a_im], axis=-1))
    kf, kb = kern
    zero = kf[0] + kb[0] + jnp.eye(C, dtype=F32)[None] * d_skip.astype(F32)[:, :, None]
    kfull = jnp.concatenate([kb[:0:-1], zero[None], kf[1:], jnp.zeros_like(zero)[None]], axis=0)
    flat = kfull.astype(BF16).transpose(1, 3, 0, 2).reshape(G, C, 2 * L * C)
    return (flat, jnp.stack(pw, axis=1), jnp.stack(fbt, axis=1), jnp.stack(ct, axis=1),
            jnp.stack(consts + consts, axis=1))


def _ssm_kernel(u_ref, flat_ref, pw_ref, fbt_ref, ct_ref, cst_ref, z_ref,
                toe_ref, wst_ref, vt_ref, f_ref, s_ref, *, bsz):
    u = u_ref[0]
    nc = u.shape[0]
    sw = cst_ref.shape[-1]
    hw = sw // 2
    C = fbt_ref.shape[2]
    L = toe_ref.shape[0] // C
    lc = L * C

    per_tile = LANES // C
    span = (2 * L - per_tile) * C
    for b in range(per_tile):
        shifted = flat_ref[0, :, (per_tile - 1 - b) * C:(per_tile - 1 - b) * C + span]
        for a in range(L // per_tile):
            t_in = a * per_tile + b
            off = (L // per_tile - 1 - a) * LANES
            toe_ref[t_in * C:(t_in + 1) * C, :] = shifted[:, off:off + lc]

    fa_f, fb_f, fa_b, fb_b = fbt_ref[0, 0], fbt_ref[0, 1], fbt_ref[0, 2], fbt_ref[0, 3]
    ca_f, cb_f, ca_b, cb_b = ct_ref[0, 0], ct_ref[0, 1], ct_ref[0, 2], ct_ref[0, 3]

    def power(k, n):
        return pw_ref[0, k, n:n + 1, :]

    for t in range(L):
        rows = slice(t * C, (t + 1) * C)
        wst_ref[rows, 0:sw] = (fa_f * power(0, L - 1 - t) + fb_f * power(1, L - 1 - t)).astype(BF16)
        wst_ref[rows, sw:2 * sw] = (fa_b * power(2, t) + fb_b * power(3, t)).astype(BF16)
        vt_ref[rows, 0:hw] = (ca_f * power(0, t + 1)[:, :hw] + cb_f * power(1, t + 1)[:, :hw]).astype(BF16)
        vt_ref[rows, hw:sw] = (ca_b * power(2, L - t)[:, :hw] + cb_b * power(3, L - t)[:, :hw]).astype(BF16)

    f_ref[...] = jnp.dot(u, wst_ref[...], preferred_element_type=F32)
    cst = cst_ref[0]
    a1f, a2f, a1b, a2b = cst[0:1], cst[1:2], cst[2:3], cst[3:4]
    per = nc // bsz

    def swap(v):
        return jnp.concatenate([v[:, hw:], v[:, :hw]], axis=-1)

    rows = 8

    def step(i, carry):
        new = []
        for b in range(bsz):
            sf, sb = carry[2 * b], carry[2 * b + 1]
            base_f = pl.multiple_of(b * per + i * rows, rows)
            base_b = pl.multiple_of(b * per + per - rows - i * rows, rows)
            f_blk = f_ref[pl.ds(base_f, rows), 0:sw]
            b_blk = f_ref[pl.ds(base_b, rows), sw:2 * sw]
            f_rows, b_rows = [], []
            for r in range(rows):
                f_rows.append(sf[:, :hw])
                sf = a1f * sf + a2f * swap(sf) + f_blk[r:r + 1]
                b_rows.append(sb[:, :hw])
                sb = a1b * sb + a2b * swap(sb) + b_blk[rows - 1 - r:rows - r]
            s_ref[pl.ds(base_f, rows), 0:hw] = jnp.concatenate(f_rows, axis=0)
            s_ref[pl.ds(base_b, rows), hw:sw] = jnp.concatenate(b_rows[::-1], axis=0)
            new += [sf, sb]
        return tuple(new)

    lax.fori_loop(0, per // rows, step, tuple(jnp.zeros((1, sw), F32) for _ in range(2 * bsz)))
    y = jnp.dot(u, toe_ref[...], preferred_element_type=F32)
    y = y + lax.dot_general(s_ref[...].astype(BF16), vt_ref[...], (((1,), (1,)), ((), ())),
                            preferred_element_type=F32)
    z_ref[0] = jax.nn.gelu(y).astype(BF16)


def _ssm(u, bsz, lam_re, lam_im, log_dt, b_re, b_im, c_re, c_im, d_skip):
    t, width = u.shape
    L, C = SSM_CHUNK, d_skip.shape[-1]
    G = width // C
    nc = t // L
    flat, pw, fbt, ct, consts = _ssm_operators(lam_re, lam_im, log_dt, b_re, b_im, c_re, c_im, d_skip)
    sw = consts.shape[-1]
    ug = u.reshape(nc, L, G, C).transpose(2, 0, 1, 3).reshape(G, nc, L * C)

    def group_spec(a):
        return pl.BlockSpec((1,) + a.shape[1:], lambda g: (g,) + (0,) * (a.ndim - 1))

    z = pl.pallas_call(
        functools.partial(_ssm_kernel, bsz=bsz),
        grid=(G,),
        in_specs=[group_spec(a) for a in (ug, flat, pw, fbt, ct, consts)],
        out_specs=pl.BlockSpec((1, nc, L * C), lambda g: (g, 0, 0)),
        out_shape=jax.ShapeDtypeStruct((G, nc, L * C), BF16),
        scratch_shapes=[pltpu.VMEM((L * C, L * C), BF16), pltpu.VMEM((L * C, 2 * sw), BF16),
                        pltpu.VMEM((L * C, sw), BF16), pltpu.VMEM((nc, 2 * sw), F32), pltpu.VMEM((nc, sw), F32)],
        compiler_params=_params("parallel"),
    )(ug, flat, pw, fbt, ct, consts)
    return z.reshape(G, nc, L, C).transpose(1, 2, 0, 3).reshape(t, width)


def _post_kernel(o0_ref, l0_ref, o1_ref, l1_ref, o2_ref, l2_ref, z_ref, gate_ref, x_ref,
                 wa_ref, wglu_ref, bglu_ref, ws_ref, wo_ref, g2_ref, wr_ref, wrlo_ref, br_ref, tri_ref,
                 x1_ref, hp_ref, idx_ref, rank_ref, wgt_ref, cnt_ref, o_scr, l_scr, carry_ref):
    tm, d = x_ref.shape

    @pl.when(pl.program_id(0) == 0)
    def _():
        carry_ref[...] = jnp.zeros_like(carry_ref)

    outs = [o0_ref[...].astype(F32)]
    lses = [l0_ref[...]]
    for (o_ref, l_ref), (_, dil) in zip(((o1_ref, l1_ref), (o2_ref, l2_ref)), ATTN_GROUPS[1:]):
        nlb = o_scr.shape[0]
        for r in range(dil):
            for c in range(nlb):
                o_scr[c, pl.ds(r, tm // dil, stride=dil), :] = o_ref[r, :, c * LANES:(c + 1) * LANES].astype(F32)
                l_scr[c, pl.ds(r, tm // dil, stride=dil), :] = l_ref[r, :, c * LANES:(c + 1) * LANES]
        outs.append(jnp.concatenate([o_scr[c] for c in range(nlb)], axis=-1))
        lses.append(jnp.concatenate([l_scr[c] for c in range(nlb)], axis=-1))
    top = jnp.maximum(jnp.maximum(lses[0], lses[1]), lses[2])
    es = [jnp.exp(l - top) for l in lses]
    y_attn = (es[0] * outs[0] + es[1] * outs[1] + es[2] * outs[2]) / (es[0] + es[1] + es[2])
    ya = jnp.dot(y_attn.astype(BF16), wa_ref[...], preferred_element_type=F32)
    z = z_ref[...]
    glu = jnp.dot(z, wglu_ref[...], preferred_element_type=F32) + bglu_ref[...]
    zz = z.astype(F32) * jax.nn.sigmoid(glu)
    ys = jnp.dot(zz.astype(BF16), ws_ref[...], preferred_element_type=F32)
    mixed = gate_ref[:, :d].astype(F32) * ya + gate_ref[:, d:].astype(F32) * ys
    x1 = x_ref[...] + jnp.dot(mixed.astype(BF16), wo_ref[...], preferred_element_type=F32)
    x1_ref[...] = x1
    ms = jnp.mean(x1 * x1, axis=-1, keepdims=True)
    h2 = x1 * lax.rsqrt(ms + NORM_EPS) * g2_ref[...]
    hb = h2.astype(BF16)
    hp_ref[...] = h2

    h_lo = (h2 - hb.astype(F32)).astype(BF16)
    nt = (((1,), (1,)), ((), ()))
    logits = (lax.dot_general(wr_ref[...], hb, nt, preferred_element_type=F32)
              + lax.dot_general(wrlo_ref[...], hb, nt, preferred_element_type=F32)
              + lax.dot_general(wr_ref[...], h_lo, nt, preferred_element_type=F32) + br_ref[...])
    ne = logits.shape[0]
    e_iota = lax.broadcasted_iota(jnp.int32, logits.shape, 0)
    vals, idxs = [], []
    work = logits
    for _ in range(TOP_K):
        m = jnp.max(work, axis=0, keepdims=True)
        i = jnp.min(jnp.where(work == m, e_iota, ne), axis=0, keepdims=True)
        vals.append(m)
        idxs.append(i)
        work = jnp.where(e_iota == i, -jnp.inf, work)
    ex = [jnp.exp(v - vals[0]) for v in vals]
    tot = ex[0] + ex[1] + ex[2] + ex[3]
    onehot = jnp.zeros(logits.shape, F32)
    for i in idxs:
        onehot = onehot + (e_iota == i).astype(F32)
    before = jnp.dot(onehot.astype(BF16), tri_ref[...], preferred_element_type=F32) + carry_ref[:, 0:1]
    for k in range(TOP_K):
        idx_ref[k:k + 1, :] = idxs[k]
        rank_ref[k:k + 1, :] = jnp.sum(jnp.where(e_iota == idxs[k], before, 0.0), axis=0,
                                       keepdims=True).astype(jnp.int32)
        wgt_ref[k:k + 1, :] = ex[k] / tot
    carry_ref[...] = carry_ref[...] + jnp.sum(onehot, axis=1, keepdims=True)
    cnt_ref[...] = carry_ref[...].astype(jnp.int32)


def _post_mix(x2, seq, attn, z, gates, w_attn_branch, w_glu, b_glu, w_ssm_branch, w_out, norm2, w_router, b_router):
    t, d = x2.shape
    tm = ROW_TILE
    gw = GROUP_WIDTH
    tiles_per_seq = seq // tm
    ne = w_router.shape[1]
    ssm_w = z.shape[1]
    in_specs, args = [], []
    for (o, lse), (_, dil) in zip(attn, ATTN_GROUPS):
        if dil == 1:
            spec = pl.BlockSpec((tm, gw), lambda i: (i, 0))
            o, lse = o.reshape(t, gw), lse.reshape(t, gw)
        else:
            spec = pl.BlockSpec((dil, tm // dil, gw), lambda i: (i // tiles_per_seq, i % tiles_per_seq, 0))
        in_specs += [spec, spec]
        args += [o, lse]
    tri = jnp.asarray(np.arange(tm)[:, None] < np.arange(tm)[None, :], BF16)
    in_specs += [pl.BlockSpec((tm, ssm_w), lambda i: (i, 0)),
                 pl.BlockSpec((tm, 2 * d), lambda i: (i, 0)),
                 pl.BlockSpec((tm, d), lambda i: (i, 0)),
                 _const_spec((gw, d)), _const_spec((ssm_w, ssm_w)), _const_spec((1, ssm_w)),
                 _const_spec((ssm_w, d)), _const_spec((d, d)), _const_spec((1, d)),
                 _const_spec((ne, d)), _const_spec((ne, d)), _const_spec((ne, 1)), _const_spec((tm, tm))]
    wr_t = w_router.T.astype(F32)
    wr_hi = wr_t.astype(BF16)
    wr_lo = (wr_t - wr_hi.astype(F32)).astype(BF16)
    args += [z, gates, x2, w_attn_branch.astype(BF16), w_glu.astype(BF16), b_glu.reshape(1, ssm_w),
             w_ssm_branch.astype(BF16), w_out.astype(BF16), norm2.reshape(1, d),
             wr_hi, wr_lo, b_router.reshape(ne, 1), tri]
    tok_spec = pl.BlockSpec((TOP_K, tm), lambda i: (0, i))
    return pl.pallas_call(
        _post_kernel,
        grid=(t // tm,),
        in_specs=in_specs,
        out_specs=[pl.BlockSpec((tm, d), lambda i: (i, 0)),
                   pl.BlockSpec((tm, d), lambda i: (i, 0)),
                   tok_spec, tok_spec, tok_spec, pl.BlockSpec((ne, LANES), lambda i: (0, 0))],
        out_shape=[jax.ShapeDtypeStruct((t, d), F32), jax.ShapeDtypeStruct((t, d), F32),
                   jax.ShapeDtypeStruct((TOP_K, t), jnp.int32), jax.ShapeDtypeStruct((TOP_K, t), jnp.int32),
                   jax.ShapeDtypeStruct((TOP_K, t), F32), jax.ShapeDtypeStruct((ne, LANES), jnp.int32)],
        scratch_shapes=[pltpu.VMEM((gw // LANES, tm, LANES), F32), pltpu.VMEM((gw // LANES, tm, LANES), F32),
                        pltpu.VMEM((ne, LANES), F32)],
        compiler_params=_params("arbitrary"),
    )(*args)


def _row_copy(src_ref, src_row, dst_ref, dst_row, sem):
    return pltpu.make_async_copy(src_ref.at[pl.ds(src_row, 1)], dst_ref.at[pl.ds(dst_row, 1)], sem)


def _rows_wait(ref, n_rows, sem):
    pltpu.make_async_copy(ref.at[pl.ds(0, n_rows)], ref.at[pl.ds(0, n_rows)], sem).wait()


def _dispatch_kernel(be_ref, nu_ref, dest_ref, h_ref, xs_ref, zero_ref, sems):
    i = pl.program_id(0)
    tm = dest_ref.shape[1]
    tb = zero_ref.shape[0]
    n_blocks = be_ref.shape[0]

    @pl.when(i == 0)
    def _():
        zero_ref[...] = jnp.zeros_like(zero_ref)

        def holds_padding(b):
            return (b >= nu_ref[0] - 1) | (be_ref[b] != be_ref[jnp.minimum(b + 1, n_blocks - 1)])

        def zero_copy(b):
            return pltpu.make_async_copy(zero_ref, xs_ref.at[pl.ds(pl.multiple_of(b * tb, tb), tb)], sems.at[1])

        def start(b, c):
            @pl.when(holds_padding(b))
            def _():
                zero_copy(b).start()
            return c

        def wait(b, c):
            @pl.when(holds_padding(b))
            def _():
                zero_copy(b).wait()
            return c

        lax.fori_loop(0, n_blocks, start, 0)
        lax.fori_loop(0, n_blocks, wait, 0)

    def issue(t, c):
        for k in range(TOP_K):
            _row_copy(h_ref, t, xs_ref, dest_ref[k, t], sems.at[0]).start()
        return c

    lax.fori_loop(0, tm, issue, 0, unroll=4)
    _rows_wait(xs_ref, TOP_K * tm, sems.at[0])


def _dispatch(block_exp, n_used, dest, hp, n_rows):
    t = hp.shape[0]
    slab = hp.shape[1:]
    tm = DISPATCH_TILE
    grid_spec = pltpu.PrefetchScalarGridSpec(
        num_scalar_prefetch=2,
        grid=(t // tm,),
        in_specs=[pl.BlockSpec((TOP_K, tm), lambda i, be, nu: (0, i), memory_space=pltpu.SMEM),
                  pl.BlockSpec((tm,) + slab, lambda i, be, nu: (i,) + (0,) * len(slab))],
        out_specs=pl.BlockSpec(memory_space=pl.ANY),
        scratch_shapes=[pltpu.VMEM((EXPERT_ROWS,) + slab, hp.dtype), pltpu.SemaphoreType.DMA((2,))],
    )
    return pl.pallas_call(
        _dispatch_kernel,
        grid_spec=grid_spec,
        out_shape=jax.ShapeDtypeStruct((n_rows,) + slab, hp.dtype),
        compiler_params=_params("arbitrary"),
    )(block_exp, n_used, dest, hp)


def _expert_kernel(be_ref, nu_ref, xs_ref, wgu_ref, bgu_ref, wd_ref, bd_ref, ys_ref, wgu_bf, wd_bf):
    i = pl.program_id(0)
    active = i < nu_ref[0]

    @pl.when(active & ((i == 0) | (be_ref[i] != be_ref[jnp.maximum(i - 1, 0)])))
    def _():
        wgu_bf[...] = wgu_ref[0].astype(BF16)
        wd_bf[...] = wd_ref[0].astype(BF16)

    @pl.when(active)
    def _():
        gu = jnp.dot(xs_ref[...].astype(BF16), wgu_bf[...], preferred_element_type=F32) + bgu_ref[0]
        de = gu.shape[1] // 2
        gate = jnp.minimum(gu[:, :de], SWIGLU_LIMIT)
        up = jnp.clip(gu[:, de:], -SWIGLU_LIMIT, SWIGLU_LIMIT)
        act = gate * jax.nn.sigmoid(SWIGLU_ALPHA * gate) * (up + 1.0)
        ys_ref[...] = jnp.dot(act.astype(BF16), wd_bf[...], preferred_element_type=F32) + bd_ref[0]

    @pl.when(i >= nu_ref[0])
    def _():
        ys_ref[...] = jnp.zeros_like(ys_ref)


def _experts(block_exp, n_used, xs, w_gate_up, b_gate_up, w_down, b_down):
    n_rows = xs.shape[0]
    ne, d, de2 = w_gate_up.shape
    tb = EXPERT_ROWS

    def row_block(i, be, nu):
        return (jnp.minimum(i, nu[0] - 1), 0)

    def expert_block(i, be, nu):
        return (be[jnp.minimum(i, nu[0] - 1)], 0, 0)

    grid_spec = pltpu.PrefetchScalarGridSpec(
        num_scalar_prefetch=2,
        grid=(n_rows // tb,),
        in_specs=[pl.BlockSpec((tb,) + xs.shape[1:], row_block),
                  pl.BlockSpec((1, d, de2), expert_block),
                  pl.BlockSpec((1, 1, de2), expert_block),
                  pl.BlockSpec((1, de2 // 2, d), expert_block),
                  pl.BlockSpec((1, 1, d), expert_block)],
        out_specs=pl.BlockSpec((tb, d), lambda i, be, nu: (i, 0)),
        scratch_shapes=[pltpu.VMEM((d, de2), BF16), pltpu.VMEM((de2 // 2, d), BF16)],
    )
    return pl.pallas_call(
        _expert_kernel,
        grid_spec=grid_spec,
        out_shape=jax.ShapeDtypeStruct((n_rows, d), F32),
        compiler_params=_params("arbitrary"),
    )(block_exp, n_used, xs, w_gate_up, b_gate_up.reshape(ne, 1, de2), w_down, b_down.reshape(ne, 1, d))


def _combine_kernel(dest_ref, dest_next_ref, wgt_ref, x1_ref, ys_ref, out_ref, buf_ref, sems):
    i = pl.program_id(0)
    n = pl.num_programs(0)
    tm = x1_ref.shape[0]
    slot = i % 2

    def gather(d_ref, into):
        def issue(t, c):
            for k in range(TOP_K):
                _row_copy(ys_ref, d_ref[k, t], buf_ref.at[into, k], t, sems.at[into]).start()
            return c

        lax.fori_loop(0, tm, issue, 0, unroll=4)

    @pl.when(i == 0)
    def _():
        gather(dest_ref, 0)

    @pl.when(i + 1 < n)
    def _():
        gather(dest_next_ref, 1 - slot)

    for k in range(TOP_K):
        _rows_wait(buf_ref.at[slot, k], tm, sems.at[slot])
    acc = x1_ref[...]
    for k in range(TOP_K):
        acc = acc + wgt_ref[:, k:k + 1] * buf_ref[slot, k]
    out_ref[...] = acc


def _combine(dest, wgt_t, x1, ys):
    t, d = x1.shape
    tm = ROUTE_TILE
    n = t // tm
    return pl.pallas_call(
        _combine_kernel,
        grid=(n,),
        in_specs=[pl.BlockSpec((TOP_K, tm), lambda i: (0, i), memory_space=pltpu.SMEM),
                  pl.BlockSpec((TOP_K, tm), lambda i: (0, jnp.minimum(i + 1, n - 1)), memory_space=pltpu.SMEM),
                  pl.BlockSpec((tm, TOP_K), lambda i: (i, 0)),
                  pl.BlockSpec((tm, d), lambda i: (i, 0)),
                  pl.BlockSpec(memory_space=pl.ANY)],
        out_specs=pl.BlockSpec((tm, d), lambda i: (i, 0)),
        out_shape=jax.ShapeDtypeStruct((t, d), F32),
        scratch_shapes=[pltpu.VMEM((2, TOP_K, tm) + ys.shape[1:], F32), pltpu.SemaphoreType.DMA((2,))],
        compiler_params=_params("arbitrary"),
    )(dest, dest, wgt_t, x1, ys)


def _moe(x1, hp, idx, rank, wgt, counts, w_gate_up, b_gate_up, w_down, b_down):
    t = x1.shape[0]
    tb = EXPERT_ROWS
    ne = w_gate_up.shape[0]
    n_blocks = -(-(t * TOP_K) // tb) + ne
    padded = ((counts + tb - 1) // tb) * tb
    pad_end = jnp.cumsum(padded)
    pad_start = pad_end - padded
    dest = rank + jnp.sum(jnp.where(idx[None] == jnp.arange(ne, dtype=jnp.int32)[:, None, None],
                                    pad_start.astype(jnp.int32)[:, None, None], 0), axis=0)
    block_start = jnp.arange(n_blocks, dtype=jnp.int32) * tb
    block_exp = jnp.minimum(jnp.sum((pad_end[None, :] <= block_start[:, None]).astype(jnp.int32), axis=1), ne - 1)
    n_used = (pad_end[-1:] // tb).astype(jnp.int32)
    xs = _dispatch(block_exp, n_used, dest, hp, n_blocks * tb)
    ys = _experts(block_exp, n_used, xs, w_gate_up, b_gate_up, w_down, b_down)
    return _combine(dest, wgt.T, x1, ys)


def kernel(x, norm1, w_in, q_norm, k_norm, lam_re, lam_im, log_dt, b_re, b_im, c_re, c_im, d_skip, w_glu, b_glu, w_attn_branch, w_ssm_branch, w_out, norm2, w_router, b_router, w_gate_up, b_gate_up, w_down, b_down):
    bsz, seq, d = x.shape
    t = bsz * seq
    x2 = x.reshape(t, d)
    for l in range(norm1.shape[0]):
        qkv0, qkv1, qkv2, u, gates = _in_proj(x2, norm1[l], w_in[l], q_norm[l], k_norm[l], seq)
        attn = [_attention(qkv0.reshape(bsz, seq, -1)), _attention(qkv1), _attention(qkv2)]
        z = _ssm(u, bsz, lam_re[l], lam_im[l], log_dt[l], b_re[l], b_im[l], c_re[l], c_im[l], d_skip[l])
        x1, hp, idx, rank, wgt, counts = _post_mix(
            x2, seq, attn, z, gates, w_attn_branch[l], w_glu[l], b_glu[l], w_ssm_branch[l], w_out[l], norm2[l],
            w_router[l], b_router[l])
        x2 = _moe(x1, hp, idx, rank, wgt, counts[:, 0], w_gate_up[l], b_gate_up[l], w_down[l], b_down[l])
    return x2.reshape(bsz, seq, d)
```

```python
import functools
import math

import numpy as np
import jax
import jax.numpy as jnp
from jax import lax
from jax.experimental import pallas as pl
from jax.experimental.pallas import tpu as pltpu

F32 = jnp.float32
BF16 = jnp.bfloat16

HEAD_DIM = 64
HEADS_PER_GROUP = 4
ATTN_GROUPS = ((128, 1), (512, 4), (2048, 16))
GROUP_WIDTH = HEADS_PER_GROUP * HEAD_DIM
ATTN_WIDTH = GROUP_WIDTH * len(ATTN_GROUPS)
BAND = 64
ROPE_THETA = 10000.0
SSM_GROUP = 16
SSM_STATE = 64
N_EXPERTS = 32
TOP_K = 4
SWIGLU_ALPHA = 1.702
SWIGLU_LIMIT = 7.0
NORM_EPS = 1e-6
MASK_VALUE = -1e30

ROW_TILE = 512
ATTN_Q_TILE = 512
ATTN_Q_SUB = 128
SSM_CHUNK = 64
SSM_POWER_ROWS = -(-(SSM_CHUNK + 1) // 8) * 8
DISPATCH_TILE = 512
ROUTE_TILE = 256
EXPERT_ROWS = 512
VMEM_LIMIT = 56 * 1024 * 1024
LANES = 128


def _params(*sem):
    return pltpu.CompilerParams(dimension_semantics=sem, vmem_limit_bytes=VMEM_LIMIT)


def _const_spec(shape):
    return pl.BlockSpec(shape, lambda *_: (0,) * len(shape), pipeline_mode=pl.Buffered(1))


def _permute_qk(a):
    lead = a.shape[:-1]
    n = len(lead)
    a = a.reshape(*lead, len(ATTN_GROUPS), HEADS_PER_GROUP, 2, HEAD_DIM // 2)
    return a.transpose(*range(n), n, n + 2, n + 1, n + 3).reshape(*lead, ATTN_WIDTH)


def _permute_group_lanes(piece):
    hd, hh = HEAD_DIM, HEAD_DIM // 2
    firsts = [piece[:, h * hd:h * hd + hh] for h in range(HEADS_PER_GROUP)]
    seconds = [piece[:, h * hd + hh:(h + 1) * hd] for h in range(HEADS_PER_GROUP)]
    return jnp.concatenate(firsts + seconds, axis=-1)


def _in_proj_kernel(x_ref, g1_ref, wf_ref, gq_ref, gk_ref, cos_ref, sin_ref, seg_ref,
                    qkv0_ref, qkv1_ref, qkv2_ref, u_ref, gate_ref, w_ref, scr_ref):
    @pl.when(pl.program_id(0) == 0)
    def _():
        rows_per_step = 128

        def cast_rows(i, _):
            rows = pl.ds(pl.multiple_of(i * rows_per_step, rows_per_step), rows_per_step)
            for c0 in range(0, 2 * ATTN_WIDTH, GROUP_WIDTH):
                w_ref[rows, c0:c0 + GROUP_WIDTH] = _permute_group_lanes(wf_ref[rows, c0:c0 + GROUP_WIDTH]).astype(BF16)
            for c0 in range(2 * ATTN_WIDTH, wf_ref.shape[1], GROUP_WIDTH):
                w_ref[rows, c0:c0 + GROUP_WIDTH] = wf_ref[rows, c0:c0 + GROUP_WIDTH].astype(BF16)
            return 0

        lax.fori_loop(0, wf_ref.shape[0] // rows_per_step, cast_rows, 0)

    x = x_ref[...]
    ms = jnp.mean(x * x, axis=-1, keepdims=True)
    h = (x * lax.rsqrt(ms + NORM_EPS) * g1_ref[...]).astype(BF16)
    cos = cos_ref[...]
    sin = sin_ref[...]
    seg = seg_ref[...]
    half = GROUP_WIDTH // 2
    tm = x.shape[0]

    def proj(c0, width):
        return jnp.dot(h, w_ref[:, c0:c0 + width], preferred_element_type=F32)

    def norm_rope(t, gain_ref, g):
        t1, t2 = t[:, :half], t[:, half:]
        ss = (t1 * t1 + t2 * t2).astype(BF16)
        tot = jnp.dot(ss, seg, preferred_element_type=F32)
        r = lax.rsqrt(tot * (1.0 / HEAD_DIM) + NORM_EPS)
        a = t1 * r * gain_ref[:, g * GROUP_WIDTH:g * GROUP_WIDTH + half]
        b = t2 * r * gain_ref[:, g * GROUP_WIDTH + half:(g + 1) * GROUP_WIDTH]
        return a * cos - b * sin, b * cos + a * sin

    outs = (qkv0_ref, qkv1_ref, qkv2_ref)
    for g, (_, dil) in enumerate(ATTN_GROUPS):
        q1, q2 = norm_rope(proj(g * GROUP_WIDTH, GROUP_WIDTH), gq_ref, g)
        k1, k2 = norm_rope(proj(ATTN_WIDTH + g * GROUP_WIDTH, GROUP_WIDTH), gk_ref, g)
        v = proj(2 * ATTN_WIDTH + g * GROUP_WIDTH, GROUP_WIDTH)
        qkv = jnp.concatenate([q1, q2, k1, k2, v], axis=-1)
        if dil == 1:
            outs[g][...] = qkv.astype(BF16)
        else:
            for c in range(qkv.shape[1] // LANES):
                scr_ref[c] = qkv[:, c * LANES:(c + 1) * LANES]
            for r in range(dil):
                for c in range(qkv.shape[1] // LANES):
                    outs[g][r, :, c * LANES:(c + 1) * LANES] = (
                        scr_ref[c, pl.ds(r, tm // dil, stride=dil), :].astype(BF16))
    o_u = 3 * ATTN_WIDTH
    u_w = u_ref.shape[-1]
    u_ref[...] = proj(o_u, u_w).astype(BF16)
    gate_w = gate_ref.shape[-1]
    for c in range(0, gate_w, 1024):
        gate_ref[:, c:c + 1024] = jax.nn.sigmoid(proj(o_u + u_w + c, 1024)).astype(BF16)


def _in_proj(x2, norm1, w_in, q_norm, k_norm, seq):
    t, d = x2.shape
    tm = ROW_TILE
    in_w = w_in.shape[1]
    ssm_w = d // 2
    gate_w = in_w - 3 * ATTN_WIDTH - ssm_w
    gq = (_permute_qk(q_norm.reshape(-1)) * (HEAD_DIM ** -0.5)).reshape(1, ATTN_WIDTH)
    gk = _permute_qk(k_norm.reshape(-1)).reshape(1, ATTN_WIDTH)
    inv_freq = ROPE_THETA ** (-jnp.arange(0, HEAD_DIM, 2, dtype=F32) / HEAD_DIM)
    ang = jnp.arange(seq, dtype=F32)[:, None] * inv_freq[None, :]
    cos = jnp.tile(jnp.cos(ang), (1, HEADS_PER_GROUP))
    sin = jnp.tile(jnp.sin(ang), (1, HEADS_PER_GROUP))
    lane = np.arange(GROUP_WIDTH // 2)
    seg = jnp.asarray(lane[:, None] // (HEAD_DIM // 2) == lane[None, :] // (HEAD_DIM // 2), BF16)
    bsz = t // seq
    tiles_per_seq = seq // tm
    qkv_w = 3 * GROUP_WIDTH
    out_shape = [jax.ShapeDtypeStruct((t, qkv_w), BF16)]
    out_specs = [pl.BlockSpec((tm, qkv_w), lambda i: (i, 0))]
    for _, dil in ATTN_GROUPS[1:]:
        out_shape.append(jax.ShapeDtypeStruct((bsz * dil, seq // dil, qkv_w), BF16))
        out_specs.append(pl.BlockSpec((dil, tm // dil, qkv_w),
                                      lambda i: (i // tiles_per_seq, i % tiles_per_seq, 0)))
    out_shape += [jax.ShapeDtypeStruct((t, ssm_w), BF16), jax.ShapeDtypeStruct((t, gate_w), BF16)]
    out_specs += [pl.BlockSpec((tm, ssm_w), lambda i: (i, 0)), pl.BlockSpec((tm, gate_w), lambda i: (i, 0))]
    return pl.pallas_call(
        _in_proj_kernel,
        grid=(t // tm,),
        in_specs=[
            pl.BlockSpec((tm, d), lambda i: (i, 0)),
            _const_spec((1, d)),
            _const_spec((d, in_w)),
            _const_spec((1, ATTN_WIDTH)),
            _const_spec((1, ATTN_WIDTH)),
            pl.BlockSpec((tm, GROUP_WIDTH // 2), lambda i: (i % tiles_per_seq, 0)),
            pl.BlockSpec((tm, GROUP_WIDTH // 2), lambda i: (i % tiles_per_seq, 0)),
            _const_spec((GROUP_WIDTH // 2, GROUP_WIDTH // 2)),
        ],
        out_specs=out_specs,
        out_shape=out_shape,
        scratch_shapes=[pltpu.VMEM((d, in_w), BF16), pltpu.VMEM((qkv_w // LANES, tm, LANES), F32)],
        compiler_params=_params("arbitrary"),
    )(x2, norm1.reshape(1, d), w_in, gq, gk, cos, sin, seg)


def _attn_kernel(c_ref, kl_ref, vl_ref, kr_ref, vr_ref, o_ref, lse_ref, kcat_ref, vcat_ref, *, length):
    tq = c_ref.shape[1]
    gw = GROUP_WIDTH
    j = pl.program_id(1)
    kcat_ref[0:BAND] = kl_ref[0]
    kcat_ref[BAND:BAND + tq] = c_ref[0, :, gw:2 * gw]
    kcat_ref[BAND + tq:] = kr_ref[0]
    vcat_ref[0:BAND] = vl_ref[0]
    vcat_ref[BAND:BAND + tq] = c_ref[0, :, 2 * gw:]
    vcat_ref[BAND + tq:] = vr_ref[0]
    sub = min(ATTN_Q_SUB, tq)
    nk = sub + 2 * BAND
    nh = HEADS_PER_GROUP
    row = lax.broadcasted_iota(jnp.int32, (nh * sub, nk), 0) & (sub - 1)
    col = lax.broadcasted_iota(jnp.int32, (nh * sub, nk), 1)
    in_band = jnp.abs(col - BAND - row) <= BAND
    q_lane = lax.broadcasted_iota(jnp.int32, (1, gw), 1)
    q_head = (q_lane % (gw // 2)) // (HEAD_DIM // 2)
    v_head = q_lane // HEAD_DIM
    for s in range(tq // sub):
        q_s = c_ref[0, s * sub:(s + 1) * sub, 0:gw]
        k_s = kcat_ref[s * sub:s * sub + nk]
        v_s = vcat_ref[s * sub:s * sub + nk]
        kpos = j * tq + s * sub - BAND + col
        valid = in_band & (kpos >= 0) & (kpos < length)
        q4 = jnp.concatenate([jnp.where(q_head == hh, q_s, jnp.zeros_like(q_s)) for hh in range(nh)], axis=0)
        sc = lax.dot_general(q4, k_s, (((1,), (1,)), ((), ())), preferred_element_type=F32)
        sc = jnp.where(valid, sc, MASK_VALUE)
        m = jnp.max(sc, axis=-1, keepdims=True)
        p = jnp.exp(sc - m)
        den = jnp.sum(p, axis=-1, keepdims=True)
        o4 = jnp.dot(p.astype(BF16), v_s, preferred_element_type=F32) * (1.0 / den)
        l4 = m + jnp.log(den)
        out = o4[0:sub]
        lse = jnp.broadcast_to(l4[0:sub], (sub, gw))
        for hh in range(1, nh):
            out = jnp.where(v_head == hh, o4[hh * sub:(hh + 1) * sub], out)
            lse = jnp.where(v_head == hh, l4[hh * sub:(hh + 1) * sub], lse)
        o_ref[0, s * sub:(s + 1) * sub] = out.astype(BF16)
        lse_ref[0, s * sub:(s + 1) * sub] = lse


def _attention(qkv):
    n, length, _ = qkv.shape
    tq = min(ATTN_Q_TILE, length)
    gw = GROUP_WIDTH
    hb = tq // BAND
    last = length // BAND - 1

    def left(lane_block):
        return pl.BlockSpec((1, BAND, gw), lambda b, j: (b, jnp.maximum(j * hb - 1, 0), lane_block))

    def right(lane_block):
        return pl.BlockSpec((1, BAND, gw), lambda b, j: (b, jnp.minimum((j + 1) * hb, last), lane_block))

    return pl.pallas_call(
        functools.partial(_attn_kernel, length=length),
        grid=(n, length // tq),
        in_specs=[pl.BlockSpec((1, tq, 3 * gw), lambda b, j: (b, j, 0)), left(1), left(2), right(1), right(2)],
        out_specs=[pl.BlockSpec((1, tq, gw), lambda b, j: (b, j, 0)),
                   pl.BlockSpec((1, tq, gw), lambda b, j: (b, j, 0))],
        out_shape=[jax.ShapeDtypeStruct((n, length, gw), BF16), jax.ShapeDtypeStruct((n, length, gw), F32)],
        scratch_shapes=[pltpu.VMEM((tq + 2 * BAND, gw), BF16), pltpu.VMEM((tq + 2 * BAND, gw), BF16)],
        compiler_params=_params("parallel", "parallel"),
    )(qkv, qkv, qkv, qkv, qkv)


def _ssm_operators(lam_re, lam_im, log_dt, b_re, b_im, c_re, c_im, d_skip):
    L = SSM_CHUNK
    G, P = lam_re.shape[1:]
    C = d_skip.shape[-1]
    hp = lax.Precision.HIGHEST
    n = jnp.arange(L + 1, dtype=F32)[:, None, None]
    pad = SSM_POWER_ROWS - (L + 1)
    kern, pw, fbt, ct, consts = [], [], [], [], []
    for direction in range(2):
        lr, li = lam_re[direction].astype(F32), lam_im[direction].astype(F32)
        dt = jnp.exp(log_dt[direction].astype(F32))[:, None]
        mag = jnp.exp(lr * dt)
        ab_re, ab_im = mag * jnp.cos(li * dt), mag * jnp.sin(li * dt)
        den = lr * lr + li * li
        nr = ab_re - 1.0
        f_re = (nr * lr + ab_im * li) / den
        f_im = (ab_im * lr - nr * li) / den
        pmag = jnp.exp(n * (lr * dt)[None])
        p_re, p_im = pmag * jnp.cos(n * (li * dt)[None]), pmag * jnp.sin(n * (li * dt)[None])
        br, bi = b_re[direction].astype(F32), b_im[direction].astype(F32)
        fb_re = f_re[..., None] * br - f_im[..., None] * bi
        fb_im = f_re[..., None] * bi + f_im[..., None] * br
        cr, ci = c_re[direction].astype(F32), c_im[direction].astype(F32)
        pr, pi = p_re[:L].transpose(1, 2, 0)[..., None], p_im[:L].transpose(1, 2, 0)[..., None]
        crt, cit = cr.transpose(0, 2, 1)[:, :, None, :], ci.transpose(0, 2, 1)[:, :, None, :]
        e_re = (pr * crt - pi * cit).reshape(G, P, L * C)
        e_im = (pr * cit + pi * crt).reshape(G, P, L * C)
        kern.append((jnp.einsum('gpc,gpx->gcx', fb_re, e_re, precision=hp)
                     - jnp.einsum('gpc,gpx->gcx', fb_im, e_im, precision=hp)).reshape(G, C, L, C))
        for p in (p_re, p_im):
            p = jnp.pad(p.transpose(1, 0, 2), ((0, 0), (0, pad), (0, 0)))
            pw.append(jnp.concatenate([p, p, p, p], axis=-1))
        tr, ti = fb_re.transpose(0, 2, 1), fb_im.transpose(0, 2, 1)
        fbt.append(jnp.concatenate([tr, ti, ti, tr], axis=-1))
        fbt.append(jnp.concatenate([-ti, tr, tr, -ti], axis=-1))
        ct.append(jnp.concatenate([cr, -ci], axis=-1))
        ct.append(jnp.concatenate([-ci, -cr], axis=-1))
        a_re, a_im = p_re[L], p_im[L]
        consts.append(jnp.concatenate([a_re, a_re, a_re, a_re], axis=-1))
        consts.append(jnp.concatenate([-a_im, a_im, a_im, -a_im], axis=-1))
    kf, kb = kern
    zero = kf[:, :, 0] + kb[:, :, 0] + jnp.eye(C, dtype=F32)[None] * d_skip.astype(F32)[:, None, :]
    kfull = jnp.concatenate([kb[:, :, :0:-1], zero[:, :, None], kf[:, :, 1:], jnp.zeros_like(zero)[:, :, None]],
                            axis=2)
    flat = kfull.astype(BF16).reshape(G, C, 2 * L * C)
    return (flat, jnp.stack(pw, axis=1), jnp.stack(fbt, axis=1), jnp.stack(ct, axis=1),
            jnp.stack(consts + consts, axis=1))


def _ssm_kernel(u_ref, flat_ref, pw_ref, fbt_ref, ct_ref, cst_ref, z_ref,
                toe_ref, wst_ref, vt_ref, f_ref, s_ref, *, bsz):
    u = u_ref[0]
    nc = u.shape[0]
    sw = cst_ref.shape[-1]
    hw = sw // 2
    C = fbt_ref.shape[2]
    L = toe_ref.shape[0] // C
    lc = L * C

    per_tile = LANES // C
    span = (2 * L - per_tile) * C
    for b in range(per_tile):
        shifted = flat_ref[0, :, (per_tile - 1 - b) * C:(per_tile - 1 - b) * C + span]
        for a in range(L // per_tile):
            t_in = a * per_tile + b
            off = (L // per_tile - 1 - a) * LANES
            toe_ref[t_in * C:(t_in + 1) * C, :] = shifted[:, off:off + lc]

    fa_f, fb_f, fa_b, fb_b = fbt_ref[0, 0], fbt_ref[0, 1], fbt_ref[0, 2], fbt_ref[0, 3]
    ca_f, cb_f, ca_b, cb_b = ct_ref[0, 0], ct_ref[0, 1], ct_ref[0, 2], ct_ref[0, 3]

    def power(k, n):
        return pw_ref[0, k, n:n + 1, :]

    for t in range(L):
        rows = slice(t * C, (t + 1) * C)
        wst_ref[rows, 0:sw] = (fa_f * power(0, L - 1 - t) + fb_f * power(1, L - 1 - t)).astype(BF16)
        wst_ref[rows, sw:2 * sw] = (fa_b * power(2, t) + fb_b * power(3, t)).astype(BF16)
        vt_ref[rows, 0:hw] = (ca_f * power(0, t + 1)[:, :hw] + cb_f * power(1, t + 1)[:, :hw]).astype(BF16)
        vt_ref[rows, hw:sw] = (ca_b * power(2, L - t)[:, :hw] + cb_b * power(3, L - t)[:, :hw]).astype(BF16)

    f_ref[...] = jnp.dot(u, wst_ref[...], preferred_element_type=F32)
    cst = cst_ref[0]
    a1f, a2f, a1b, a2b = cst[0:1], cst[1:2], cst[2:3], cst[3:4]
    per = nc // bsz

    def swap(v):
        return jnp.concatenate([v[:, hw:], v[:, :hw]], axis=-1)

    rows = 8

    def step(i, carry):
        new = []
        for b in range(bsz):
            sf, sb = carry[2 * b], carry[2 * b + 1]
            base_f = pl.multiple_of(b * per + i * rows, rows)
            base_b = pl.multiple_of(b * per + per - rows - i * rows, rows)
            f_blk = f_ref[pl.ds(base_f, rows), 0:sw]
            b_blk = f_ref[pl.ds(base_b, rows), sw:2 * sw]
            f_rows, b_rows = [], []
            for r in range(rows):
                f_rows.append(sf[:, :hw])
                sf = a1f * sf + a2f * swap(sf) + f_blk[r:r + 1]
                b_rows.append(sb[:, :hw])
                sb = a1b * sb + a2b * swap(sb) + b_blk[rows - 1 - r:rows - r]
            s_ref[pl.ds(base_f, rows), 0:hw] = jnp.concatenate(f_rows, axis=0)
            s_ref[pl.ds(base_b, rows), hw:sw] = jnp.concatenate(b_rows[::-1], axis=0)
            new += [sf, sb]
        return tuple(new)

    lax.fori_loop(0, per // rows, step, tuple(jnp.zeros((1, sw), F32) for _ in range(2 * bsz)))
    y = jnp.dot(u, toe_ref[...], preferred_element_type=F32)
    y = y + lax.dot_general(s_ref[...].astype(BF16), vt_ref[...], (((1,), (1,)), ((), ())),
                            preferred_element_type=F32)
    z_ref[0] = jax.nn.gelu(y).astype(BF16)


def _ssm(u, bsz, lam_re, lam_im, log_dt, b_re, b_im, c_re, c_im, d_skip):
    t, width = u.shape
    L, C = SSM_CHUNK, d_skip.shape[-1]
    G = width // C
    nc = t // L
    flat, pw, fbt, ct, consts = _ssm_operators(lam_re, lam_im, log_dt, b_re, b_im, c_re, c_im, d_skip)
    sw = consts.shape[-1]
    ug = u.reshape(nc, L, G, C).transpose(2, 0, 1, 3).reshape(G, nc, L * C)

    def group_spec(a):
        return pl.BlockSpec((1,) + a.shape[1:], lambda g: (g,) + (0,) * (a.ndim - 1))

    z = pl.pallas_call(
        functools.partial(_ssm_kernel, bsz=bsz),
        grid=(G,),
        in_specs=[group_spec(a) for a in (ug, flat, pw, fbt, ct, consts)],
        out_specs=pl.BlockSpec((1, nc, L * C), lambda g: (g, 0, 0)),
        out_shape=jax.ShapeDtypeStruct((G, nc, L * C), BF16),
        scratch_shapes=[pltpu.VMEM((L * C, L * C), BF16), pltpu.VMEM((L * C, 2 * sw), BF16),
                        pltpu.VMEM((L * C, sw), BF16), pltpu.VMEM((nc, 2 * sw), F32), pltpu.VMEM((nc, sw), F32)],
        compiler_params=_params("parallel"),
    )(ug, flat, pw, fbt, ct, consts)
    return z.reshape(G, nc, L, C).transpose(1, 2, 0, 3).reshape(t, width)


def _post_kernel(o0_ref, l0_ref, o1_ref, l1_ref, o2_ref, l2_ref, z_ref, gate_ref, x_ref,
                 wa_ref, wglu_ref, bglu_ref, ws_ref, wo_ref, g2_ref, wr_ref, wrlo_ref, br_ref, tri_ref,
                 x1_ref, hp_ref, idx_ref, rank_ref, wgt_ref, cnt_ref, o_scr, l_scr, carry_ref):
    tm, d = x_ref.shape

    @pl.when(pl.program_id(0) == 0)
    def _():
        carry_ref[...] = jnp.zeros_like(carry_ref)

    outs = [o0_ref[...].astype(F32)]
    lses = [l0_ref[...]]
    for (o_ref, l_ref), (_, dil) in zip(((o1_ref, l1_ref), (o2_ref, l2_ref)), ATTN_GROUPS[1:]):
        nlb = o_scr.shape[0]
        for r in range(dil):
            for c in range(nlb):
                o_scr[c, pl.ds(r, tm // dil, stride=dil), :] = o_ref[r, :, c * LANES:(c + 1) * LANES].astype(F32)
                l_scr[c, pl.ds(r, tm // dil, stride=dil), :] = l_ref[r, :, c * LANES:(c + 1) * LANES]
        outs.append(jnp.concatenate([o_scr[c] for c in range(nlb)], axis=-1))
        lses.append(jnp.concatenate([l_scr[c] for c in range(nlb)], axis=-1))
    top = jnp.maximum(jnp.maximum(lses[0], lses[1]), lses[2])
    es = [jnp.exp(l - top) for l in lses]
    y_attn = (es[0] * outs[0] + es[1] * outs[1] + es[2] * outs[2]) / (es[0] + es[1] + es[2])
    ya = jnp.dot(y_attn.astype(BF16), wa_ref[...], preferred_element_type=F32)
    z = z_ref[...]
    glu = jnp.dot(z, wglu_ref[...], preferred_element_type=F32) + bglu_ref[...]
    zz = z.astype(F32) * jax.nn.sigmoid(glu)
    ys = jnp.dot(zz.astype(BF16), ws_ref[...], preferred_element_type=F32)
    mixed = gate_ref[:, :d].astype(F32) * ya + gate_ref[:, d:].astype(F32) * ys
    x1 = x_ref[...] + jnp.dot(mixed.astype(BF16), wo_ref[...], preferred_element_type=F32)
    x1_ref[...] = x1
    ms = jnp.mean(x1 * x1, axis=-1, keepdims=True)
    h2 = x1 * lax.rsqrt(ms + NORM_EPS) * g2_ref[...]
    hb = h2.astype(BF16)
    hp_ref[...] = h2

    h_lo = (h2 - hb.astype(F32)).astype(BF16)
    nt = (((1,), (1,)), ((), ()))
    logits = (lax.dot_general(wr_ref[...], hb, nt, preferred_element_type=F32)
              + lax.dot_general(wrlo_ref[...], hb, nt, preferred_element_type=F32)
              + lax.dot_general(wr_ref[...], h_lo, nt, preferred_element_type=F32) + br_ref[...])
    ne = logits.shape[0]
    e_iota = lax.broadcasted_iota(jnp.int32, logits.shape, 0)
    vals, idxs = [], []
    work = logits
    for _ in range(TOP_K):
        m = jnp.max(work, axis=0, keepdims=True)
        i = jnp.min(jnp.where(work == m, e_iota, ne), axis=0, keepdims=True)
        vals.append(m)
        idxs.append(i)
        work = jnp.where(e_iota == i, -jnp.inf, work)
    ex = [jnp.exp(v - vals[0]) for v in vals]
    tot = ex[0] + ex[1] + ex[2] + ex[3]
    onehot = jnp.zeros(logits.shape, F32)
    for i in idxs:
        onehot = onehot + (e_iota == i).astype(F32)
    before = jnp.dot(onehot.astype(BF16), tri_ref[...], preferred_element_type=F32) + carry_ref[:, 0:1]
    for k in range(TOP_K):
        idx_ref[k:k + 1, :] = idxs[k]
        rank_ref[k:k + 1, :] = jnp.sum(jnp.where(e_iota == idxs[k], before, 0.0), axis=0,
                                       keepdims=True).astype(jnp.int32)
        wgt_ref[k:k + 1, :] = ex[k] / tot
    carry_ref[...] = carry_ref[...] + jnp.sum(onehot, axis=1, keepdims=True)
    cnt_ref[...] = carry_ref[...].astype(jnp.int32)


def _post_mix(x2, seq, attn, z, gates, w_attn_branch, w_glu, b_glu, w_ssm_branch, w_out, norm2, w_router, b_router):
    t, d = x2.shape
    tm = ROW_TILE
    gw = GROUP_WIDTH
    tiles_per_seq = seq // tm
    ne = w_router.shape[1]
    ssm_w = z.shape[1]
    in_specs, args = [], []
    for (o, lse), (_, dil) in zip(attn, ATTN_GROUPS):
        if dil == 1:
            spec = pl.BlockSpec((tm, gw), lambda i: (i, 0))
            o, lse = o.reshape(t, gw), lse.reshape(t, gw)
        else:
            spec = pl.BlockSpec((dil, tm // dil, gw), lambda i: (i // tiles_per_seq, i % tiles_per_seq, 0))
        in_specs += [spec, spec]
        args += [o, lse]
    tri = jnp.asarray(np.arange(tm)[:, None] < np.arange(tm)[None, :], BF16)
    in_specs += [pl.BlockSpec((tm, ssm_w), lambda i: (i, 0)),
                 pl.BlockSpec((tm, 2 * d), lambda i: (i, 0)),
                 pl.BlockSpec((tm, d), lambda i: (i, 0)),
                 _const_spec((gw, d)), _const_spec((ssm_w, ssm_w)), _const_spec((1, ssm_w)),
                 _const_spec((ssm_w, d)), _const_spec((d, d)), _const_spec((1, d)),
                 _const_spec((ne, d)), _const_spec((ne, d)), _const_spec((ne, 1)), _const_spec((tm, tm))]
    wr_t = w_router.T.astype(F32)
    wr_hi = wr_t.astype(BF16)
    wr_lo = (wr_t - wr_hi.astype(F32)).astype(BF16)
    args += [z, gates, x2, w_attn_branch.astype(BF16), w_glu.astype(BF16), b_glu.reshape(1, ssm_w),
             w_ssm_branch.astype(BF16), w_out.astype(BF16), norm2.reshape(1, d),
             wr_hi, wr_lo, b_router.reshape(ne, 1), tri]
    tok_spec = pl.BlockSpec((TOP_K, tm), lambda i: (0, i))
    return pl.pallas_call(
        _post_kernel,
        grid=(t // tm,),
        in_specs=in_specs,
        out_specs=[pl.BlockSpec((tm, d), lambda i: (i, 0)),
                   pl.BlockSpec((tm, d), lambda i: (i, 0)),
                   tok_spec, tok_spec, tok_spec, pl.BlockSpec((ne, LANES), lambda i: (0, 0))],
        out_shape=[jax.ShapeDtypeStruct((t, d), F32), jax.ShapeDtypeStruct((t, d), F32),
                   jax.ShapeDtypeStruct((TOP_K, t), jnp.int32), jax.ShapeDtypeStruct((TOP_K, t), jnp.int32),
                   jax.ShapeDtypeStruct((TOP_K, t), F32), jax.ShapeDtypeStruct((ne, LANES), jnp.int32)],
        scratch_shapes=[pltpu.VMEM((gw // LANES, tm, LANES), F32), pltpu.VMEM((gw // LANES, tm, LANES), F32),
                        pltpu.VMEM((ne, LANES), F32)],
        compiler_params=_params("arbitrary"),
    )(*args)


def _row_copy(src_ref, src_row, dst_ref, dst_row, sem):
    return pltpu.make_async_copy(src_ref.at[pl.ds(src_row, 1)], dst_ref.at[pl.ds(dst_row, 1)], sem)


def _rows_wait(ref, n_rows, sem):
    pltpu.make_async_copy(ref.at[pl.ds(0, n_rows)], ref.at[pl.ds(0, n_rows)], sem).wait()


def _dispatch_kernel(be_ref, nu_ref, dest_ref, h_ref, xs_ref, zero_ref, sems):
    i = pl.program_id(0)
    tm = dest_ref.shape[1]
    tb = zero_ref.shape[0]
    n_blocks = be_ref.shape[0]

    @pl.when(i == 0)
    def _():
        zero_ref[...] = jnp.zeros_like(zero_ref)

        def holds_padding(b):
            return (b >= nu_ref[0] - 1) | (be_ref[b] != be_ref[jnp.minimum(b + 1, n_blocks - 1)])

        def zero_copy(b):
            return pltpu.make_async_copy(zero_ref, xs_ref.at[pl.ds(pl.multiple_of(b * tb, tb), tb)], sems.at[1])

        def start(b, c):
            @pl.when(holds_padding(b))
            def _():
                zero_copy(b).start()
            return c

        def wait(b, c):
            @pl.when(holds_padding(b))
            def _():
                zero_copy(b).wait()
            return c

        lax.fori_loop(0, n_blocks, start, 0)
        lax.fori_loop(0, n_blocks, wait, 0)

    def issue(t, c):
        for k in range(TOP_K):
            _row_copy(h_ref, t, xs_ref, dest_ref[k, t], sems.at[0]).start()
        return c

    lax.fori_loop(0, tm, issue, 0, unroll=4)
    _rows_wait(xs_ref, TOP_K * tm, sems.at[0])


def _dispatch(block_exp, n_used, dest, hp, n_rows):
    t = hp.shape[0]
    slab = hp.shape[1:]
    tm = DISPATCH_TILE
    grid_spec = pltpu.PrefetchScalarGridSpec(
        num_scalar_prefetch=2,
        grid=(t // tm,),
        in_specs=[pl.BlockSpec((TOP_K, tm), lambda i, be, nu: (0, i), memory_space=pltpu.SMEM),
                  pl.BlockSpec((tm,) + slab, lambda i, be, nu: (i,) + (0,) * len(slab))],
        out_specs=pl.BlockSpec(memory_space=pl.ANY),
        scratch_shapes=[pltpu.VMEM((EXPERT_ROWS,) + slab, hp.dtype), pltpu.SemaphoreType.DMA((2,))],
    )
    return pl.pallas_call(
        _dispatch_kernel,
        grid_spec=grid_spec,
        out_shape=jax.ShapeDtypeStruct((n_rows,) + slab, hp.dtype),
        compiler_params=_params("arbitrary"),
    )(block_exp, n_used, dest, hp)


def _expert_kernel(be_ref, nu_ref, xs_ref, wgu_ref, bgu_ref, wd_ref, bd_ref, ys_ref, wgu_bf, wd_bf):
    i = pl.program_id(0)
    active = i < nu_ref[0]

    @pl.when(active & ((i == 0) | (be_ref[i] != be_ref[jnp.maximum(i - 1, 0)])))
    def _():
        wgu_bf[...] = wgu_ref[0].astype(BF16)
        wd_bf[...] = wd_ref[0].astype(BF16)

    @pl.when(active)
    def _():
        gu = jnp.dot(xs_ref[...].astype(BF16), wgu_bf[...], preferred_element_type=F32) + bgu_ref[0]
        de = gu.shape[1] // 2
        gate = jnp.minimum(gu[:, :de], SWIGLU_LIMIT)
        up = jnp.clip(gu[:, de:], -SWIGLU_LIMIT, SWIGLU_LIMIT)
        act = gate * jax.nn.sigmoid(SWIGLU_ALPHA * gate) * (up + 1.0)
        ys_ref[...] = jnp.dot(act.astype(BF16), wd_bf[...], preferred_element_type=F32) + bd_ref[0]

    @pl.when(i >= nu_ref[0])
    def _():
        ys_ref[...] = jnp.zeros_like(ys_ref)


def _experts(block_exp, n_used, xs, w_gate_up, b_gate_up, w_down, b_down):
    n_rows = xs.shape[0]
    ne, d, de2 = w_gate_up.shape
    tb = EXPERT_ROWS

    def row_block(i, be, nu):
        return (jnp.minimum(i, nu[0] - 1), 0)

    def expert_block(i, be, nu):
        return (be[jnp.minimum(i, nu[0] - 1)], 0, 0)

    grid_spec = pltpu.PrefetchScalarGridSpec(
        num_scalar_prefetch=2,
        grid=(n_rows // tb,),
        in_specs=[pl.BlockSpec((tb,) + xs.shape[1:], row_block),
                  pl.BlockSpec((1, d, de2), expert_block),
                  pl.BlockSpec((1, 1, de2), expert_block),
                  pl.BlockSpec((1, de2 // 2, d), expert_block),
                  pl.BlockSpec((1, 1, d), expert_block)],
        out_specs=pl.BlockSpec((tb, d), lambda i, be, nu: (i, 0)),
        scratch_shapes=[pltpu.VMEM((d, de2), BF16), pltpu.VMEM((de2 // 2, d), BF16)],
    )
    return pl.pallas_call(
        _expert_kernel,
        grid_spec=grid_spec,
        out_shape=jax.ShapeDtypeStruct((n_rows, d), F32),
        compiler_params=_params("arbitrary"),
    )(block_exp, n_used, xs, w_gate_up, b_gate_up.reshape(ne, 1, de2), w_down, b_down.reshape(ne, 1, d))


def _combine_kernel(dest_ref, dest_next_ref, wgt_ref, x1_ref, ys_ref, out_ref, buf_ref, sems):
    i = pl.program_id(0)
    n = pl.num_programs(0)
    tm = x1_ref.shape[0]
    slot = i % 2

    def gather(d_ref, into):
        def issue(t, c):
            for k in range(TOP_K):
                _row_copy(ys_ref, d_ref[k, t], buf_ref.at[into, k], t, sems.at[into]).start()
            return c

        lax.fori_loop(0, tm, issue, 0, unroll=4)

    @pl.when(i == 0)
    def _():
        gather(dest_ref, 0)

    @pl.when(i + 1 < n)
    def _():
        gather(dest_next_ref, 1 - slot)

    for k in range(TOP_K):
        _rows_wait(buf_ref.at[slot, k], tm, sems.at[slot])
    acc = x1_ref[...]
    for k in range(TOP_K):
        acc = acc + wgt_ref[:, k:k + 1] * buf_ref[slot, k]
    out_ref[...] = acc


def _combine(dest, wgt_t, x1, ys):
    t, d = x1.shape
    tm = ROUTE_TILE
    n = t // tm
    return pl.pallas_call(
        _combine_kernel,
        grid=(n,),
        in_specs=[pl.BlockSpec((TOP_K, tm), lambda i: (0, i), memory_space=pltpu.SMEM),
                  pl.BlockSpec((TOP_K, tm), lambda i: (0, jnp.minimum(i + 1, n - 1)), memory_space=pltpu.SMEM),
                  pl.BlockSpec((tm, TOP_K), lambda i: (i, 0)),
                  pl.BlockSpec((tm, d), lambda i: (i, 0)),
                  pl.BlockSpec(memory_space=pl.ANY)],
        out_specs=pl.BlockSpec((tm, d), lambda i: (i, 0)),
        out_shape=jax.ShapeDtypeStruct((t, d), F32),
        scratch_shapes=[pltpu.VMEM((2, TOP_K, tm) + ys.shape[1:], F32), pltpu.SemaphoreType.DMA((2,))],
        compiler_params=_params("arbitrary"),
    )(dest, dest, wgt_t, x1, ys)


def _moe(x1, hp, idx, rank, wgt, counts, w_gate_up, b_gate_up, w_down, b_down):
    t = x1.shape[0]
    tb = EXPERT_ROWS
    ne = w_gate_up.shape[0]
    n_blocks = -(-(t * TOP_K) // tb) + ne
    padded = ((counts + tb - 1) // tb) * tb
    pad_end = jnp.cumsum(padded)
    pad_start = pad_end - padded
    dest = rank + jnp.sum(jnp.where(idx[None] == jnp.arange(ne, dtype=jnp.int32)[:, None, None],
                                    pad_start.astype(jnp.int32)[:, None, None], 0), axis=0)
    block_start = jnp.arange(n_blocks, dtype=jnp.int32) * tb
    block_exp = jnp.minimum(jnp.sum((pad_end[None, :] <= block_start[:, None]).astype(jnp.int32), axis=1), ne - 1)
    n_used = (pad_end[-1:] // tb).astype(jnp.int32)
    xs = _dispatch(block_exp, n_used, dest, hp, n_blocks * tb)
    ys = _experts(block_exp, n_used, xs, w_gate_up, b_gate_up, w_down, b_down)
    return _combine(dest, wgt.T, x1, ys)


def kernel(x, norm1, w_in, q_norm, k_norm, lam_re, lam_im, log_dt, b_re, b_im, c_re, c_im, d_skip, w_glu, b_glu, w_attn_branch, w_ssm_branch, w_out, norm2, w_router, b_router, w_gate_up, b_gate_up, w_down, b_down):
    bsz, seq, d = x.shape
    t = bsz * seq
    x2 = x.reshape(t, d)
    for l in range(norm1.shape[0]):
        qkv0, qkv1, qkv2, u, gates = _in_proj(x2, norm1[l], w_in[l], q_norm[l], k_norm[l], seq)
        attn = [_attention(qkv0.reshape(bsz, seq, -1)), _attention(qkv1), _attention(qkv2)]
        z = _ssm(u, bsz, lam_re[l], lam_im[l], log_dt[l], b_re[l], b_im[l], c_re[l], c_im[l], d_skip[l])
        x1, hp, idx, rank, wgt, counts = _post_mix(
            x2, seq, attn, z, gates, w_attn_branch[l], w_glu[l], b_glu[l], w_ssm_branch[l], w_out[l], norm2[l],
            w_router[l], b_router[l])
        x2 = _moe(x1, hp, idx, rank, wgt, counts[:, 0], w_gate_up[l], b_gate_up[l], w_down[l], b_down[l])
    return x2.reshape(bsz, seq, d)
```

```python
import functools
import math

import numpy as np
import jax
import jax.numpy as jnp
from jax import lax
from jax.experimental import pallas as pl
from jax.experimental.pallas import tpu as pltpu

F32 = jnp.float32
BF16 = jnp.bfloat16

HEAD_DIM = 64
HEADS_PER_GROUP = 4
ATTN_GROUPS = ((128, 1), (512, 4), (2048, 16))
GROUP_WIDTH = HEADS_PER_GROUP * HEAD_DIM
ATTN_WIDTH = GROUP_WIDTH * len(ATTN_GROUPS)
BAND = 64
ROPE_THETA = 10000.0
SSM_GROUP = 16
SSM_STATE = 64
N_EXPERTS = 32
TOP_K = 4
SWIGLU_ALPHA = 1.702
SWIGLU_LIMIT = 7.0
NORM_EPS = 1e-6
MASK_VALUE = -1e30

ROW_TILE = 512
ATTN_Q_TILE = 512
ATTN_Q_SUB = 128
SSM_CHUNK = 64
SSM_POWER_ROWS = -(-(SSM_CHUNK + 1) // 8) * 8
DISPATCH_TILE = 1024
ROUTE_TILE = 512
EXPERT_ROWS = 512
VMEM_LIMIT = 56 * 1024 * 1024
LANES = 128


def _params(*sem):
    return pltpu.CompilerParams(dimension_semantics=sem, vmem_limit_bytes=VMEM_LIMIT)


def _const_spec(shape):
    return pl.BlockSpec(shape, lambda *_: (0,) * len(shape), pipeline_mode=pl.Buffered(1))


def _permute_qk(a):
    lead = a.shape[:-1]
    n = len(lead)
    a = a.reshape(*lead, len(ATTN_GROUPS), HEADS_PER_GROUP, 2, HEAD_DIM // 2)
    return a.transpose(*range(n), n, n + 2, n + 1, n + 3).reshape(*lead, ATTN_WIDTH)


def _permute_group_lanes(piece):
    hd, hh = HEAD_DIM, HEAD_DIM // 2
    firsts = [piece[:, h * hd:h * hd + hh] for h in range(HEADS_PER_GROUP)]
    seconds = [piece[:, h * hd + hh:(h + 1) * hd] for h in range(HEADS_PER_GROUP)]
    return jnp.concatenate(firsts + seconds, axis=-1)


def _in_proj_kernel(x_ref, g1_ref, wf_ref, gq_ref, gk_ref, cos_ref, sin_ref, seg_ref,
                    qkv0_ref, qkv1_ref, qkv2_ref, u_ref, gate_ref, w_ref, scr_ref):
    @pl.when(pl.program_id(0) == 0)
    def _():
        rows_per_step = 128

        def cast_rows(i, _):
            rows = pl.ds(pl.multiple_of(i * rows_per_step, rows_per_step), rows_per_step)
            for c0 in range(0, 2 * ATTN_WIDTH, GROUP_WIDTH):
                w_ref[rows, c0:c0 + GROUP_WIDTH] = _permute_group_lanes(wf_ref[rows, c0:c0 + GROUP_WIDTH]).astype(BF16)
            for c0 in range(2 * ATTN_WIDTH, wf_ref.shape[1], GROUP_WIDTH):
                w_ref[rows, c0:c0 + GROUP_WIDTH] = wf_ref[rows, c0:c0 + GROUP_WIDTH].astype(BF16)
            return 0

        lax.fori_loop(0, wf_ref.shape[0] // rows_per_step, cast_rows, 0)

    x = x_ref[...]
    ms = jnp.mean(x * x, axis=-1, keepdims=True)
    h = (x * lax.rsqrt(ms + NORM_EPS) * g1_ref[...]).astype(BF16)
    cos = cos_ref[...]
    sin = sin_ref[...]
    seg = seg_ref[...]
    half = GROUP_WIDTH // 2
    tm = x.shape[0]

    def proj(c0, width):
        return jnp.dot(h, w_ref[:, c0:c0 + width], preferred_element_type=F32)

    def sum_squares(t):
        return t[:, :half] * t[:, :half] + t[:, half:] * t[:, half:]

    def norm_rope(t, tot, gain_ref, g):
        t1, t2 = t[:, :half], t[:, half:]
        r = lax.rsqrt(tot * (1.0 / HEAD_DIM) + NORM_EPS)
        a = t1 * r * gain_ref[:, g * GROUP_WIDTH:g * GROUP_WIDTH + half]
        b = t2 * r * gain_ref[:, g * GROUP_WIDTH + half:(g + 1) * GROUP_WIDTH]
        return a * cos - b * sin, b * cos + a * sin

    outs = (qkv0_ref, qkv1_ref, qkv2_ref)
    for g, (_, dil) in enumerate(ATTN_GROUPS):
        tq = proj(g * GROUP_WIDTH, GROUP_WIDTH)
        tk = proj(ATTN_WIDTH + g * GROUP_WIDTH, GROUP_WIDTH)
        ss = jnp.concatenate([sum_squares(tq), sum_squares(tk)], axis=-1).astype(BF16)
        tot = jnp.dot(ss, seg, preferred_element_type=F32)
        q1, q2 = norm_rope(tq, tot[:, :half], gq_ref, g)
        k1, k2 = norm_rope(tk, tot[:, half:], gk_ref, g)
        v = proj(2 * ATTN_WIDTH + g * GROUP_WIDTH, GROUP_WIDTH)
        qkv = jnp.concatenate([q1, q2, k1, k2, v], axis=-1)
        if dil == 1:
            outs[g][...] = qkv.astype(BF16)
        else:
            for c in range(qkv.shape[1] // LANES):
                scr_ref[c] = qkv[:, c * LANES:(c + 1) * LANES]
            for r in range(dil):
                for c in range(qkv.shape[1] // LANES):
                    outs[g][r, :, c * LANES:(c + 1) * LANES] = (
                        scr_ref[c, pl.ds(r, tm // dil, stride=dil), :].astype(BF16))
    o_u = 3 * ATTN_WIDTH
    u_w = u_ref.shape[-1]
    u_ref[...] = proj(o_u, u_w).astype(BF16)
    gate_w = gate_ref.shape[-1]
    for c in range(0, gate_w, 1024):
        gate_ref[:, c:c + 1024] = jax.nn.sigmoid(proj(o_u + u_w + c, 1024)).astype(BF16)


def _in_proj(x2, norm1, w_in, q_norm, k_norm, seq):
    t, d = x2.shape
    tm = ROW_TILE
    in_w = w_in.shape[1]
    ssm_w = d // 2
    gate_w = in_w - 3 * ATTN_WIDTH - ssm_w
    gq = (_permute_qk(q_norm.reshape(-1)) * (HEAD_DIM ** -0.5)).reshape(1, ATTN_WIDTH)
    gk = _permute_qk(k_norm.reshape(-1)).reshape(1, ATTN_WIDTH)
    inv_freq = ROPE_THETA ** (-jnp.arange(0, HEAD_DIM, 2, dtype=F32) / HEAD_DIM)
    ang = jnp.arange(seq, dtype=F32)[:, None] * inv_freq[None, :]
    cos = jnp.tile(jnp.cos(ang), (1, HEADS_PER_GROUP))
    sin = jnp.tile(jnp.sin(ang), (1, HEADS_PER_GROUP))
    lane = np.arange(GROUP_WIDTH)
    seg = jnp.asarray(lane[:, None] // (HEAD_DIM // 2) == lane[None, :] // (HEAD_DIM // 2), BF16)
    bsz = t // seq
    tiles_per_seq = seq // tm
    qkv_w = 3 * GROUP_WIDTH
    out_shape = [jax.ShapeDtypeStruct((t, qkv_w), BF16)]
    out_specs = [pl.BlockSpec((tm, qkv_w), lambda i: (i, 0))]
    for _, dil in ATTN_GROUPS[1:]:
        out_shape.append(jax.ShapeDtypeStruct((bsz * dil, seq // dil, qkv_w), BF16))
        out_specs.append(pl.BlockSpec((dil, tm // dil, qkv_w),
                                      lambda i: (i // tiles_per_seq, i % tiles_per_seq, 0)))
    out_shape += [jax.ShapeDtypeStruct((t, ssm_w), BF16), jax.ShapeDtypeStruct((t, gate_w), BF16)]
    out_specs += [pl.BlockSpec((tm, ssm_w), lambda i: (i, 0)), pl.BlockSpec((tm, gate_w), lambda i: (i, 0))]
    return pl.pallas_call(
        _in_proj_kernel,
        grid=(t // tm,),
        in_specs=[
            pl.BlockSpec((tm, d), lambda i: (i, 0)),
            _const_spec((1, d)),
            _const_spec((d, in_w)),
            _const_spec((1, ATTN_WIDTH)),
            _const_spec((1, ATTN_WIDTH)),
            pl.BlockSpec((tm, GROUP_WIDTH // 2), lambda i: (i % tiles_per_seq, 0)),
            pl.BlockSpec((tm, GROUP_WIDTH // 2), lambda i: (i % tiles_per_seq, 0)),
            _const_spec((GROUP_WIDTH, GROUP_WIDTH)),
        ],
        out_specs=out_specs,
        out_shape=out_shape,
        scratch_shapes=[pltpu.VMEM((d, in_w), BF16), pltpu.VMEM((qkv_w // LANES, tm, LANES), F32)],
        compiler_params=_params("arbitrary"),
    )(x2, norm1.reshape(1, d), w_in, gq, gk, cos, sin, seg)


def _attn_kernel(c_ref, kl_ref, vl_ref, kr_ref, vr_ref, o_ref, lse_ref, kcat_ref, vcat_ref, *, length):
    tq = c_ref.shape[1]
    gw = GROUP_WIDTH
    j = pl.program_id(1)
    kcat_ref[0:BAND] = kl_ref[0]
    kcat_ref[BAND:BAND + tq] = c_ref[0, :, gw:2 * gw]
    kcat_ref[BAND + tq:] = kr_ref[0]
    vcat_ref[0:BAND] = vl_ref[0]
    vcat_ref[BAND:BAND + tq] = c_ref[0, :, 2 * gw:]
    vcat_ref[BAND + tq:] = vr_ref[0]
    sub = min(ATTN_Q_SUB, tq)
    nk = sub + 2 * BAND
    nh = HEADS_PER_GROUP
    row = lax.broadcasted_iota(jnp.int32, (nh * sub, nk), 0) & (sub - 1)
    col = lax.broadcasted_iota(jnp.int32, (nh * sub, nk), 1)
    in_band = jnp.abs(col - BAND - row) <= BAND
    q_lane = lax.broadcasted_iota(jnp.int32, (1, gw), 1)
    q_head = (q_lane % (gw // 2)) // (HEAD_DIM // 2)
    v_head = q_lane // HEAD_DIM
    for s in range(tq // sub):
        q_s = c_ref[0, s * sub:(s + 1) * sub, 0:gw]
        k_s = kcat_ref[s * sub:s * sub + nk]
        v_s = vcat_ref[s * sub:s * sub + nk]
        kpos = j * tq + s * sub - BAND + col
        valid = in_band & (kpos >= 0) & (kpos < length)
        q4 = jnp.concatenate([jnp.where(q_head == hh, q_s, jnp.zeros_like(q_s)) for hh in range(nh)], axis=0)
        sc = lax.dot_general(q4, k_s, (((1,), (1,)), ((), ())), preferred_element_type=F32)
        sc = jnp.where(valid, sc, MASK_VALUE)
        m = jnp.max(sc, axis=-1, keepdims=True)
        p = jnp.exp(sc - m)
        den = jnp.sum(p, axis=-1, keepdims=True)
        o4 = jnp.dot(p.astype(BF16), v_s, preferred_element_type=F32) * (1.0 / den)
        l4 = m + jnp.log(den)
        out = o4[0:sub]
        lse = jnp.broadcast_to(l4[0:sub], (sub, gw))
        for hh in range(1, nh):
            out = jnp.where(v_head == hh, o4[hh * sub:(hh + 1) * sub], out)
            lse = jnp.where(v_head == hh, l4[hh * sub:(hh + 1) * sub], lse)
        o_ref[0, s * sub:(s + 1) * sub] = out.astype(BF16)
        lse_ref[0, s * sub:(s + 1) * sub] = lse


def _attention(qkv):
    n, length, _ = qkv.shape
    tq = min(ATTN_Q_TILE, length)
    gw = GROUP_WIDTH
    hb = tq // BAND
    last = length // BAND - 1

    def left(lane_block):
        return pl.BlockSpec((1, BAND, gw), lambda b, j: (b, jnp.maximum(j * hb - 1, 0), lane_block))

    def right(lane_block):
        return pl.BlockSpec((1, BAND, gw), lambda b, j: (b, jnp.minimum((j + 1) * hb, last), lane_block))

    return pl.pallas_call(
        functools.partial(_attn_kernel, length=length),
        grid=(n, length // tq),
        in_specs=[pl.BlockSpec((1, tq, 3 * gw), lambda b, j: (b, j, 0)), left(1), left(2), right(1), right(2)],
        out_specs=[pl.BlockSpec((1, tq, gw), lambda b, j: (b, j, 0)),
                   pl.BlockSpec((1, tq, gw), lambda b, j: (b, j, 0))],
        out_shape=[jax.ShapeDtypeStruct((n, length, gw), BF16), jax.ShapeDtypeStruct((n, length, gw), F32)],
        scratch_shapes=[pltpu.VMEM((tq + 2 * BAND, gw), BF16), pltpu.VMEM((tq + 2 * BAND, gw), BF16)],
        compiler_params=_params("parallel", "parallel"),
    )(qkv, qkv, qkv, qkv, qkv)


def _ssm_operators(lam_re, lam_im, log_dt, b_re, b_im, c_re, c_im, d_skip):
    L = SSM_CHUNK
    G, P = lam_re.shape[1:]
    C = d_skip.shape[-1]
    hp = lax.Precision.HIGHEST
    n = jnp.arange(L + 1, dtype=F32)[:, None, None]
    pad = SSM_POWER_ROWS - (L + 1)
    kern, pw, fbt, ct, consts = [], [], [], [], []
    for direction in range(2):
        lr, li = lam_re[direction].astype(F32), lam_im[direction].astype(F32)
        dt = jnp.exp(log_dt[direction].astype(F32))[:, None]
        mag = jnp.exp(lr * dt)
        ab_re, ab_im = mag * jnp.cos(li * dt), mag * jnp.sin(li * dt)
        den = lr * lr + li * li
        nr = ab_re - 1.0
        f_re = (nr * lr + ab_im * li) / den
        f_im = (ab_im * lr - nr * li) / den
        pmag = jnp.exp(n * (lr * dt)[None])
        p_re, p_im = pmag * jnp.cos(n * (li * dt)[None]), pmag * jnp.sin(n * (li * dt)[None])
        br, bi = b_re[direction].astype(F32), b_im[direction].astype(F32)
        fb_re = f_re[..., None] * br - f_im[..., None] * bi
        fb_im = f_re[..., None] * bi + f_im[..., None] * br
        cr, ci = c_re[direction].astype(F32), c_im[direction].astype(F32)
        pr, pi = p_re[:L].transpose(1, 2, 0)[..., None], p_im[:L].transpose(1, 2, 0)[..., None]
        crt, cit = cr.transpose(0, 2, 1)[:, :, None, :], ci.transpose(0, 2, 1)[:, :, None, :]
        e_re = (pr * crt - pi * cit).reshape(G, P, L * C)
        e_im = (pr * cit + pi * crt).reshape(G, P, L * C)
        kern.append((jnp.einsum('gpc,gpx->gcx', fb_re, e_re, precision=hp)
                     - jnp.einsum('gpc,gpx->gcx', fb_im, e_im, precision=hp)).reshape(G, C, L, C))
        for p in (p_re, p_im):
            p = jnp.pad(p.transpose(1, 0, 2), ((0, 0), (0, pad), (0, 0)))
            pw.append(jnp.concatenate([p, p, p, p], axis=-1))
        tr, ti = fb_re.transpose(0, 2, 1), fb_im.transpose(0, 2, 1)
        fbt.append(jnp.concatenate([tr, ti, ti, tr], axis=-1))
        fbt.append(jnp.concatenate([-ti, tr, tr, -ti], axis=-1))
        ct.append(jnp.concatenate([cr, -ci], axis=-1))
        ct.append(jnp.concatenate([-ci, -cr], axis=-1))
        a_re, a_im = p_re[L], p_im[L]
        consts.append(jnp.concatenate([a_re, a_re, a_re, a_re], axis=-1))
        consts.append(jnp.concatenate([-a_im, a_im, a_im, -a_im], axis=-1))
    kf, kb = kern
    zero = kf[:, :, 0] + kb[:, :, 0] + jnp.eye(C, dtype=F32)[None] * d_skip.astype(F32)[:, None, :]
    kfull = jnp.concatenate([kb[:, :, :0:-1], zero[:, :, None], kf[:, :, 1:], jnp.zeros_like(zero)[:, :, None]],
                            axis=2)
    flat = kfull.astype(BF16).reshape(G, C, 2 * L * C)
    return (flat, jnp.stack(pw, axis=1), jnp.stack(fbt, axis=1), jnp.stack(ct, axis=1),
            jnp.stack(consts + consts, axis=1))


def _ssm_kernel(u_ref, flat_ref, pw_ref, fbt_ref, ct_ref, cst_ref, z_ref,
                toe_ref, wst_ref, vt_ref, f_ref, s_ref, *, bsz):
    u = u_ref[0]
    nc = u.shape[0]
    sw = cst_ref.shape[-1]
    hw = sw // 2
    C = fbt_ref.shape[2]
    L = toe_ref.shape[0] // C
    lc = L * C

    per_tile = LANES // C
    span = (2 * L - per_tile) * C
    for b in range(per_tile):
        shifted = flat_ref[0, :, (per_tile - 1 - b) * C:(per_tile - 1 - b) * C + span]
        for a in range(L // per_tile):
            t_in = a * per_tile + b
            off = (L // per_tile - 1 - a) * LANES
            toe_ref[t_in * C:(t_in + 1) * C, :] = shifted[:, off:off + lc]

    fa_f, fb_f, fa_b, fb_b = fbt_ref[0, 0], fbt_ref[0, 1], fbt_ref[0, 2], fbt_ref[0, 3]
    ca_f, cb_f, ca_b, cb_b = ct_ref[0, 0], ct_ref[0, 1], ct_ref[0, 2], ct_ref[0, 3]

    def power(k, n):
        return pw_ref[0, k, n:n + 1, :]

    for t in range(L):
        rows = slice(t * C, (t + 1) * C)
        wst_ref[rows, 0:sw] = (fa_f * power(0, L - 1 - t) + fb_f * power(1, L - 1 - t)).astype(BF16)
        wst_ref[rows, sw:2 * sw] = (fa_b * power(2, t) + fb_b * power(3, t)).astype(BF16)
        vt_ref[rows, 0:hw] = (ca_f * power(0, t + 1)[:, :hw] + cb_f * power(1, t + 1)[:, :hw]).astype(BF16)
        vt_ref[rows, hw:sw] = (ca_b * power(2, L - t)[:, :hw] + cb_b * power(3, L - t)[:, :hw]).astype(BF16)

    f_ref[...] = jnp.dot(u, wst_ref[...], preferred_element_type=F32)
    cst = cst_ref[0]
    a1f, a2f, a1b, a2b = cst[0:1], cst[1:2], cst[2:3], cst[3:4]
    per = nc // bsz

    def swap(v):
        return jnp.concatenate([v[:, hw:], v[:, :hw]], axis=-1)

    rows = 8

    def step(i, carry):
        new = []
        for b in range(bsz):
            sf, sb = carry[2 * b], carry[2 * b + 1]
            base_f = pl.multiple_of(b * per + i * rows, rows)
            base_b = pl.multiple_of(b * per + per - rows - i * rows, rows)
            f_blk = f_ref[pl.ds(base_f, rows), 0:sw]
            b_blk = f_ref[pl.ds(base_b, rows), sw:2 * sw]
            f_rows, b_rows = [], []
            for r in range(rows):
                f_rows.append(sf[:, :hw])
                sf = a1f * sf + a2f * swap(sf) + f_blk[r:r + 1]
                b_rows.append(sb[:, :hw])
                sb = a1b * sb + a2b * swap(sb) + b_blk[rows - 1 - r:rows - r]
            s_ref[pl.ds(base_f, rows), 0:hw] = jnp.concatenate(f_rows, axis=0)
            s_ref[pl.ds(base_b, rows), hw:sw] = jnp.concatenate(b_rows[::-1], axis=0)
            new += [sf, sb]
        return tuple(new)

    lax.fori_loop(0, per // rows, step, tuple(jnp.zeros((1, sw), F32) for _ in range(2 * bsz)))
    y = jnp.dot(u, toe_ref[...], preferred_element_type=F32)
    y = y + lax.dot_general(s_ref[...].astype(BF16), vt_ref[...], (((1,), (1,)), ((), ())),
                            preferred_element_type=F32)
    z_ref[0] = jax.nn.gelu(y).astype(BF16)


def _ssm(u, bsz, lam_re, lam_im, log_dt, b_re, b_im, c_re, c_im, d_skip):
    t, width = u.shape
    L, C = SSM_CHUNK, d_skip.shape[-1]
    G = width // C
    nc = t // L
    flat, pw, fbt, ct, consts = _ssm_operators(lam_re, lam_im, log_dt, b_re, b_im, c_re, c_im, d_skip)
    sw = consts.shape[-1]
    ug = u.reshape(nc, L, G, C).transpose(2, 0, 1, 3).reshape(G, nc, L * C)

    def group_spec(a):
        return pl.BlockSpec((1,) + a.shape[1:], lambda g: (g,) + (0,) * (a.ndim - 1))

    z = pl.pallas_call(
        functools.partial(_ssm_kernel, bsz=bsz),
        grid=(G,),
        in_specs=[group_spec(a) for a in (ug, flat, pw, fbt, ct, consts)],
        out_specs=pl.BlockSpec((1, nc, L * C), lambda g: (g, 0, 0)),
        out_shape=jax.ShapeDtypeStruct((G, nc, L * C), BF16),
        scratch_shapes=[pltpu.VMEM((L * C, L * C), BF16), pltpu.VMEM((L * C, 2 * sw), BF16),
                        pltpu.VMEM((L * C, sw), BF16), pltpu.VMEM((nc, 2 * sw), F32), pltpu.VMEM((nc, sw), F32)],
        compiler_params=_params("parallel"),
    )(ug, flat, pw, fbt, ct, consts)
    return z.reshape(G, nc, L, C).transpose(1, 2, 0, 3).reshape(t, width)


def _post_kernel(o0_ref, l0_ref, o1_ref, l1_ref, o2_ref, l2_ref, z_ref, gate_ref, x_ref,
                 wa_ref, wglu_ref, bglu_ref, ws_ref, wo_ref, g2_ref, wr_ref, wrlo_ref, br_ref, tri_ref,
                 x1_ref, hp_ref, idx_ref, rank_ref, wgt_ref, cnt_ref, o_scr, l_scr, carry_ref):
    tm, d = x_ref.shape

    @pl.when(pl.program_id(0) == 0)
    def _():
        carry_ref[...] = jnp.zeros_like(carry_ref)

    outs = [o0_ref[...].astype(F32)]
    lses = [l0_ref[...]]
    for (o_ref, l_ref), (_, dil) in zip(((o1_ref, l1_ref), (o2_ref, l2_ref)), ATTN_GROUPS[1:]):
        nlb = o_scr.shape[0]
        for r in range(dil):
            for c in range(nlb):
                o_scr[c, pl.ds(r, tm // dil, stride=dil), :] = o_ref[r, :, c * LANES:(c + 1) * LANES].astype(F32)
                l_scr[c, pl.ds(r, tm // dil, stride=dil), :] = l_ref[r, :, c * LANES:(c + 1) * LANES]
        outs.append(jnp.concatenate([o_scr[c] for c in range(nlb)], axis=-1))
        lses.append(jnp.concatenate([l_scr[c] for c in range(nlb)], axis=-1))
    top = jnp.maximum(jnp.maximum(lses[0], lses[1]), lses[2])
    es = [jnp.exp(l - top) for l in lses]
    y_attn = (es[0] * outs[0] + es[1] * outs[1] + es[2] * outs[2]) / (es[0] + es[1] + es[2])
    ya = jnp.dot(y_attn.astype(BF16), wa_ref[...], preferred_element_type=F32)
    z = z_ref[...]
    glu = jnp.dot(z, wglu_ref[...], preferred_element_type=F32) + bglu_ref[...]
    zz = z.astype(F32) * jax.nn.sigmoid(glu)
    ys = jnp.dot(zz.astype(BF16), ws_ref[...], preferred_element_type=F32)
    mixed = gate_ref[:, :d].astype(F32) * ya + gate_ref[:, d:].astype(F32) * ys
    x1 = x_ref[...] + jnp.dot(mixed.astype(BF16), wo_ref[...], preferred_element_type=F32)
    x1_ref[...] = x1
    ms = jnp.mean(x1 * x1, axis=-1, keepdims=True)
    h2 = x1 * lax.rsqrt(ms + NORM_EPS) * g2_ref[...]
    hb = h2.astype(BF16)
    hp_ref[...] = h2

    h_lo = (h2 - hb.astype(F32)).astype(BF16)
    nt = (((1,), (1,)), ((), ()))
    logits = (lax.dot_general(wr_ref[...], hb, nt, preferred_element_type=F32)
              + lax.dot_general(wrlo_ref[...], hb, nt, preferred_element_type=F32)
              + lax.dot_general(wr_ref[...], h_lo, nt, preferred_element_type=F32) + br_ref[...])
    ne = logits.shape[0]
    e_iota = lax.broadcasted_iota(jnp.int32, logits.shape, 0)
    vals, idxs = [], []
    work = logits
    for _ in range(TOP_K):
        m = jnp.max(work, axis=0, keepdims=True)
        i = jnp.min(jnp.where(work == m, e_iota, ne), axis=0, keepdims=True)
        vals.append(m)
        idxs.append(i)
        work = jnp.where(e_iota == i, -jnp.inf, work)
    ex = [jnp.exp(v - vals[0]) for v in vals]
    tot = ex[0] + ex[1] + ex[2] + ex[3]
    onehot = jnp.zeros(logits.shape, F32)
    for i in idxs:
        onehot = onehot + (e_iota == i).astype(F32)
    before = jnp.dot(onehot.astype(BF16), tri_ref[...], preferred_element_type=F32) + carry_ref[:, 0:1]
    for k in range(TOP_K):
        idx_ref[k:k + 1, :] = idxs[k]
        rank_ref[k:k + 1, :] = jnp.sum(jnp.where(e_iota == idxs[k], before, 0.0), axis=0,
                                       keepdims=True).astype(jnp.int32)
        wgt_ref[k:k + 1, :] = ex[k] / tot
    carry_ref[...] = carry_ref[...] + jnp.sum(onehot, axis=1, keepdims=True)
    cnt_ref[...] = carry_ref[...].astype(jnp.int32)


def _post_mix(x2, seq, attn, z, gates, w_attn_branch, w_glu, b_glu, w_ssm_branch, w_out, norm2, w_router, b_router):
    t, d = x2.shape
    tm = ROW_TILE
    gw = GROUP_WIDTH
    tiles_per_seq = seq // tm
    ne = w_router.shape[1]
    ssm_w = z.shape[1]
    in_specs, args = [], []
    for (o, lse), (_, dil) in zip(attn, ATTN_GROUPS):
        if dil == 1:
            spec = pl.BlockSpec((tm, gw), lambda i: (i, 0))
            o, lse = o.reshape(t, gw), lse.reshape(t, gw)
        else:
            spec = pl.BlockSpec((dil, tm // dil, gw), lambda i: (i // tiles_per_seq, i % tiles_per_seq, 0))
        in_specs += [spec, spec]
        args += [o, lse]
    tri = jnp.asarray(np.arange(tm)[:, None] < np.arange(tm)[None, :], BF16)
    in_specs += [pl.BlockSpec((tm, ssm_w), lambda i: (i, 0)),
                 pl.BlockSpec((tm, 2 * d), lambda i: (i, 0)),
                 pl.BlockSpec((tm, d), lambda i: (i, 0)),
                 _const_spec((gw, d)), _const_spec((ssm_w, ssm_w)), _const_spec((1, ssm_w)),
                 _const_spec((ssm_w, d)), _const_spec((d, d)), _const_spec((1, d)),
                 _const_spec((ne, d)), _const_spec((ne, d)), _const_spec((ne, 1)), _const_spec((tm, tm))]
    wr_t = w_router.T.astype(F32)
    wr_hi = wr_t.astype(BF16)
    wr_lo = (wr_t - wr_hi.astype(F32)).astype(BF16)
    args += [z, gates, x2, w_attn_branch.astype(BF16), w_glu.astype(BF16), b_glu.reshape(1, ssm_w),
             w_ssm_branch.astype(BF16), w_out.astype(BF16), norm2.reshape(1, d),
             wr_hi, wr_lo, b_router.reshape(ne, 1), tri]
    tok_spec = pl.BlockSpec((TOP_K, tm), lambda i: (0, i))
    return pl.pallas_call(
        _post_kernel,
        grid=(t // tm,),
        in_specs=in_specs,
        out_specs=[pl.BlockSpec((tm, d), lambda i: (i, 0)),
                   pl.BlockSpec((tm, d), lambda i: (i, 0)),
                   tok_spec, tok_spec, tok_spec, pl.BlockSpec((ne, LANES), lambda i: (0, 0))],
        out_shape=[jax.ShapeDtypeStruct((t, d), F32), jax.ShapeDtypeStruct((t, d), F32),
                   jax.ShapeDtypeStruct((TOP_K, t), jnp.int32), jax.ShapeDtypeStruct((TOP_K, t), jnp.int32),
                   jax.ShapeDtypeStruct((TOP_K, t), F32), jax.ShapeDtypeStruct((ne, LANES), jnp.int32)],
        scratch_shapes=[pltpu.VMEM((gw // LANES, tm, LANES), F32), pltpu.VMEM((gw // LANES, tm, LANES), F32),
                        pltpu.VMEM((ne, LANES), F32)],
        compiler_params=_params("arbitrary"),
    )(*args)


def _row_copy(src_ref, src_row, dst_ref, dst_row, sem):
    return pltpu.make_async_copy(src_ref.at[pl.ds(src_row, 1)], dst_ref.at[pl.ds(dst_row, 1)], sem)


def _rows_wait(ref, n_rows, sem):
    pltpu.make_async_copy(ref.at[pl.ds(0, n_rows)], ref.at[pl.ds(0, n_rows)], sem).wait()


def _dispatch_kernel(be_ref, nu_ref, dest_ref, h_ref, xs_ref, zero_ref, sems):
    i = pl.program_id(0)
    tm = dest_ref.shape[1]
    tb = zero_ref.shape[0]
    n_blocks = be_ref.shape[0]

    @pl.when(i == 0)
    def _():
        zero_ref[...] = jnp.zeros_like(zero_ref)

        def holds_padding(b):
            return (b >= nu_ref[0] - 1) | (be_ref[b] != be_ref[jnp.minimum(b + 1, n_blocks - 1)])

        def zero_copy(b):
            return pltpu.make_async_copy(zero_ref, xs_ref.at[pl.ds(pl.multiple_of(b * tb, tb), tb)], sems.at[1])

        def start(b, c):
            @pl.when(holds_padding(b))
            def _():
                zero_copy(b).start()
            return c

        def wait(b, c):
            @pl.when(holds_padding(b))
            def _():
                zero_copy(b).wait()
            return c

        lax.fori_loop(0, n_blocks, start, 0)
        lax.fori_loop(0, n_blocks, wait, 0)

    def issue(t, c):
        for k in range(TOP_K):
            _row_copy(h_ref, t, xs_ref, dest_ref[k, t], sems.at[0]).start()
        return c

    lax.fori_loop(0, tm, issue, 0, unroll=4)
    _rows_wait(xs_ref, TOP_K * tm, sems.at[0])


def _dispatch(block_exp, n_used, dest, hp, n_rows):
    t = hp.shape[0]
    slab = hp.shape[1:]
    tm = DISPATCH_TILE
    grid_spec = pltpu.PrefetchScalarGridSpec(
        num_scalar_prefetch=2,
        grid=(t // tm,),
        in_specs=[pl.BlockSpec((TOP_K, tm), lambda i, be, nu: (0, i), memory_space=pltpu.SMEM),
                  pl.BlockSpec((tm,) + slab, lambda i, be, nu: (i,) + (0,) * len(slab))],
        out_specs=pl.BlockSpec(memory_space=pl.ANY),
        scratch_shapes=[pltpu.VMEM((EXPERT_ROWS,) + slab, hp.dtype), pltpu.SemaphoreType.DMA((2,))],
    )
    return pl.pallas_call(
        _dispatch_kernel,
        grid_spec=grid_spec,
        out_shape=jax.ShapeDtypeStruct((n_rows,) + slab, hp.dtype),
        compiler_params=_params("arbitrary"),
    )(block_exp, n_used, dest, hp)


def _expert_kernel(be_ref, nu_ref, xs_ref, wgu_ref, bgu_ref, wd_ref, bd_ref, ys_ref, wgu_bf, wd_bf):
    i = pl.program_id(0)
    active = i < nu_ref[0]

    @pl.when(active & ((i == 0) | (be_ref[i] != be_ref[jnp.maximum(i - 1, 0)])))
    def _():
        wgu_bf[...] = wgu_ref[0].astype(BF16)
        wd_bf[...] = wd_ref[0].astype(BF16)

    @pl.when(active)
    def _():
        gu = jnp.dot(xs_ref[...].astype(BF16), wgu_bf[...], preferred_element_type=F32) + bgu_ref[0]
        de = gu.shape[1] // 2
        gate = jnp.minimum(gu[:, :de], SWIGLU_LIMIT)
        up = jnp.clip(gu[:, de:], -SWIGLU_LIMIT, SWIGLU_LIMIT)
        act = gate * jax.nn.sigmoid(SWIGLU_ALPHA * gate) * (up + 1.0)
        ys_ref[...] = jnp.dot(act.astype(BF16), wd_bf[...], preferred_element_type=F32) + bd_ref[0]

    @pl.when(i >= nu_ref[0])
    def _():
        ys_ref[...] = jnp.zeros_like(ys_ref)


def _experts(block_exp, n_used, xs, w_gate_up, b_gate_up, w_down, b_down):
    n_rows = xs.shape[0]
    ne, d, de2 = w_gate_up.shape
    tb = EXPERT_ROWS

    def row_block(i, be, nu):
        return (jnp.minimum(i, nu[0] - 1), 0)

    def expert_block(i, be, nu):
        return (be[jnp.minimum(i, nu[0] - 1)], 0, 0)

    grid_spec = pltpu.PrefetchScalarGridSpec(
        num_scalar_prefetch=2,
        grid=(n_rows // tb,),
        in_specs=[pl.BlockSpec((tb,) + xs.shape[1:], row_block),
                  pl.BlockSpec((1, d, de2), expert_block),
                  pl.BlockSpec((1, 1, de2), expert_block),
                  pl.BlockSpec((1, de2 // 2, d), expert_block),
                  pl.BlockSpec((1, 1, d), expert_block)],
        out_specs=pl.BlockSpec((tb, d), lambda i, be, nu: (i, 0)),
        scratch_shapes=[pltpu.VMEM((d, de2), BF16), pltpu.VMEM((de2 // 2, d), BF16)],
    )
    return pl.pallas_call(
        _expert_kernel,
        grid_spec=grid_spec,
        out_shape=jax.ShapeDtypeStruct((n_rows, d), F32),
        compiler_params=_params("arbitrary"),
    )(block_exp, n_used, xs, w_gate_up, b_gate_up.reshape(ne, 1, de2), w_down, b_down.reshape(ne, 1, d))


def _combine_kernel(dest_ref, dest_next_ref, wgt_ref, x1_ref, ys_ref, out_ref, buf_ref, sems):
    i = pl.program_id(0)
    n = pl.num_programs(0)
    tm = x1_ref.shape[0]
    slot = i % 2

    def gather(d_ref, into):
        def issue(t, c):
            for k in range(TOP_K):
                _row_copy(ys_ref, d_ref[k, t], buf_ref.at[into, k], t, sems.at[into]).start()
            return c

        lax.fori_loop(0, tm, issue, 0, unroll=4)

    @pl.when(i == 0)
    def _():
        gather(dest_ref, 0)

    @pl.when(i + 1 < n)
    def _():
        gather(dest_next_ref, 1 - slot)

    for k in range(TOP_K):
        _rows_wait(buf_ref.at[slot, k], tm, sems.at[slot])
    acc = x1_ref[...]
    for k in range(TOP_K):
        acc = acc + wgt_ref[:, k:k + 1] * buf_ref[slot, k]
    out_ref[...] = acc


def _combine(dest, wgt_t, x1, ys):
    t, d = x1.shape
    tm = ROUTE_TILE
    n = t // tm
    return pl.pallas_call(
        _combine_kernel,
        grid=(n,),
        in_specs=[pl.BlockSpec((TOP_K, tm), lambda i: (0, i), memory_space=pltpu.SMEM),
                  pl.BlockSpec((TOP_K, tm), lambda i: (0, jnp.minimum(i + 1, n - 1)), memory_space=pltpu.SMEM),
                  pl.BlockSpec((tm, TOP_K), lambda i: (i, 0)),
                  pl.BlockSpec((tm, d), lambda i: (i, 0)),
                  pl.BlockSpec(memory_space=pl.ANY)],
        out_specs=pl.BlockSpec((tm, d), lambda i: (i, 0)),
        out_shape=jax.ShapeDtypeStruct((t, d), F32),
        scratch_shapes=[pltpu.VMEM((2, TOP_K, tm) + ys.shape[1:], F32), pltpu.SemaphoreType.DMA((2,))],
        compiler_params=_params("arbitrary"),
    )(dest, dest, wgt_t, x1, ys)


def _moe(x1, hp, idx, rank, wgt, counts, w_gate_up, b_gate_up, w_down, b_down):
    t = x1.shape[0]
    tb = EXPERT_ROWS
    ne = w_gate_up.shape[0]
    n_blocks = -(-(t * TOP_K) // tb) + ne
    padded = ((counts + tb - 1) // tb) * tb
    pad_end = jnp.cumsum(padded)
    pad_start = pad_end - padded
    dest = rank + jnp.sum(jnp.where(idx[None] == jnp.arange(ne, dtype=jnp.int32)[:, None, None],
                                    pad_start.astype(jnp.int32)[:, None, None], 0), axis=0)
    block_start = jnp.arange(n_blocks, dtype=jnp.int32) * tb
    block_exp = jnp.minimum(jnp.sum((pad_end[None, :] <= block_start[:, None]).astype(jnp.int32), axis=1), ne - 1)
    n_used = (pad_end[-1:] // tb).astype(jnp.int32)
    xs = _dispatch(block_exp, n_used, dest, hp, n_blocks * tb)
    ys = _experts(block_exp, n_used, xs, w_gate_up, b_gate_up, w_down, b_down)
    return _combine(dest, wgt.T, x1, ys)


def kernel(x, norm1, w_in, q_norm, k_norm, lam_re, lam_im, log_dt, b_re, b_im, c_re, c_im, d_skip, w_glu, b_glu, w_attn_branch, w_ssm_branch, w_out, norm2, w_router, b_router, w_gate_up, b_gate_up, w_down, b_down):
    bsz, seq, d = x.shape
    t = bsz * seq
    x2 = x.reshape(t, d)
    for l in range(norm1.shape[0]):
        qkv0, qkv1, qkv2, u, gates = _in_proj(x2, norm1[l], w_in[l], q_norm[l], k_norm[l], seq)
        attn = [_attention(qkv0.reshape(bsz, seq, -1)), _attention(qkv1), _attention(qkv2)]
        z = _ssm(u, bsz, lam_re[l], lam_im[l], log_dt[l], b_re[l], b_im[l], c_re[l], c_im[l], d_skip[l])
        x1, hp, idx, rank, wgt, counts = _post_mix(
            x2, seq, attn, z, gates, w_attn_branch[l], w_glu[l], b_glu[l], w_ssm_branch[l], w_out[l], norm2[l],
            w_router[l], b_router[l])
        x2 = _moe(x1, hp, idx, rank, wgt, counts[:, 0], w_gate_up[l], b_gate_up[l], w_down[l], b_down[l])
    return x2.reshape(bsz, seq, d)
```

```python
import functools
import math

import numpy as np
import jax
import jax.numpy as jnp
from jax import lax
from jax.experimental import pallas as pl
from jax.experimental.pallas import tpu as pltpu

F32 = jnp.float32
BF16 = jnp.bfloat16

HEAD_DIM = 64
HEADS_PER_GROUP = 4
ATTN_GROUPS = ((128, 1), (512, 4), (2048, 16))
GROUP_WIDTH = HEADS_PER_GROUP * HEAD_DIM
ATTN_WIDTH = GROUP_WIDTH * len(ATTN_GROUPS)
BAND = 64
ROPE_THETA = 10000.0
SSM_GROUP = 16
SSM_STATE = 64
N_EXPERTS = 32
TOP_K = 4
SWIGLU_ALPHA = 1.702
SWIGLU_LIMIT = 7.0
NORM_EPS = 1e-6
MASK_VALUE = -1e30

ROW_TILE = 512
ATTN_Q_TILE = 1024
ATTN_Q_SUB = 128
SSM_CHUNK = 64
SSM_POWER_ROWS = -(-(SSM_CHUNK + 1) // 8) * 8
DISPATCH_TILE = 1024
ROUTE_TILE = 512
EXPERT_ROWS = 512
EXPERT_CHUNK = 512
VMEM_LIMIT = 56 * 1024 * 1024
LANES = 128


def _params(*sem):
    return pltpu.CompilerParams(dimension_semantics=sem, vmem_limit_bytes=VMEM_LIMIT)


def _const_spec(shape):
    return pl.BlockSpec(shape, lambda *_: (0,) * len(shape), pipeline_mode=pl.Buffered(1))


def _permute_qk(a):
    lead = a.shape[:-1]
    n = len(lead)
    a = a.reshape(*lead, len(ATTN_GROUPS), HEADS_PER_GROUP, 2, HEAD_DIM // 2)
    return a.transpose(*range(n), n, n + 2, n + 1, n + 3).reshape(*lead, ATTN_WIDTH)


def _permute_group_lanes(piece):
    hd, hh = HEAD_DIM, HEAD_DIM // 2
    firsts = [piece[:, h * hd:h * hd + hh] for h in range(HEADS_PER_GROUP)]
    seconds = [piece[:, h * hd + hh:(h + 1) * hd] for h in range(HEADS_PER_GROUP)]
    return jnp.concatenate(firsts + seconds, axis=-1)


def _in_proj_kernel(x_ref, g1_ref, wf_ref, gq_ref, gk_ref, cos_ref, sin_ref, seg_ref,
                    qkv0_ref, qkv1_ref, qkv2_ref, u_ref, gate_ref, w_ref, scr_ref):
    @pl.when(pl.program_id(0) == 0)
    def _():
        rows_per_step = 128

        def cast_rows(i, _):
            rows = pl.ds(pl.multiple_of(i * rows_per_step, rows_per_step), rows_per_step)
            for c0 in range(0, 2 * ATTN_WIDTH, GROUP_WIDTH):
                w_ref[rows, c0:c0 + GROUP_WIDTH] = _permute_group_lanes(wf_ref[rows, c0:c0 + GROUP_WIDTH]).astype(BF16)
            for c0 in range(2 * ATTN_WIDTH, wf_ref.shape[1], GROUP_WIDTH):
                w_ref[rows, c0:c0 + GROUP_WIDTH] = wf_ref[rows, c0:c0 + GROUP_WIDTH].astype(BF16)
            return 0

        lax.fori_loop(0, wf_ref.shape[0] // rows_per_step, cast_rows, 0)

    x = x_ref[...]
    ms = jnp.mean(x * x, axis=-1, keepdims=True)
    h = (x * lax.rsqrt(ms + NORM_EPS) * g1_ref[...]).astype(BF16)
    cos = cos_ref[...]
    sin = sin_ref[...]
    seg = seg_ref[...]
    half = GROUP_WIDTH // 2
    tm = x.shape[0]

    def proj(c0, width):
        return jnp.dot(h, w_ref[:, c0:c0 + width], preferred_element_type=F32)

    def sum_squares(t):
        return t[:, :half] * t[:, :half] + t[:, half:] * t[:, half:]

    def norm_rope(t, tot, gain_ref, g):
        t1, t2 = t[:, :half], t[:, half:]
        r = lax.rsqrt(tot * (1.0 / HEAD_DIM) + NORM_EPS)
        a = t1 * r * gain_ref[:, g * GROUP_WIDTH:g * GROUP_WIDTH + half]
        b = t2 * r * gain_ref[:, g * GROUP_WIDTH + half:(g + 1) * GROUP_WIDTH]
        return a * cos - b * sin, b * cos + a * sin

    outs = (qkv0_ref, qkv1_ref, qkv2_ref)
    for g, (_, dil) in enumerate(ATTN_GROUPS):
        tq = proj(g * GROUP_WIDTH, GROUP_WIDTH)
        tk = proj(ATTN_WIDTH + g * GROUP_WIDTH, GROUP_WIDTH)
        ss = jnp.concatenate([sum_squares(tq), sum_squares(tk)], axis=-1).astype(BF16)
        tot = jnp.dot(ss, seg, preferred_element_type=F32)
        q1, q2 = norm_rope(tq, tot[:, :half], gq_ref, g)
        k1, k2 = norm_rope(tk, tot[:, half:], gk_ref, g)
        v = proj(2 * ATTN_WIDTH + g * GROUP_WIDTH, GROUP_WIDTH)
        qkv = jnp.concatenate([q1, q2, k1, k2, v], axis=-1)
        if dil == 1:
            outs[g][...] = qkv.astype(BF16)
        else:
            for c in range(qkv.shape[1] // LANES):
                scr_ref[c] = qkv[:, c * LANES:(c + 1) * LANES]
            for r in range(dil):
                for c in range(qkv.shape[1] // LANES):
                    outs[g][r, :, c * LANES:(c + 1) * LANES] = (
                        scr_ref[c, pl.ds(r, tm // dil, stride=dil), :].astype(BF16))
    o_u = 3 * ATTN_WIDTH
    u_w = u_ref.shape[-1]
    u_ref[...] = proj(o_u, u_w).astype(BF16)
    gate_w = gate_ref.shape[-1]
    for c in range(0, gate_w, 1024):
        gate_ref[:, c:c + 1024] = jax.nn.sigmoid(proj(o_u + u_w + c, 1024)).astype(BF16)


def _in_proj(x2, norm1, w_in, q_norm, k_norm, seq):
    t, d = x2.shape
    tm = ROW_TILE
    in_w = w_in.shape[1]
    ssm_w = d // 2
    gate_w = in_w - 3 * ATTN_WIDTH - ssm_w
    gq = (_permute_qk(q_norm.reshape(-1)) * (HEAD_DIM ** -0.5)).reshape(1, ATTN_WIDTH)
    gk = _permute_qk(k_norm.reshape(-1)).reshape(1, ATTN_WIDTH)
    inv_freq = ROPE_THETA ** (-jnp.arange(0, HEAD_DIM, 2, dtype=F32) / HEAD_DIM)
    ang = jnp.arange(seq, dtype=F32)[:, None] * inv_freq[None, :]
    cos = jnp.tile(jnp.cos(ang), (1, HEADS_PER_GROUP))
    sin = jnp.tile(jnp.sin(ang), (1, HEADS_PER_GROUP))
    lane = np.arange(GROUP_WIDTH)
    seg = jnp.asarray(lane[:, None] // (HEAD_DIM // 2) == lane[None, :] // (HEAD_DIM // 2), BF16)
    bsz = t // seq
    tiles_per_seq = seq // tm
    qkv_w = 3 * GROUP_WIDTH
    out_shape = [jax.ShapeDtypeStruct((t, qkv_w), BF16)]
    out_specs = [pl.BlockSpec((tm, qkv_w), lambda i: (i, 0))]
    for _, dil in ATTN_GROUPS[1:]:
        out_shape.append(jax.ShapeDtypeStruct((bsz * dil, seq // dil, qkv_w), BF16))
        out_specs.append(pl.BlockSpec((dil, tm // dil, qkv_w),
                                      lambda i: (i // tiles_per_seq, i % tiles_per_seq, 0)))
    out_shape += [jax.ShapeDtypeStruct((t, ssm_w), BF16), jax.ShapeDtypeStruct((t, gate_w), BF16)]
    out_specs += [pl.BlockSpec((tm, ssm_w), lambda i: (i, 0)), pl.BlockSpec((tm, gate_w), lambda i: (i, 0))]
    return pl.pallas_call(
        _in_proj_kernel,
        grid=(t // tm,),
        in_specs=[
            pl.BlockSpec((tm, d), lambda i: (i, 0)),
            _const_spec((1, d)),
            _const_spec((d, in_w)),
            _const_spec((1, ATTN_WIDTH)),
            _const_spec((1, ATTN_WIDTH)),
            pl.BlockSpec((tm, GROUP_WIDTH // 2), lambda i: (i % tiles_per_seq, 0)),
            pl.BlockSpec((tm, GROUP_WIDTH // 2), lambda i: (i % tiles_per_seq, 0)),
            _const_spec((GROUP_WIDTH, GROUP_WIDTH)),
        ],
        out_specs=out_specs,
        out_shape=out_shape,
        scratch_shapes=[pltpu.VMEM((d, in_w), BF16), pltpu.VMEM((qkv_w // LANES, tm, LANES), F32)],
        compiler_params=_params("arbitrary"),
    )(x2, norm1.reshape(1, d), w_in, gq, gk, cos, sin, seg)


def _attn_kernel(c_ref, kl_ref, vl_ref, kr_ref, vr_ref, o_ref, lse_ref, kcat_ref, vcat_ref, *, length):
    tq = c_ref.shape[1]
    gw = GROUP_WIDTH
    j = pl.program_id(1)
    kcat_ref[0:BAND] = kl_ref[0]
    kcat_ref[BAND:BAND + tq] = c_ref[0, :, gw:2 * gw]
    kcat_ref[BAND + tq:] = kr_ref[0]
    vcat_ref[0:BAND] = vl_ref[0]
    vcat_ref[BAND:BAND + tq] = c_ref[0, :, 2 * gw:]
    vcat_ref[BAND + tq:] = vr_ref[0]
    sub = min(ATTN_Q_SUB, tq)
    nk = sub + 2 * BAND
    nh = HEADS_PER_GROUP
    row = lax.broadcasted_iota(jnp.int32, (nh * sub, nk), 0) & (sub - 1)
    col = lax.broadcasted_iota(jnp.int32, (nh * sub, nk), 1)
    in_band = jnp.abs(col - BAND - row) <= BAND
    q_lane = lax.broadcasted_iota(jnp.int32, (1, gw), 1)
    q_head = (q_lane % (gw // 2)) // (HEAD_DIM // 2)
    v_head = q_lane // HEAD_DIM
    for s in range(tq // sub):
        q_s = c_ref[0, s * sub:(s + 1) * sub, 0:gw]
        k_s = kcat_ref[s * sub:s * sub + nk]
        v_s = vcat_ref[s * sub:s * sub + nk]
        kpos = j * tq + s * sub - BAND + col
        valid = in_band & (kpos >= 0) & (kpos < length)
        q4 = jnp.concatenate([jnp.where(q_head == hh, q_s, jnp.zeros_like(q_s)) for hh in range(nh)], axis=0)
        sc = lax.dot_general(q4, k_s, (((1,), (1,)), ((), ())), preferred_element_type=F32)
        sc = jnp.where(valid, sc, MASK_VALUE)
        m = jnp.max(sc, axis=-1, keepdims=True)
        p = jnp.exp(sc - m)
        den = jnp.sum(p, axis=-1, keepdims=True)
        o4 = jnp.dot(p.astype(BF16), v_s, preferred_element_type=F32) * (1.0 / den)
        l4 = m + jnp.log(den)
        out = o4[0:sub]
        lse = jnp.broadcast_to(l4[0:sub], (sub, gw))
        for hh in range(1, nh):
            out = jnp.where(v_head == hh, o4[hh * sub:(hh + 1) * sub], out)
            lse = jnp.where(v_head == hh, l4[hh * sub:(hh + 1) * sub], lse)
        o_ref[0, s * sub:(s + 1) * sub] = out.astype(BF16)
        lse_ref[0, s * sub:(s + 1) * sub] = lse


def _attention(qkv):
    n, length, _ = qkv.shape
    tq = min(ATTN_Q_TILE, length)
    gw = GROUP_WIDTH
    hb = tq // BAND
    last = length // BAND - 1

    def left(lane_block):
        return pl.BlockSpec((1, BAND, gw), lambda b, j: (b, jnp.maximum(j * hb - 1, 0), lane_block))

    def right(lane_block):
        return pl.BlockSpec((1, BAND, gw), lambda b, j: (b, jnp.minimum((j + 1) * hb, last), lane_block))

    return pl.pallas_call(
        functools.partial(_attn_kernel, length=length),
        grid=(n, length // tq),
        in_specs=[pl.BlockSpec((1, tq, 3 * gw), lambda b, j: (b, j, 0)), left(1), left(2), right(1), right(2)],
        out_specs=[pl.BlockSpec((1, tq, gw), lambda b, j: (b, j, 0)),
                   pl.BlockSpec((1, tq, gw), lambda b, j: (b, j, 0))],
        out_shape=[jax.ShapeDtypeStruct((n, length, gw), BF16), jax.ShapeDtypeStruct((n, length, gw), F32)],
        scratch_shapes=[pltpu.VMEM((tq + 2 * BAND, gw), BF16), pltpu.VMEM((tq + 2 * BAND, gw), BF16)],
        compiler_params=_params("parallel", "parallel"),
    )(qkv, qkv, qkv, qkv, qkv)


def _ssm_operators(lam_re, lam_im, log_dt, b_re, b_im, c_re, c_im, d_skip):
    L = SSM_CHUNK
    G, P = lam_re.shape[1:]
    C = d_skip.shape[-1]
    hp = lax.Precision.HIGHEST
    n = jnp.arange(L + 1, dtype=F32)[:, None, None]
    pad = SSM_POWER_ROWS - (L + 1)
    kern, pw, fbt, ct, consts = [], [], [], [], []
    for direction in range(2):
        lr, li = lam_re[direction].astype(F32), lam_im[direction].astype(F32)
        dt = jnp.exp(log_dt[direction].astype(F32))[:, None]
        mag = jnp.exp(lr * dt)
        ab_re, ab_im = mag * jnp.cos(li * dt), mag * jnp.sin(li * dt)
        den = lr * lr + li * li
        nr = ab_re - 1.0
        f_re = (nr * lr + ab_im * li) / den
        f_im = (ab_im * lr - nr * li) / den
        pmag = jnp.exp(n * (lr * dt)[None])
        p_re, p_im = pmag * jnp.cos(n * (li * dt)[None]), pmag * jnp.sin(n * (li * dt)[None])
        br, bi = b_re[direction].astype(F32), b_im[direction].astype(F32)
        fb_re = f_re[..., None] * br - f_im[..., None] * bi
        fb_im = f_re[..., None] * bi + f_im[..., None] * br
        cr, ci = c_re[direction].astype(F32), c_im[direction].astype(F32)
        pr, pi = p_re[:L].transpose(1, 2, 0)[..., None], p_im[:L].transpose(1, 2, 0)[..., None]
        crt, cit = cr.transpose(0, 2, 1)[:, :, None, :], ci.transpose(0, 2, 1)[:, :, None, :]
        e_re = (pr * crt - pi * cit).reshape(G, P, L * C)
        e_im = (pr * cit + pi * crt).reshape(G, P, L * C)
        kern.append((jnp.einsum('gpc,gpx->gcx', fb_re, e_re, precision=hp)
                     - jnp.einsum('gpc,gpx->gcx', fb_im, e_im, precision=hp)).reshape(G, C, L, C))
        for p in (p_re, p_im):
            p = jnp.pad(p.transpose(1, 0, 2), ((0, 0), (0, pad), (0, 0)))
            pw.append(jnp.concatenate([p, p, p, p], axis=-1))
        tr, ti = fb_re.transpose(0, 2, 1), fb_im.transpose(0, 2, 1)
        fbt.append(jnp.concatenate([tr, ti, ti, tr], axis=-1))
        fbt.append(jnp.concatenate([-ti, tr, tr, -ti], axis=-1))
        ct.append(jnp.concatenate([cr, -ci], axis=-1))
        ct.append(jnp.concatenate([-ci, -cr], axis=-1))
        a_re, a_im = p_re[L], p_im[L]
        consts.append(jnp.concatenate([a_re, a_re, a_re, a_re], axis=-1))
        consts.append(jnp.concatenate([-a_im, a_im, a_im, -a_im], axis=-1))
    kf, kb = kern
    zero = kf[:, :, 0] + kb[:, :, 0] + jnp.eye(C, dtype=F32)[None] * d_skip.astype(F32)[:, None, :]
    kfull = jnp.concatenate([kb[:, :, :0:-1], zero[:, :, None], kf[:, :, 1:], jnp.zeros_like(zero)[:, :, None]],
                            axis=2)
    flat = kfull.astype(BF16).reshape(G, C, 2 * L * C)
    return (flat, jnp.stack(pw, axis=1), jnp.stack(fbt, axis=1), jnp.stack(ct, axis=1),
            jnp.stack(consts + consts, axis=1))


def _ssm_kernel(u_ref, flat_ref, pw_ref, fbt_ref, ct_ref, cst_ref, z_ref,
                toe_ref, wst_ref, vt_ref, f_ref, s_ref, *, bsz):
    u = u_ref[0]
    nc = u.shape[0]
    sw = cst_ref.shape[-1]
    hw = sw // 2
    C = fbt_ref.shape[2]
    L = toe_ref.shape[0] // C
    lc = L * C

    per_tile = LANES // C
    span = (2 * L - per_tile) * C
    for b in range(per_tile):
        shifted = flat_ref[0, :, (per_tile - 1 - b) * C:(per_tile - 1 - b) * C + span]
        for a in range(L // per_tile):
            t_in = a * per_tile + b
            off = (L // per_tile - 1 - a) * LANES
            toe_ref[t_in * C:(t_in + 1) * C, :] = shifted[:, off:off + lc]

    fa_f, fb_f, fa_b, fb_b = fbt_ref[0, 0], fbt_ref[0, 1], fbt_ref[0, 2], fbt_ref[0, 3]
    ca_f, cb_f, ca_b, cb_b = ct_ref[0, 0], ct_ref[0, 1], ct_ref[0, 2], ct_ref[0, 3]

    def power(k, n):
        return pw_ref[0, k, n:n + 1, :]

    for t in range(L):
        rows = slice(t * C, (t + 1) * C)
        wst_ref[rows, 0:sw] = (fa_f * power(0, L - 1 - t) + fb_f * power(1, L - 1 - t)).astype(BF16)
        wst_ref[rows, sw:2 * sw] = (fa_b * power(2, t) + fb_b * power(3, t)).astype(BF16)
        vt_ref[rows, 0:hw] = (ca_f * power(0, t + 1)[:, :hw] + cb_f * power(1, t + 1)[:, :hw]).astype(BF16)
        vt_ref[rows, hw:sw] = (ca_b * power(2, L - t)[:, :hw] + cb_b * power(3, L - t)[:, :hw]).astype(BF16)

    f_ref[...] = jnp.dot(u, wst_ref[...], preferred_element_type=F32)
    cst = cst_ref[0]
    a1f, a2f, a1b, a2b = cst[0:1], cst[1:2], cst[2:3], cst[3:4]
    per = nc // bsz

    def swap(v):
        return jnp.concatenate([v[:, hw:], v[:, :hw]], axis=-1)

    rows = 8

    def step(i, carry):
        new = []
        for b in range(bsz):
            sf, sb = carry[2 * b], carry[2 * b + 1]
            base_f = pl.multiple_of(b * per + i * rows, rows)
            base_b = pl.multiple_of(b * per + per - rows - i * rows, rows)
            f_blk = f_ref[pl.ds(base_f, rows), 0:sw]
            b_blk = f_ref[pl.ds(base_b, rows), sw:2 * sw]
            f_rows, b_rows = [], []
            for r in range(rows):
                f_rows.append(sf[:, :hw])
                sf = a1f * sf + a2f * swap(sf) + f_blk[r:r + 1]
                b_rows.append(sb[:, :hw])
                sb = a1b * sb + a2b * swap(sb) + b_blk[rows - 1 - r:rows - r]
            s_ref[pl.ds(base_f, rows), 0:hw] = jnp.concatenate(f_rows, axis=0)
            s_ref[pl.ds(base_b, rows), hw:sw] = jnp.concatenate(b_rows[::-1], axis=0)
            new += [sf, sb]
        return tuple(new)

    lax.fori_loop(0, per // rows, step, tuple(jnp.zeros((1, sw), F32) for _ in range(2 * bsz)))
    y = jnp.dot(u, toe_ref[...], preferred_element_type=F32)
    y = y + lax.dot_general(s_ref[...].astype(BF16), vt_ref[...], (((1,), (1,)), ((), ())),
                            preferred_element_type=F32)
    z_ref[0] = jax.nn.gelu(y).astype(BF16)


def _ssm(u, bsz, lam_re, lam_im, log_dt, b_re, b_im, c_re, c_im, d_skip):
    t, width = u.shape
    L, C = SSM_CHUNK, d_skip.shape[-1]
    G = width // C
    nc = t // L
    flat, pw, fbt, ct, consts = _ssm_operators(lam_re, lam_im, log_dt, b_re, b_im, c_re, c_im, d_skip)
    sw = consts.shape[-1]
    ug = u.reshape(nc, L, G, C).transpose(2, 0, 1, 3).reshape(G, nc, L * C)

    def group_spec(a):
        return pl.BlockSpec((1,) + a.shape[1:], lambda g: (g,) + (0,) * (a.ndim - 1))

    z = pl.pallas_call(
        functools.partial(_ssm_kernel, bsz=bsz),
        grid=(G,),
        in_specs=[group_spec(a) for a in (ug, flat, pw, fbt, ct, consts)],
        out_specs=pl.BlockSpec((1, nc, L * C), lambda g: (g, 0, 0)),
        out_shape=jax.ShapeDtypeStruct((G, nc, L * C), BF16),
        scratch_shapes=[pltpu.VMEM((L * C, L * C), BF16), pltpu.VMEM((L * C, 2 * sw), BF16),
                        pltpu.VMEM((L * C, sw), BF16), pltpu.VMEM((nc, 2 * sw), F32), pltpu.VMEM((nc, sw), F32)],
        compiler_params=_params("parallel"),
    )(ug, flat, pw, fbt, ct, consts)
    return z.reshape(G, nc, L, C).transpose(1, 2, 0, 3).reshape(t, width)


def _post_kernel(o0_ref, l0_ref, o1_ref, l1_ref, o2_ref, l2_ref, z_ref, gate_ref, x_ref,
                 wa_ref, wglu_ref, bglu_ref, ws_ref, wo_ref, g2_ref, wr_ref, wrlo_ref, br_ref, tri_ref,
                 x1_ref, hp_ref, idx_ref, rank_ref, wgt_ref, cnt_ref, o_scr, l_scr, carry_ref):
    tm, d = x_ref.shape

    @pl.when(pl.program_id(0) == 0)
    def _():
        carry_ref[...] = jnp.zeros_like(carry_ref)

    outs = [o0_ref[...].astype(F32)]
    lses = [l0_ref[...]]
    for (o_ref, l_ref), (_, dil) in zip(((o1_ref, l1_ref), (o2_ref, l2_ref)), ATTN_GROUPS[1:]):
        nlb = o_scr.shape[0]
        for r in range(dil):
            for c in range(nlb):
                o_scr[c, pl.ds(r, tm // dil, stride=dil), :] = o_ref[r, :, c * LANES:(c + 1) * LANES].astype(F32)
                l_scr[c, pl.ds(r, tm // dil, stride=dil), :] = l_ref[r, :, c * LANES:(c + 1) * LANES]
        outs.append(jnp.concatenate([o_scr[c] for c in range(nlb)], axis=-1))
        lses.append(jnp.concatenate([l_scr[c] for c in range(nlb)], axis=-1))
    top = jnp.maximum(jnp.maximum(lses[0], lses[1]), lses[2])
    es = [jnp.exp(l - top) for l in lses]
    y_attn = (es[0] * outs[0] + es[1] * outs[1] + es[2] * outs[2]) / (es[0] + es[1] + es[2])
    ya = jnp.dot(y_attn.astype(BF16), wa_ref[...], preferred_element_type=F32)
    z = z_ref[...]
    glu = jnp.dot(z, wglu_ref[...], preferred_element_type=F32) + bglu_ref[...]
    zz = z.astype(F32) * jax.nn.sigmoid(glu)
    ys = jnp.dot(zz.astype(BF16), ws_ref[...], preferred_element_type=F32)
    mixed = gate_ref[:, :d].astype(F32) * ya + gate_ref[:, d:].astype(F32) * ys
    x1 = x_ref[...] + jnp.dot(mixed.astype(BF16), wo_ref[...], preferred_element_type=F32)
    x1_ref[...] = x1
    ms = jnp.mean(x1 * x1, axis=-1, keepdims=True)
    h2 = x1 * lax.rsqrt(ms + NORM_EPS) * g2_ref[...]
    hb = h2.astype(BF16)
    hp_ref[...] = h2

    h_lo = (h2 - hb.astype(F32)).astype(BF16)
    nt = (((1,), (1,)), ((), ()))
    logits = (lax.dot_general(wr_ref[...], hb, nt, preferred_element_type=F32)
              + lax.dot_general(wrlo_ref[...], hb, nt, preferred_element_type=F32)
              + lax.dot_general(wr_ref[...], h_lo, nt, preferred_element_type=F32) + br_ref[...])
    ne = logits.shape[0]
    e_iota = lax.broadcasted_iota(jnp.int32, logits.shape, 0)
    vals, idxs = [], []
    work = logits
    for _ in range(TOP_K):
        m = jnp.max(work, axis=0, keepdims=True)
        i = jnp.min(jnp.where(work == m, e_iota, ne), axis=0, keepdims=True)
        vals.append(m)
        idxs.append(i)
        work = jnp.where(e_iota == i, -jnp.inf, work)
    ex = [jnp.exp(v - vals[0]) for v in vals]
    tot = ex[0] + ex[1] + ex[2] + ex[3]
    onehot = jnp.zeros(logits.shape, F32)
    for i in idxs:
        onehot = onehot + (e_iota == i).astype(F32)
    before = jnp.dot(onehot.astype(BF16), tri_ref[...], preferred_element_type=F32) + carry_ref[:, 0:1]
    for k in range(TOP_K):
        idx_ref[k:k + 1, :] = idxs[k]
        rank_ref[k:k + 1, :] = jnp.sum(jnp.where(e_iota == idxs[k], before, 0.0), axis=0,
                                       keepdims=True).astype(jnp.int32)
        wgt_ref[k:k + 1, :] = ex[k] / tot
    carry_ref[...] = carry_ref[...] + jnp.sum(onehot, axis=1, keepdims=True)
    cnt_ref[...] = carry_ref[...].astype(jnp.int32)


def _post_mix(x2, seq, attn, z, gates, w_attn_branch, w_glu, b_glu, w_ssm_branch, w_out, norm2, w_router, b_router):
    t, d = x2.shape
    tm = ROW_TILE
    gw = GROUP_WIDTH
    tiles_per_seq = seq // tm
    ne = w_router.shape[1]
    ssm_w = z.shape[1]
    in_specs, args = [], []
    for (o, lse), (_, dil) in zip(attn, ATTN_GROUPS):
        if dil == 1:
            spec = pl.BlockSpec((tm, gw), lambda i: (i, 0))
            o, lse = o.reshape(t, gw), lse.reshape(t, gw)
        else:
            spec = pl.BlockSpec((dil, tm // dil, gw), lambda i: (i // tiles_per_seq, i % tiles_per_seq, 0))
        in_specs += [spec, spec]
        args += [o, lse]
    tri = jnp.asarray(np.arange(tm)[:, None] < np.arange(tm)[None, :], BF16)
    in_specs += [pl.BlockSpec((tm, ssm_w), lambda i: (i, 0)),
                 pl.BlockSpec((tm, 2 * d), lambda i: (i, 0)),
                 pl.BlockSpec((tm, d), lambda i: (i, 0)),
                 _const_spec((gw, d)), _const_spec((ssm_w, ssm_w)), _const_spec((1, ssm_w)),
                 _const_spec((ssm_w, d)), _const_spec((d, d)), _const_spec((1, d)),
                 _const_spec((ne, d)), _const_spec((ne, d)), _const_spec((ne, 1)), _const_spec((tm, tm))]
    wr_t = w_router.T.astype(F32)
    wr_hi = wr_t.astype(BF16)
    wr_lo = (wr_t - wr_hi.astype(F32)).astype(BF16)
    args += [z, gates, x2, w_attn_branch.astype(BF16), w_glu.astype(BF16), b_glu.reshape(1, ssm_w),
             w_ssm_branch.astype(BF16), w_out.astype(BF16), norm2.reshape(1, d),
             wr_hi, wr_lo, b_router.reshape(ne, 1), tri]
    tok_spec = pl.BlockSpec((TOP_K, tm), lambda i: (0, i))
    return pl.pallas_call(
        _post_kernel,
        grid=(t // tm,),
        in_specs=in_specs,
        out_specs=[pl.BlockSpec((tm, d), lambda i: (i, 0)),
                   pl.BlockSpec((tm, d), lambda i: (i, 0)),
                   tok_spec, tok_spec, tok_spec, pl.BlockSpec((ne, LANES), lambda i: (0, 0))],
        out_shape=[jax.ShapeDtypeStruct((t, d), F32), jax.ShapeDtypeStruct((t, d), F32),
                   jax.ShapeDtypeStruct((TOP_K, t), jnp.int32), jax.ShapeDtypeStruct((TOP_K, t), jnp.int32),
                   jax.ShapeDtypeStruct((TOP_K, t), F32), jax.ShapeDtypeStruct((ne, LANES), jnp.int32)],
        scratch_shapes=[pltpu.VMEM((gw // LANES, tm, LANES), F32), pltpu.VMEM((gw // LANES, tm, LANES), F32),
                        pltpu.VMEM((ne, LANES), F32)],
        compiler_params=_params("arbitrary"),
    )(*args)


def _row_copy(src_ref, src_row, dst_ref, dst_row, sem):
    return pltpu.make_async_copy(src_ref.at[pl.ds(src_row, 1)], dst_ref.at[pl.ds(dst_row, 1)], sem)


def _rows_wait(ref, n_rows, sem):
    pltpu.make_async_copy(ref.at[pl.ds(0, n_rows)], ref.at[pl.ds(0, n_rows)], sem).wait()


def _dispatch_kernel(be_ref, nu_ref, dest_ref, h_ref, xs_ref, zero_ref, sems):
    i = pl.program_id(0)
    tm = dest_ref.shape[1]
    tb = zero_ref.shape[0]
    n_blocks = be_ref.shape[0]

    @pl.when(i == 0)
    def _():
        zero_ref[...] = jnp.zeros_like(zero_ref)

        def holds_padding(b):
            return (b >= nu_ref[0] - 1) | (be_ref[b] != be_ref[jnp.minimum(b + 1, n_blocks - 1)])

        def zero_copy(b):
            return pltpu.make_async_copy(zero_ref, xs_ref.at[pl.ds(pl.multiple_of(b * tb, tb), tb)], sems.at[1])

        def start(b, c):
            @pl.when(holds_padding(b))
            def _():
                zero_copy(b).start()
            return c

        def wait(b, c):
            @pl.when(holds_padding(b))
            def _():
                zero_copy(b).wait()
            return c

        lax.fori_loop(0, n_blocks, start, 0)
        lax.fori_loop(0, n_blocks, wait, 0)

    def issue(t, c):
        for k in range(TOP_K):
            _row_copy(h_ref, t, xs_ref, dest_ref[k, t], sems.at[0]).start()
        return c

    lax.fori_loop(0, tm, issue, 0, unroll=4)
    _rows_wait(xs_ref, TOP_K * tm, sems.at[0])


def _dispatch(block_exp, n_used, dest, hp, n_rows):
    t = hp.shape[0]
    slab = hp.shape[1:]
    tm = DISPATCH_TILE
    grid_spec = pltpu.PrefetchScalarGridSpec(
        num_scalar_prefetch=2,
        grid=(t // tm,),
        in_specs=[pl.BlockSpec((TOP_K, tm), lambda i, be, nu: (0, i), memory_space=pltpu.SMEM),
                  pl.BlockSpec((tm,) + slab, lambda i, be, nu: (i,) + (0,) * len(slab))],
        out_specs=pl.BlockSpec(memory_space=pl.ANY),
        scratch_shapes=[pltpu.VMEM((EXPERT_ROWS,) + slab, hp.dtype), pltpu.SemaphoreType.DMA((2,))],
    )
    return pl.pallas_call(
        _dispatch_kernel,
        grid_spec=grid_spec,
        out_shape=jax.ShapeDtypeStruct((n_rows,) + slab, hp.dtype),
        compiler_params=_params("arbitrary"),
    )(block_exp, n_used, dest, hp)


def _expert_kernel(be_ref, nu_ref, xs_ref, wgu_ref, bgu_ref, wd_ref, bd_ref, ys_ref, wgu_bf, wd_bf):
    i = pl.program_id(0)
    active = i < nu_ref[0]

    @pl.when(active & ((i == 0) | (be_ref[i] != be_ref[jnp.maximum(i - 1, 0)])))
    def _():
        wgu_bf[...] = wgu_ref[0].astype(BF16)
        wd_bf[...] = wd_ref[0].astype(BF16)

    @pl.when(active)
    def _():
        x = xs_ref[...].astype(BF16)
        de = wd_bf.shape[0]
        y = None
        for c0 in range(0, de, EXPERT_CHUNK):
            cols = slice(c0, c0 + EXPERT_CHUNK)
            ucols = slice(de + c0, de + c0 + EXPERT_CHUNK)
            gate = jnp.dot(x, wgu_bf[:, cols], preferred_element_type=F32) + bgu_ref[0, :, cols]
            up = jnp.dot(x, wgu_bf[:, ucols], preferred_element_type=F32) + bgu_ref[0, :, ucols]
            gate = jnp.minimum(gate, SWIGLU_LIMIT)
            up = jnp.clip(up, -SWIGLU_LIMIT, SWIGLU_LIMIT)
            act = gate * jax.nn.sigmoid(SWIGLU_ALPHA * gate) * (up + 1.0)
            part = jnp.dot(act.astype(BF16), wd_bf[cols, :], preferred_element_type=F32)
            y = part + bd_ref[0] if y is None else y + part
        ys_ref[...] = y

    @pl.when(i >= nu_ref[0])
    def _():
        ys_ref[...] = jnp.zeros_like(ys_ref)


def _experts(block_exp, n_used, xs, w_gate_up, b_gate_up, w_down, b_down):
    n_rows = xs.shape[0]
    ne, d, de2 = w_gate_up.shape
    tb = EXPERT_ROWS

    def row_block(i, be, nu):
        return (jnp.minimum(i, nu[0] - 1), 0)

    def expert_block(i, be, nu):
        return (be[jnp.minimum(i, nu[0] - 1)], 0, 0)

    grid_spec = pltpu.PrefetchScalarGridSpec(
        num_scalar_prefetch=2,
        grid=(n_rows // tb,),
        in_specs=[pl.BlockSpec((tb,) + xs.shape[1:], row_block),
                  pl.BlockSpec((1, d, de2), expert_block),
                  pl.BlockSpec((1, 1, de2), expert_block),
                  pl.BlockSpec((1, de2 // 2, d), expert_block),
                  pl.BlockSpec((1, 1, d), expert_block)],
        out_specs=pl.BlockSpec((tb, d), lambda i, be, nu: (i, 0)),
        scratch_shapes=[pltpu.VMEM((d, de2), BF16), pltpu.VMEM((de2 // 2, d), BF16)],
    )
    return pl.pallas_call(
        _expert_kernel,
        grid_spec=grid_spec,
        out_shape=jax.ShapeDtypeStruct((n_rows, d), F32),
        compiler_params=_params("arbitrary"),
    )(block_exp, n_used, xs, w_gate_up, b_gate_up.reshape(ne, 1, de2), w_down, b_down.reshape(ne, 1, d))


def _combine_kernel(dest_ref, dest_next_ref, wgt_ref, x1_ref, ys_ref, out_ref, buf_ref, sems):
    i = pl.program_id(0)
    n = pl.num_programs(0)
    tm = x1_ref.shape[0]
    slot = i % 2

    def gather(d_ref, into):
        def issue(t, c):
            for k in range(TOP_K):
                _row_copy(ys_ref, d_ref[k, t], buf_ref.at[into, k], t, sems.at[into]).start()
            return c

        lax.fori_loop(0, tm, issue, 0, unroll=4)

    @pl.when(i == 0)
    def _():
        gather(dest_ref, 0)

    @pl.when(i + 1 < n)
    def _():
        gather(dest_next_ref, 1 - slot)

    for k in range(TOP_K):
        _rows_wait(buf_ref.at[slot, k], tm, sems.at[slot])
    acc = x1_ref[...]
    for k in range(TOP_K):
        acc = acc + wgt_ref[:, k:k + 1] * buf_ref[slot, k]
    out_ref[...] = acc


def _combine(dest, wgt_t, x1, ys):
    t, d = x1.shape
    tm = ROUTE_TILE
    n = t // tm
    return pl.pallas_call(
        _combine_kernel,
        grid=(n,),
        in_specs=[pl.BlockSpec((TOP_K, tm), lambda i: (0, i), memory_space=pltpu.SMEM),
                  pl.BlockSpec((TOP_K, tm), lambda i: (0, jnp.minimum(i + 1, n - 1)), memory_space=pltpu.SMEM),
                  pl.BlockSpec((tm, TOP_K), lambda i: (i, 0)),
                  pl.BlockSpec((tm, d), lambda i: (i, 0)),
                  pl.BlockSpec(memory_space=pl.ANY)],
        out_specs=pl.BlockSpec((tm, d), lambda i: (i, 0)),
        out_shape=jax.ShapeDtypeStruct((t, d), F32),
        scratch_shapes=[pltpu.VMEM((2, TOP_K, tm) + ys.shape[1:], F32), pltpu.SemaphoreType.DMA((2,))],
        compiler_params=_params("arbitrary"),
    )(dest, dest, wgt_t, x1, ys)


def _moe(x1, hp, idx, rank, wgt, counts, w_gate_up, b_gate_up, w_down, b_down):
    t = x1.shape[0]
    tb = EXPERT_ROWS
    ne = w_gate_up.shape[0]
    n_blocks = -(-(t * TOP_K) // tb) + ne
    padded = ((counts + tb - 1) // tb) * tb
    pad_end = jnp.cumsum(padded)
    pad_start = pad_end - padded
    dest = rank + jnp.sum(jnp.where(idx[None] == jnp.arange(ne, dtype=jnp.int32)[:, None, None],
                                    pad_start.astype(jnp.int32)[:, None, None], 0), axis=0)
    block_start = jnp.arange(n_blocks, dtype=jnp.int32) * tb
    block_exp = jnp.minimum(jnp.sum((pad_end[None, :] <= block_start[:, None]).astype(jnp.int32), axis=1), ne - 1)
    n_used = (pad_end[-1:] // tb).astype(jnp.int32)
    xs = _dispatch(block_exp, n_used, dest, hp, n_blocks * tb)
    ys = _experts(block_exp, n_used, xs, w_gate_up, b_gate_up, w_down, b_down)
    return _combine(dest, wgt.T, x1, ys)


def kernel(x, norm1, w_in, q_norm, k_norm, lam_re, lam_im, log_dt, b_re, b_im, c_re, c_im, d_skip, w_glu, b_glu, w_attn_branch, w_ssm_branch, w_out, norm2, w_router, b_router, w_gate_up, b_gate_up, w_down, b_down):
    bsz, seq, d = x.shape
    t = bsz * seq
    x2 = x.reshape(t, d)
    for l in range(norm1.shape[0]):
        qkv0, qkv1, qkv2, u, gates = _in_proj(x2, norm1[l], w_in[l], q_norm[l], k_norm[l], seq)
        attn = [_attention(qkv0.reshape(bsz, seq, -1)), _attention(qkv1), _attention(qkv2)]
        z = _ssm(u, bsz, lam_re[l], lam_im[l], log_dt[l], b_re[l], b_im[l], c_re[l], c_im[l], d_skip[l])
        x1, hp, idx, rank, wgt, counts = _post_mix(
            x2, seq, attn, z, gates, w_attn_branch[l], w_glu[l], b_glu[l], w_ssm_branch[l], w_out[l], norm2[l],
            w_router[l], b_router[l])
        x2 = _moe(x1, hp, idx, rank, wgt, counts[:, 0], w_gate_up[l], b_gate_up[l], w_down[l], b_down[l])
    return x2.reshape(bsz, seq, d)
```

```python
import functools
import math

import numpy as np
import jax
import jax.numpy as jnp
from jax import lax
from jax.experimental import pallas as pl
from jax.experimental.pallas import tpu as pltpu

F32 = jnp.float32
BF16 = jnp.bfloat16

HEAD_DIM = 64
HEADS_PER_GROUP = 4
ATTN_GROUPS = ((128, 1), (512, 4), (2048, 16))
GROUP_WIDTH = HEADS_PER_GROUP * HEAD_DIM
ATTN_WIDTH = GROUP_WIDTH * len(ATTN_GROUPS)
BAND = 64
ROPE_THETA = 10000.0
SSM_GROUP = 16
SSM_STATE = 64
N_EXPERTS = 32
TOP_K = 4
SWIGLU_ALPHA = 1.702
SWIGLU_LIMIT = 7.0
NORM_EPS = 1e-6
MASK_VALUE = -1e30

ROW_TILE = 512
ATTN_Q_TILE = 1024
ATTN_Q_SUB = 128
SSM_CHUNK = 64
SSM_POWER_ROWS = -(-(SSM_CHUNK + 1) // 8) * 8
DISPATCH_TILE = 1024
ROUTE_TILE = 512
EXPERT_ROWS = 512
VMEM_LIMIT = 56 * 1024 * 1024
LANES = 128


def _params(*sem):
    return pltpu.CompilerParams(dimension_semantics=sem, vmem_limit_bytes=VMEM_LIMIT)


def _const_spec(shape):
    return pl.BlockSpec(shape, lambda *_: (0,) * len(shape), pipeline_mode=pl.Buffered(1))


def _permute_qk(a):
    lead = a.shape[:-1]
    n = len(lead)
    a = a.reshape(*lead, len(ATTN_GROUPS), HEADS_PER_GROUP, 2, HEAD_DIM // 2)
    return a.transpose(*range(n), n, n + 2, n + 1, n + 3).reshape(*lead, ATTN_WIDTH)


def _permute_group_lanes(piece):
    hd, hh = HEAD_DIM, HEAD_DIM // 2
    firsts = [piece[:, h * hd:h * hd + hh] for h in range(HEADS_PER_GROUP)]
    seconds = [piece[:, h * hd + hh:(h + 1) * hd] for h in range(HEADS_PER_GROUP)]
    return jnp.concatenate(firsts + seconds, axis=-1)


def _in_proj_kernel(x_ref, g1_ref, wf_ref, gq_ref, gk_ref, cos_ref, sin_ref, seg_ref,
                    qkv0_ref, qkv1_ref, qkv2_ref, u_ref, gate_ref, w_ref):
    @pl.when(pl.program_id(0) == 0)
    def _():
        rows_per_step = 128

        def cast_rows(i, _):
            rows = pl.ds(pl.multiple_of(i * rows_per_step, rows_per_step), rows_per_step)
            for c0 in range(0, 2 * ATTN_WIDTH, GROUP_WIDTH):
                w_ref[rows, c0:c0 + GROUP_WIDTH] = _permute_group_lanes(wf_ref[rows, c0:c0 + GROUP_WIDTH]).astype(BF16)
            for c0 in range(2 * ATTN_WIDTH, wf_ref.shape[1], GROUP_WIDTH):
                w_ref[rows, c0:c0 + GROUP_WIDTH] = wf_ref[rows, c0:c0 + GROUP_WIDTH].astype(BF16)
            return 0

        lax.fori_loop(0, wf_ref.shape[0] // rows_per_step, cast_rows, 0)

    x = x_ref[...]
    ms = jnp.mean(x * x, axis=-1, keepdims=True)
    h = (x * lax.rsqrt(ms + NORM_EPS) * g1_ref[...]).astype(BF16)
    cos = cos_ref[...]
    sin = sin_ref[...]
    seg = seg_ref[...]
    half = GROUP_WIDTH // 2

    def proj(c0, width):
        return jnp.dot(h, w_ref[:, c0:c0 + width], preferred_element_type=F32)

    def sum_squares(t):
        return t[:, :half] * t[:, :half] + t[:, half:] * t[:, half:]

    def norm_rope(t, tot, gain_ref, g):
        t1, t2 = t[:, :half], t[:, half:]
        r = lax.rsqrt(tot * (1.0 / HEAD_DIM) + NORM_EPS)
        a = t1 * r * gain_ref[:, g * GROUP_WIDTH:g * GROUP_WIDTH + half]
        b = t2 * r * gain_ref[:, g * GROUP_WIDTH + half:(g + 1) * GROUP_WIDTH]
        return a * cos - b * sin, b * cos + a * sin

    for g, out_ref in enumerate((qkv0_ref, qkv1_ref, qkv2_ref)):
        tq = proj(g * GROUP_WIDTH, GROUP_WIDTH)
        tk = proj(ATTN_WIDTH + g * GROUP_WIDTH, GROUP_WIDTH)
        ss = jnp.concatenate([sum_squares(tq), sum_squares(tk)], axis=-1).astype(BF16)
        tot = jnp.dot(ss, seg, preferred_element_type=F32)
        q1, q2 = norm_rope(tq, tot[:, :half], gq_ref, g)
        k1, k2 = norm_rope(tk, tot[:, half:], gk_ref, g)
        v = proj(2 * ATTN_WIDTH + g * GROUP_WIDTH, GROUP_WIDTH)
        out_ref[...] = jnp.concatenate([q1, q2, k1, k2, v], axis=-1).astype(BF16)
    o_u = 3 * ATTN_WIDTH
    u_w = u_ref.shape[-1]
    u_ref[...] = proj(o_u, u_w).astype(BF16)
    gate_w = gate_ref.shape[-1]
    for c in range(0, gate_w, 1024):
        gate_ref[:, c:c + 1024] = jax.nn.sigmoid(proj(o_u + u_w + c, 1024)).astype(BF16)


def _in_proj(x2, norm1, w_in, q_norm, k_norm, seq):
    t, d = x2.shape
    tm = ROW_TILE
    in_w = w_in.shape[1]
    ssm_w = d // 2
    gate_w = in_w - 3 * ATTN_WIDTH - ssm_w
    gq = (_permute_qk(q_norm.reshape(-1)) * (HEAD_DIM ** -0.5)).reshape(1, ATTN_WIDTH)
    gk = _permute_qk(k_norm.reshape(-1)).reshape(1, ATTN_WIDTH)
    inv_freq = ROPE_THETA ** (-jnp.arange(0, HEAD_DIM, 2, dtype=F32) / HEAD_DIM)
    ang = jnp.arange(seq, dtype=F32)[:, None] * inv_freq[None, :]
    cos = jnp.tile(jnp.cos(ang), (1, HEADS_PER_GROUP))
    sin = jnp.tile(jnp.sin(ang), (1, HEADS_PER_GROUP))
    lane = np.arange(GROUP_WIDTH)
    seg = jnp.asarray(lane[:, None] // (HEAD_DIM // 2) == lane[None, :] // (HEAD_DIM // 2), BF16)
    tiles_per_seq = seq // tm
    qkv_w = 3 * GROUP_WIDTH
    widths = [qkv_w] * len(ATTN_GROUPS) + [ssm_w, gate_w]
    out_shape = [jax.ShapeDtypeStruct((t, w), BF16) for w in widths]
    out_specs = [pl.BlockSpec((tm, w), lambda i: (i, 0)) for w in widths]
    return pl.pallas_call(
        _in_proj_kernel,
        grid=(t // tm,),
        in_specs=[
            pl.BlockSpec((tm, d), lambda i: (i, 0)),
            _const_spec((1, d)),
            _const_spec((d, in_w)),
            _const_spec((1, ATTN_WIDTH)),
            _const_spec((1, ATTN_WIDTH)),
            pl.BlockSpec((tm, GROUP_WIDTH // 2), lambda i: (i % tiles_per_seq, 0)),
            pl.BlockSpec((tm, GROUP_WIDTH // 2), lambda i: (i % tiles_per_seq, 0)),
            _const_spec((GROUP_WIDTH, GROUP_WIDTH)),
        ],
        out_specs=out_specs,
        out_shape=out_shape,
        scratch_shapes=[pltpu.VMEM((d, in_w), BF16)],
        compiler_params=_params("arbitrary"),
    )(x2, norm1.reshape(1, d), w_in, gq, gk, cos, sin, seg)


def _attn_kernel(c_ref, kl_ref, vl_ref, kr_ref, vr_ref, o_ref, lse_ref, kcat_ref, vcat_ref, *, length):
    tq = c_ref.shape[1]
    gw = GROUP_WIDTH
    j = pl.program_id(1)
    kcat_ref[0:BAND] = kl_ref[0]
    kcat_ref[BAND:BAND + tq] = c_ref[0, :, gw:2 * gw]
    kcat_ref[BAND + tq:] = kr_ref[0]
    vcat_ref[0:BAND] = vl_ref[0]
    vcat_ref[BAND:BAND + tq] = c_ref[0, :, 2 * gw:]
    vcat_ref[BAND + tq:] = vr_ref[0]
    sub = min(ATTN_Q_SUB, tq)
    nk = sub + 2 * BAND
    nh = HEADS_PER_GROUP
    row = lax.broadcasted_iota(jnp.int32, (nh * sub, nk), 0) & (sub - 1)
    col = lax.broadcasted_iota(jnp.int32, (nh * sub, nk), 1)
    in_band = jnp.abs(col - BAND - row) <= BAND
    q_lane = lax.broadcasted_iota(jnp.int32, (1, gw), 1)
    q_head = (q_lane % (gw // 2)) // (HEAD_DIM // 2)
    v_head = q_lane // HEAD_DIM
    for s in range(tq // sub):
        q_s = c_ref[0, s * sub:(s + 1) * sub, 0:gw]
        k_s = kcat_ref[s * sub:s * sub + nk]
        v_s = vcat_ref[s * sub:s * sub + nk]
        kpos = j * tq + s * sub - BAND + col
        valid = in_band & (kpos >= 0) & (kpos < length)
        q4 = jnp.concatenate([jnp.where(q_head == hh, q_s, jnp.zeros_like(q_s)) for hh in range(nh)], axis=0)
        sc = lax.dot_general(q4, k_s, (((1,), (1,)), ((), ())), preferred_element_type=F32)
        sc = jnp.where(valid, sc, MASK_VALUE)
        m = jnp.max(sc, axis=-1, keepdims=True)
        p = jnp.exp(sc - m)
        den = jnp.sum(p, axis=-1, keepdims=True)
        o4 = jnp.dot(p.astype(BF16), v_s, preferred_element_type=F32) * (1.0 / den)
        l4 = m + jnp.log(den)
        out = o4[0:sub]
        lse = jnp.broadcast_to(l4[0:sub], (sub, gw))
        for hh in range(1, nh):
            out = jnp.where(v_head == hh, o4[hh * sub:(hh + 1) * sub], out)
            lse = jnp.where(v_head == hh, l4[hh * sub:(hh + 1) * sub], lse)
        o_ref[0, s * sub:(s + 1) * sub] = out.astype(BF16)
        lse_ref[0, s * sub:(s + 1) * sub] = lse


def _attention(qkv, bsz, dil):
    t, qkv_w = qkv.shape
    gw = GROUP_WIDTH
    length = t // (bsz * dil)
    tq = min(ATTN_Q_TILE, length)
    hb = tq // BAND
    last = length // BAND - 1
    parts = qkv_w // gw

    def left(part):
        return pl.BlockSpec((1, BAND, gw),
                            lambda n, j: (n // dil, jnp.maximum(j * hb - 1, 0), (n % dil) * parts + part))

    def right(part):
        return pl.BlockSpec((1, BAND, gw),
                            lambda n, j: (n // dil, jnp.minimum((j + 1) * hb, last), (n % dil) * parts + part))

    centre = pl.BlockSpec((1, tq, qkv_w), lambda n, j: (n // dil, j, n % dil))
    out_spec = pl.BlockSpec((1, tq, gw), lambda n, j: (n // dil, j, n % dil))
    qkv = qkv.reshape(bsz, length, dil * qkv_w)
    o, lse = pl.pallas_call(
        functools.partial(_attn_kernel, length=length),
        grid=(bsz * dil, length // tq),
        in_specs=[centre, left(1), left(2), right(1), right(2)],
        out_specs=[out_spec, out_spec],
        out_shape=[jax.ShapeDtypeStruct((bsz, length, dil * gw), BF16),
                   jax.ShapeDtypeStruct((bsz, length, dil * gw), F32)],
        scratch_shapes=[pltpu.VMEM((tq + 2 * BAND, gw), BF16), pltpu.VMEM((tq + 2 * BAND, gw), BF16)],
        compiler_params=_params("parallel", "parallel"),
    )(qkv, qkv, qkv, qkv, qkv)
    return o.reshape(t, gw), lse.reshape(t, gw)


def _ssm_operators(lam_re, lam_im, log_dt, b_re, b_im, c_re, c_im, d_skip):
    L = SSM_CHUNK
    G, P = lam_re.shape[1:]
    C = d_skip.shape[-1]
    hp = lax.Precision.HIGHEST
    n = jnp.arange(L + 1, dtype=F32)[:, None, None]
    pad = SSM_POWER_ROWS - (L + 1)
    kern, pw, fbt, ct, consts = [], [], [], [], []
    for direction in range(2):
        lr, li = lam_re[direction].astype(F32), lam_im[direction].astype(F32)
        dt = jnp.exp(log_dt[direction].astype(F32))[:, None]
        mag = jnp.exp(lr * dt)
        ab_re, ab_im = mag * jnp.cos(li * dt), mag * jnp.sin(li * dt)
        den = lr * lr + li * li
        nr = ab_re - 1.0
        f_re = (nr * lr + ab_im * li) / den
        f_im = (ab_im * lr - nr * li) / den
        pmag = jnp.exp(n * (lr * dt)[None])
        p_re, p_im = pmag * jnp.cos(n * (li * dt)[None]), pmag * jnp.sin(n * (li * dt)[None])
        br, bi = b_re[direction].astype(F32), b_im[direction].astype(F32)
        fb_re = f_re[..., None] * br - f_im[..., None] * bi
        fb_im = f_re[..., None] * bi + f_im[..., None] * br
        cr, ci = c_re[direction].astype(F32), c_im[direction].astype(F32)
        pr, pi = p_re[:L].transpose(1, 2, 0)[..., None], p_im[:L].transpose(1, 2, 0)[..., None]
        crt, cit = cr.transpose(0, 2, 1)[:, :, None, :], ci.transpose(0, 2, 1)[:, :, None, :]
        e_re = (pr * crt - pi * cit).reshape(G, P, L * C)
        e_im = (pr * cit + pi * crt).reshape(G, P, L * C)
        kern.append((jnp.einsum('gpc,gpx->gcx', fb_re, e_re, precision=hp)
                     - jnp.einsum('gpc,gpx->gcx', fb_im, e_im, precision=hp)).reshape(G, C, L, C))
        for p in (p_re, p_im):
            p = jnp.pad(p.transpose(1, 0, 2), ((0, 0), (0, pad), (0, 0)))
            pw.append(jnp.concatenate([p, p, p, p], axis=-1))
        tr, ti = fb_re.transpose(0, 2, 1), fb_im.transpose(0, 2, 1)
        fbt.append(jnp.concatenate([tr, ti, ti, tr], axis=-1))
        fbt.append(jnp.concatenate([-ti, tr, tr, -ti], axis=-1))
        ct.append(jnp.concatenate([cr, -ci], axis=-1))
        ct.append(jnp.concatenate([-ci, -cr], axis=-1))
        a_re, a_im = p_re[L], p_im[L]
        consts.append(jnp.concatenate([a_re, a_re, a_re, a_re], axis=-1))
        consts.append(jnp.concatenate([-a_im, a_im, a_im, -a_im], axis=-1))
    kf, kb = kern
    zero = kf[:, :, 0] + kb[:, :, 0] + jnp.eye(C, dtype=F32)[None] * d_skip.astype(F32)[:, None, :]
    kfull = jnp.concatenate([kb[:, :, :0:-1], zero[:, :, None], kf[:, :, 1:], jnp.zeros_like(zero)[:, :, None]],
                            axis=2)
    flat = kfull.astype(BF16).reshape(G, C, 2 * L * C)
    return (flat, jnp.stack(pw, axis=1), jnp.stack(fbt, axis=1), jnp.stack(ct, axis=1),
            jnp.stack(consts + consts, axis=1))


def _ssm_kernel(u_ref, flat_ref, pw_ref, fbt_ref, ct_ref, cst_ref, z_ref,
                toe_ref, wst_ref, vt_ref, f_ref, s_ref, *, bsz):
    u = u_ref[0]
    nc = u.shape[0]
    sw = cst_ref.shape[-1]
    hw = sw // 2
    C = fbt_ref.shape[2]
    L = toe_ref.shape[0] // C
    lc = L * C

    per_tile = LANES // C
    span = (2 * L - per_tile) * C
    for b in range(per_tile):
        shifted = flat_ref[0, :, (per_tile - 1 - b) * C:(per_tile - 1 - b) * C + span]
        for a in range(L // per_tile):
            t_in = a * per_tile + b
            off = (L // per_tile - 1 - a) * LANES
            toe_ref[t_in * C:(t_in + 1) * C, :] = shifted[:, off:off + lc]

    fa_f, fb_f, fa_b, fb_b = fbt_ref[0, 0], fbt_ref[0, 1], fbt_ref[0, 2], fbt_ref[0, 3]
    ca_f, cb_f, ca_b, cb_b = ct_ref[0, 0], ct_ref[0, 1], ct_ref[0, 2], ct_ref[0, 3]

    def power(k, n):
        return pw_ref[0, k, n:n + 1, :]

    for t in range(L):
        rows = slice(t * C, (t + 1) * C)
        wst_ref[rows, 0:sw] = (fa_f * power(0, L - 1 - t) + fb_f * power(1, L - 1 - t)).astype(BF16)
        wst_ref[rows, sw:2 * sw] = (fa_b * power(2, t) + fb_b * power(3, t)).astype(BF16)
        vt_ref[rows, 0:hw] = (ca_f * power(0, t + 1)[:, :hw] + cb_f * power(1, t + 1)[:, :hw]).astype(BF16)
        vt_ref[rows, hw:sw] = (ca_b * power(2, L - t)[:, :hw] + cb_b * power(3, L - t)[:, :hw]).astype(BF16)

    f_ref[...] = jnp.dot(u, wst_ref[...], preferred_element_type=F32)
    cst = cst_ref[0]
    a1f, a2f, a1b, a2b = cst[0:1], cst[1:2], cst[2:3], cst[3:4]
    per = nc // bsz

    def swap(v):
        return jnp.concatenate([v[:, hw:], v[:, :hw]], axis=-1)

    rows = 8

    def step(i, carry):
        new = []
        for b in range(bsz):
            sf, sb = carry[2 * b], carry[2 * b + 1]
            base_f = pl.multiple_of(b * per + i * rows, rows)
            base_b = pl.multiple_of(b * per + per - rows - i * rows, rows)
            f_blk = f_ref[pl.ds(base_f, rows), 0:sw]
            b_blk = f_ref[pl.ds(base_b, rows), sw:2 * sw]
            f_rows, b_rows = [], []
            for r in range(rows):
                f_rows.append(sf[:, :hw])
                sf = a1f * sf + a2f * swap(sf) + f_blk[r:r + 1]
                b_rows.append(sb[:, :hw])
                sb = a1b * sb + a2b * swap(sb) + b_blk[rows - 1 - r:rows - r]
            s_ref[pl.ds(base_f, rows), 0:hw] = jnp.concatenate(f_rows, axis=0)
            s_ref[pl.ds(base_b, rows), hw:sw] = jnp.concatenate(b_rows[::-1], axis=0)
            new += [sf, sb]
        return tuple(new)

    lax.fori_loop(0, per // rows, step, tuple(jnp.zeros((1, sw), F32) for _ in range(2 * bsz)))
    y = jnp.dot(u, toe_ref[...], preferred_element_type=F32)
    y = y + lax.dot_general(s_ref[...].astype(BF16), vt_ref[...], (((1,), (1,)), ((), ())),
                            preferred_element_type=F32)
    z_ref[0] = jax.nn.gelu(y).astype(BF16)


def _ssm(u, bsz, lam_re, lam_im, log_dt, b_re, b_im, c_re, c_im, d_skip):
    t, width = u.shape
    L, C = SSM_CHUNK, d_skip.shape[-1]
    G = width // C
    nc = t // L
    flat, pw, fbt, ct, consts = _ssm_operators(lam_re, lam_im, log_dt, b_re, b_im, c_re, c_im, d_skip)
    sw = consts.shape[-1]
    ug = u.reshape(nc, L, G, C).transpose(2, 0, 1, 3).reshape(G, nc, L * C)

    def group_spec(a):
        return pl.BlockSpec((1,) + a.shape[1:], lambda g: (g,) + (0,) * (a.ndim - 1))

    z = pl.pallas_call(
        functools.partial(_ssm_kernel, bsz=bsz),
        grid=(G,),
        in_specs=[group_spec(a) for a in (ug, flat, pw, fbt, ct, consts)],
        out_specs=pl.BlockSpec((1, nc, L * C), lambda g: (g, 0, 0)),
        out_shape=jax.ShapeDtypeStruct((G, nc, L * C), BF16),
        scratch_shapes=[pltpu.VMEM((L * C, L * C), BF16), pltpu.VMEM((L * C, 2 * sw), BF16),
                        pltpu.VMEM((L * C, sw), BF16), pltpu.VMEM((nc, 2 * sw), F32), pltpu.VMEM((nc, sw), F32)],
        compiler_params=_params("parallel"),
    )(ug, flat, pw, fbt, ct, consts)
    return z.reshape(G, nc, L, C).transpose(1, 2, 0, 3).reshape(t, width)


def _post_kernel(o0_ref, l0_ref, o1_ref, l1_ref, o2_ref, l2_ref, z_ref, gate_ref, x_ref,
                 wa_ref, wglu_ref, bglu_ref, ws_ref, wo_ref, g2_ref, wr_ref, wrlo_ref, br_ref, tri_ref,
                 x1_ref, hp_ref, idx_ref, rank_ref, wgt_ref, cnt_ref, carry_ref):
    tm, d = x_ref.shape

    @pl.when(pl.program_id(0) == 0)
    def _():
        carry_ref[...] = jnp.zeros_like(carry_ref)

    outs = [o_ref[...].astype(F32) for o_ref in (o0_ref, o1_ref, o2_ref)]
    lses = [l_ref[...] for l_ref in (l0_ref, l1_ref, l2_ref)]
    top = jnp.maximum(jnp.maximum(lses[0], lses[1]), lses[2])
    es = [jnp.exp(l - top) for l in lses]
    y_attn = (es[0] * outs[0] + es[1] * outs[1] + es[2] * outs[2]) / (es[0] + es[1] + es[2])
    ya = jnp.dot(y_attn.astype(BF16), wa_ref[...], preferred_element_type=F32)
    z = z_ref[...]
    glu = jnp.dot(z, wglu_ref[...], preferred_element_type=F32) + bglu_ref[...]
    zz = z.astype(F32) * jax.nn.sigmoid(glu)
    ys = jnp.dot(zz.astype(BF16), ws_ref[...], preferred_element_type=F32)
    mixed = gate_ref[:, :d].astype(F32) * ya + gate_ref[:, d:].astype(F32) * ys
    x1 = x_ref[...] + jnp.dot(mixed.astype(BF16), wo_ref[...], preferred_element_type=F32)
    x1_ref[...] = x1
    ms = jnp.mean(x1 * x1, axis=-1, keepdims=True)
    h2 = x1 * lax.rsqrt(ms + NORM_EPS) * g2_ref[...]
    hb = h2.astype(BF16)
    hp_ref[...] = h2

    h_lo = (h2 - hb.astype(F32)).astype(BF16)
    nt = (((1,), (1,)), ((), ()))
    logits = (lax.dot_general(wr_ref[...], hb, nt, preferred_element_type=F32)
              + lax.dot_general(wrlo_ref[...], hb, nt, preferred_element_type=F32)
              + lax.dot_general(wr_ref[...], h_lo, nt, preferred_element_type=F32) + br_ref[...])
    ne = logits.shape[0]
    e_iota = lax.broadcasted_iota(jnp.int32, logits.shape, 0)
    vals, idxs = [], []
    work = logits
    for _ in range(TOP_K):
        m = jnp.max(work, axis=0, keepdims=True)
        i = jnp.min(jnp.where(work == m, e_iota, ne), axis=0, keepdims=True)
        vals.append(m)
        idxs.append(i)
        work = jnp.where(e_iota == i, -jnp.inf, work)
    ex = [jnp.exp(v - vals[0]) for v in vals]
    tot = ex[0] + ex[1] + ex[2] + ex[3]
    onehot = jnp.zeros(logits.shape, F32)
    for i in idxs:
        onehot = onehot + (e_iota == i).astype(F32)
    before = jnp.dot(onehot.astype(BF16), tri_ref[...], preferred_element_type=F32) + carry_ref[:, 0:1]
    for k in range(TOP_K):
        idx_ref[k:k + 1, :] = idxs[k]
        rank_ref[k:k + 1, :] = jnp.sum(jnp.where(e_iota == idxs[k], before, 0.0), axis=0,
                                       keepdims=True).astype(jnp.int32)
        wgt_ref[k:k + 1, :] = ex[k] / tot
    carry_ref[...] = carry_ref[...] + jnp.sum(onehot, axis=1, keepdims=True)
    cnt_ref[...] = carry_ref[...].astype(jnp.int32)


def _post_mix(x2, attn, z, gates, w_attn_branch, w_glu, b_glu, w_ssm_branch, w_out, norm2, w_router, b_router):
    t, d = x2.shape
    tm = ROW_TILE
    gw = GROUP_WIDTH
    ne = w_router.shape[1]
    ssm_w = z.shape[1]
    args = [a for o_lse in attn for a in o_lse]
    in_specs = [pl.BlockSpec((tm, gw), lambda i: (i, 0)) for _ in args]
    tri = jnp.asarray(np.arange(tm)[:, None] < np.arange(tm)[None, :], BF16)
    in_specs += [pl.BlockSpec((tm, ssm_w), lambda i: (i, 0)),
                 pl.BlockSpec((tm, 2 * d), lambda i: (i, 0)),
                 pl.BlockSpec((tm, d), lambda i: (i, 0)),
                 _const_spec((gw, d)), _const_spec((ssm_w, ssm_w)), _const_spec((1, ssm_w)),
                 _const_spec((ssm_w, d)), _const_spec((d, d)), _const_spec((1, d)),
                 _const_spec((ne, d)), _const_spec((ne, d)), _const_spec((ne, 1)), _const_spec((tm, tm))]
    wr_t = w_router.T.astype(F32)
    wr_hi = wr_t.astype(BF16)
    wr_lo = (wr_t - wr_hi.astype(F32)).astype(BF16)
    args += [z, gates, x2, w_attn_branch.astype(BF16), w_glu.astype(BF16), b_glu.reshape(1, ssm_w),
             w_ssm_branch.astype(BF16), w_out.astype(BF16), norm2.reshape(1, d),
             wr_hi, wr_lo, b_router.reshape(ne, 1), tri]
    tok_spec = pl.BlockSpec((TOP_K, tm), lambda i: (0, i))
    return pl.pallas_call(
        _post_kernel,
        grid=(t // tm,),
        in_specs=in_specs,
        out_specs=[pl.BlockSpec((tm, d), lambda i: (i, 0)),
                   pl.BlockSpec((tm, d), lambda i: (i, 0)),
                   tok_spec, tok_spec, tok_spec, pl.BlockSpec((ne, LANES), lambda i: (0, 0))],
        out_shape=[jax.ShapeDtypeStruct((t, d), F32), jax.ShapeDtypeStruct((t, d), F32),
                   jax.ShapeDtypeStruct((TOP_K, t), jnp.int32), jax.ShapeDtypeStruct((TOP_K, t), jnp.int32),
                   jax.ShapeDtypeStruct((TOP_K, t), F32), jax.ShapeDtypeStruct((ne, LANES), jnp.int32)],
        scratch_shapes=[pltpu.VMEM((ne, LANES), F32)],
        compiler_params=_params("arbitrary"),
    )(*args)


def _row_copy(src_ref, src_row, dst_ref, dst_row, sem):
    return pltpu.make_async_copy(src_ref.at[pl.ds(src_row, 1)], dst_ref.at[pl.ds(dst_row, 1)], sem)


def _rows_wait(ref, n_rows, sem):
    pltpu.make_async_copy(ref.at[pl.ds(0, n_rows)], ref.at[pl.ds(0, n_rows)], sem).wait()


def _dispatch_kernel(be_ref, nu_ref, dest_ref, h_ref, xs_ref, zero_ref, sems):
    i = pl.program_id(0)
    tm = dest_ref.shape[1]
    tb = zero_ref.shape[0]
    n_blocks = be_ref.shape[0]

    @pl.when(i == 0)
    def _():
        zero_ref[...] = jnp.zeros_like(zero_ref)

        def holds_padding(b):
            return (b >= nu_ref[0] - 1) | (be_ref[b] != be_ref[jnp.minimum(b + 1, n_blocks - 1)])

        def zero_copy(b):
            return pltpu.make_async_copy(zero_ref, xs_ref.at[pl.ds(pl.multiple_of(b * tb, tb), tb)], sems.at[1])

        def start(b, c):
            @pl.when(holds_padding(b))
            def _():
                zero_copy(b).start()
            return c

        def wait(b, c):
            @pl.when(holds_padding(b))
            def _():
                zero_copy(b).wait()
            return c

        lax.fori_loop(0, n_blocks, start, 0)
        lax.fori_loop(0, n_blocks, wait, 0)

    def issue(t, c):
        for k in range(TOP_K):
            _row_copy(h_ref, t, xs_ref, dest_ref[k, t], sems.at[0]).start()
        return c

    lax.fori_loop(0, tm, issue, 0, unroll=4)
    _rows_wait(xs_ref, TOP_K * tm, sems.at[0])


def _dispatch(block_exp, n_used, dest, hp, n_rows):
    t = hp.shape[0]
    slab = hp.shape[1:]
    tm = DISPATCH_TILE
    grid_spec = pltpu.PrefetchScalarGridSpec(
        num_scalar_prefetch=2,
        grid=(t // tm,),
        in_specs=[pl.BlockSpec((TOP_K, tm), lambda i, be, nu: (0, i), memory_space=pltpu.SMEM),
                  pl.BlockSpec((tm,) + slab, lambda i, be, nu: (i,) + (0,) * len(slab))],
        out_specs=pl.BlockSpec(memory_space=pl.ANY),
        scratch_shapes=[pltpu.VMEM((EXPERT_ROWS,) + slab, hp.dtype), pltpu.SemaphoreType.DMA((2,))],
    )
    return pl.pallas_call(
        _dispatch_kernel,
        grid_spec=grid_spec,
        out_shape=jax.ShapeDtypeStruct((n_rows,) + slab, hp.dtype),
        compiler_params=_params("arbitrary"),
    )(block_exp, n_used, dest, hp)


def _expert_kernel(be_ref, nu_ref, xs_ref, wgu_ref, bgu_ref, wd_ref, bd_ref, ys_ref, wgu_bf, wd_bf):
    i = pl.program_id(0)
    active = i < nu_ref[0]

    @pl.when(active & ((i == 0) | (be_ref[i] != be_ref[jnp.maximum(i - 1, 0)])))
    def _():
        wgu_bf[...] = wgu_ref[0].astype(BF16)
        wd_bf[...] = wd_ref[0].astype(BF16)

    @pl.when(active)
    def _():
        gu = jnp.dot(xs_ref[...].astype(BF16), wgu_bf[...], preferred_element_type=F32) + bgu_ref[0]
        de = gu.shape[1] // 2
        gate = jnp.minimum(gu[:, :de], SWIGLU_LIMIT)
        up = jnp.clip(gu[:, de:], -SWIGLU_LIMIT, SWIGLU_LIMIT)
        act = gate * jax.nn.sigmoid(SWIGLU_ALPHA * gate) * (up + 1.0)
        ys_ref[...] = jnp.dot(act.astype(BF16), wd_bf[...], preferred_element_type=F32) + bd_ref[0]

    @pl.when(i >= nu_ref[0])
    def _():
        ys_ref[...] = jnp.zeros_like(ys_ref)


def _experts(block_exp, n_used, xs, w_gate_up, b_gate_up, w_down, b_down):
    n_rows = xs.shape[0]
    ne, d, de2 = w_gate_up.shape
    tb = EXPERT_ROWS

    def row_block(i, be, nu):
        return (jnp.minimum(i, nu[0] - 1), 0)

    def expert_block(i, be, nu):
        return (be[jnp.minimum(i, nu[0] - 1)], 0, 0)

    grid_spec = pltpu.PrefetchScalarGridSpec(
        num_scalar_prefetch=2,
        grid=(n_rows // tb,),
        in_specs=[pl.BlockSpec((tb,) + xs.shape[1:], row_block),
                  pl.BlockSpec((1, d, de2), expert_block),
                  pl.BlockSpec((1, 1, de2), expert_block),
                  pl.BlockSpec((1, de2 // 2, d), expert_block),
                  pl.BlockSpec((1, 1, d), expert_block)],
        out_specs=pl.BlockSpec((tb, d), lambda i, be, nu: (i, 0)),
        scratch_shapes=[pltpu.VMEM((d, de2), BF16), pltpu.VMEM((de2 // 2, d), BF16)],
    )
    return pl.pallas_call(
        _expert_kernel,
        grid_spec=grid_spec,
        out_shape=jax.ShapeDtypeStruct((n_rows, d), F32),
        compiler_params=_params("arbitrary"),
    )(block_exp, n_used, xs, w_gate_up, b_gate_up.reshape(ne, 1, de2), w_down, b_down.reshape(ne, 1, d))


def _combine_kernel(dest_ref, dest_next_ref, wgt_ref, x1_ref, ys_ref, out_ref, buf_ref, sems):
    i = pl.program_id(0)
    n = pl.num_programs(0)
    tm = x1_ref.shape[0]
    slot = i % 2

    def gather(d_ref, into):
        def issue(t, c):
            for k in range(TOP_K):
                _row_copy(ys_ref, d_ref[k, t], buf_ref.at[into, k], t, sems.at[into]).start()
            return c

        lax.fori_loop(0, tm, issue, 0, unroll=4)

    @pl.when(i == 0)
    def _():
        gather(dest_ref, 0)

    @pl.when(i + 1 < n)
    def _():
        gather(dest_next_ref, 1 - slot)

    for k in range(TOP_K):
        _rows_wait(buf_ref.at[slot, k], tm, sems.at[slot])
    acc = x1_ref[...]
    for k in range(TOP_K):
        acc = acc + wgt_ref[:, k:k + 1] * buf_ref[slot, k]
    out_ref[...] = acc


def _combine(dest, wgt_t, x1, ys):
    t, d = x1.shape
    tm = ROUTE_TILE
    n = t // tm
    return pl.pallas_call(
        _combine_kernel,
        grid=(n,),
        in_specs=[pl.BlockSpec((TOP_K, tm), lambda i: (0, i), memory_space=pltpu.SMEM),
                  pl.BlockSpec((TOP_K, tm), lambda i: (0, jnp.minimum(i + 1, n - 1)), memory_space=pltpu.SMEM),
                  pl.BlockSpec((tm, TOP_K), lambda i: (i, 0)),
                  pl.BlockSpec((tm, d), lambda i: (i, 0)),
                  pl.BlockSpec(memory_space=pl.ANY)],
        out_specs=pl.BlockSpec((tm, d), lambda i: (i, 0)),
        out_shape=jax.ShapeDtypeStruct((t, d), F32),
        scratch_shapes=[pltpu.VMEM((2, TOP_K, tm) + ys.shape[1:], F32), pltpu.SemaphoreType.DMA((2,))],
        compiler_params=_params("arbitrary"),
    )(dest, dest, wgt_t, x1, ys)


def _moe(x1, hp, idx, rank, wgt, counts, w_gate_up, b_gate_up, w_down, b_down):
    t = x1.shape[0]
    tb = EXPERT_ROWS
    ne = w_gate_up.shape[0]
    n_blocks = -(-(t * TOP_K) // tb) + ne
    padded = ((counts + tb - 1) // tb) * tb
    pad_end = jnp.cumsum(padded)
    pad_start = pad_end - padded
    dest = rank + jnp.sum(jnp.where(idx[None] == jnp.arange(ne, dtype=jnp.int32)[:, None, None],
                                    pad_start.astype(jnp.int32)[:, None, None], 0), axis=0)
    block_start = jnp.arange(n_blocks, dtype=jnp.int32) * tb
    block_exp = jnp.minimum(jnp.sum((pad_end[None, :] <= block_start[:, None]).astype(jnp.int32), axis=1), ne - 1)
    n_used = (pad_end[-1:] // tb).astype(jnp.int32)
    xs = _dispatch(block_exp, n_used, dest, hp, n_blocks * tb)
    ys = _experts(block_exp, n_used, xs, w_gate_up, b_gate_up, w_down, b_down)
    return _combine(dest, wgt.T, x1, ys)


def kernel(x, norm1, w_in, q_norm, k_norm, lam_re, lam_im, log_dt, b_re, b_im, c_re, c_im, d_skip, w_glu, b_glu, w_attn_branch, w_ssm_branch, w_out, norm2, w_router, b_router, w_gate_up, b_gate_up, w_down, b_down):
    bsz, seq, d = x.shape
    t = bsz * seq
    x2 = x.reshape(t, d)
    for l in range(norm1.shape[0]):
        *qkvs, u, gates = _in_proj(x2, norm1[l], w_in[l], q_norm[l], k_norm[l], seq)
        attn = [_attention(qkv, bsz, dil) for qkv, (_, dil) in zip(qkvs, ATTN_GROUPS)]
        z = _ssm(u, bsz, lam_re[l], lam_im[l], log_dt[l], b_re[l], b_im[l], c_re[l], c_im[l], d_skip[l])
        x1, hp, idx, rank, wgt, counts = _post_mix(
            x2, attn, z, gates, w_attn_branch[l], w_glu[l], b_glu[l], w_ssm_branch[l], w_out[l], norm2[l],
            w_router[l], b_router[l])
        x2 = _moe(x1, hp, idx, rank, wgt, counts[:, 0], w_gate_up[l], b_gate_up[l], w_down[l], b_down[l])
    return x2.reshape(bsz, seq, d)
```

```python
import functools
import math

import numpy as np
import jax
import jax.numpy as jnp
from jax import lax
from jax.experimental import pallas as pl
from jax.experimental.pallas import tpu as pltpu

F32 = jnp.float32
BF16 = jnp.bfloat16

HEAD_DIM = 64
HEADS_PER_GROUP = 4
ATTN_GROUPS = ((128, 1), (512, 4), (2048, 16))
GROUP_WIDTH = HEADS_PER_GROUP * HEAD_DIM
ATTN_WIDTH = GROUP_WIDTH * len(ATTN_GROUPS)
BAND = 64
ROPE_THETA = 10000.0
SSM_GROUP = 16
SSM_STATE = 64
N_EXPERTS = 32
TOP_K = 4
SWIGLU_ALPHA = 1.702
SWIGLU_LIMIT = 7.0
NORM_EPS = 1e-6
MASK_VALUE = -1e30

ROW_TILE = 512
ATTN_Q_TILE = 1024
ATTN_Q_SUB = 128
SSM_CHUNK = 64
SSM_POWER_ROWS = -(-(SSM_CHUNK + 1) // 8) * 8
DISPATCH_TILE = 1024
ROUTE_TILE = 512
EXPERT_ROWS = 512
VMEM_LIMIT = 56 * 1024 * 1024
LANES = 128


def _params(*sem):
    return pltpu.CompilerParams(dimension_semantics=sem, vmem_limit_bytes=VMEM_LIMIT)


def _const_spec(shape):
    return pl.BlockSpec(shape, lambda *_: (0,) * len(shape), pipeline_mode=pl.Buffered(1))


def _permute_qk(a):
    lead = a.shape[:-1]
    n = len(lead)
    a = a.reshape(*lead, len(ATTN_GROUPS), HEADS_PER_GROUP, 2, HEAD_DIM // 2)
    return a.transpose(*range(n), n, n + 2, n + 1, n + 3).reshape(*lead, ATTN_WIDTH)


def _permute_group_lanes(piece):
    hd, hh = HEAD_DIM, HEAD_DIM // 2
    firsts = [piece[:, h * hd:h * hd + hh] for h in range(HEADS_PER_GROUP)]
    seconds = [piece[:, h * hd + hh:(h + 1) * hd] for h in range(HEADS_PER_GROUP)]
    return jnp.concatenate(firsts + seconds, axis=-1)


def _in_proj_kernel(x_ref, g1_ref, wf_ref, gq_ref, gk_ref, cos_ref, sin_ref, seg_ref,
                    qkv0_ref, qkv1_ref, qkv2_ref, u_ref, gate_ref, w_ref, scr_ref):
    @pl.when(pl.program_id(0) == 0)
    def _():
        rows_per_step = 128

        def cast_rows(i, _):
            rows = pl.ds(pl.multiple_of(i * rows_per_step, rows_per_step), rows_per_step)
            for c0 in range(0, 2 * ATTN_WIDTH, GROUP_WIDTH):
                w_ref[rows, c0:c0 + GROUP_WIDTH] = _permute_group_lanes(wf_ref[rows, c0:c0 + GROUP_WIDTH]).astype(BF16)
            for c0 in range(2 * ATTN_WIDTH, wf_ref.shape[1], GROUP_WIDTH):
                w_ref[rows, c0:c0 + GROUP_WIDTH] = wf_ref[rows, c0:c0 + GROUP_WIDTH].astype(BF16)
            return 0

        lax.fori_loop(0, wf_ref.shape[0] // rows_per_step, cast_rows, 0)

    x = x_ref[...]
    ms = jnp.mean(x * x, axis=-1, keepdims=True)
    h = (x * lax.rsqrt(ms + NORM_EPS) * g1_ref[...]).astype(BF16)
    cos = cos_ref[...]
    sin = sin_ref[...]
    seg = seg_ref[...]
    half = GROUP_WIDTH // 2
    tm = x.shape[0]

    def proj(c0, width):
        return jnp.dot(h, w_ref[:, c0:c0 + width], preferred_element_type=F32)

    def sum_squares(t):
        return t[:, :half] * t[:, :half] + t[:, half:] * t[:, half:]

    def norm_rope(t, tot, gain_ref, g):
        t1, t2 = t[:, :half], t[:, half:]
        r = lax.rsqrt(tot * (1.0 / HEAD_DIM) + NORM_EPS)
        a = t1 * r * gain_ref[:, g * GROUP_WIDTH:g * GROUP_WIDTH + half]
        b = t2 * r * gain_ref[:, g * GROUP_WIDTH + half:(g + 1) * GROUP_WIDTH]
        return a * cos - b * sin, b * cos + a * sin

    outs = (qkv0_ref, qkv1_ref, qkv2_ref)
    for g, (_, dil) in enumerate(ATTN_GROUPS):
        tq = proj(g * GROUP_WIDTH, GROUP_WIDTH)
        tk = proj(ATTN_WIDTH + g * GROUP_WIDTH, GROUP_WIDTH)
        ss = jnp.concatenate([sum_squares(tq), sum_squares(tk)], axis=-1).astype(BF16)
        tot = jnp.dot(ss, seg, preferred_element_type=F32)
        q1, q2 = norm_rope(tq, tot[:, :half], gq_ref, g)
        k1, k2 = norm_rope(tk, tot[:, half:], gk_ref, g)
        v = proj(2 * ATTN_WIDTH + g * GROUP_WIDTH, GROUP_WIDTH)
        qkv = jnp.concatenate([q1, q2, k1, k2, v], axis=-1)
        if dil == 1:
            outs[g][...] = qkv.astype(BF16)
        else:
            for c in range(qkv.shape[1] // LANES):
                scr_ref[c] = qkv[:, c * LANES:(c + 1) * LANES]
            for r in range(dil):
                for c in range(qkv.shape[1] // LANES):
                    outs[g][r, :, c * LANES:(c + 1) * LANES] = (
                        scr_ref[c, pl.ds(r, tm // dil, stride=dil), :].astype(BF16))
    o_u = 3 * ATTN_WIDTH
    u_w = u_ref.shape[-1]
    u_ref[...] = proj(o_u, u_w).astype(BF16)
    gate_w = gate_ref.shape[-1]
    for c in range(0, gate_w, 1024):
        gate_ref[:, c:c + 1024] = jax.nn.sigmoid(proj(o_u + u_w + c, 1024)).astype(BF16)


def _in_proj(x2, norm1, w_in, q_norm, k_norm, seq):
    t, d = x2.shape
    tm = ROW_TILE
    in_w = w_in.shape[1]
    ssm_w = d // 2
    gate_w = in_w - 3 * ATTN_WIDTH - ssm_w
    gq = (_permute_qk(q_norm.reshape(-1)) * (HEAD_DIM ** -0.5)).reshape(1, ATTN_WIDTH)
    gk = _permute_qk(k_norm.reshape(-1)).reshape(1, ATTN_WIDTH)
    inv_freq = ROPE_THETA ** (-jnp.arange(0, HEAD_DIM, 2, dtype=F32) / HEAD_DIM)
    ang = jnp.arange(seq, dtype=F32)[:, None] * inv_freq[None, :]
    cos = jnp.tile(jnp.cos(ang), (1, HEADS_PER_GROUP))
    sin = jnp.tile(jnp.sin(ang), (1, HEADS_PER_GROUP))
    lane = np.arange(GROUP_WIDTH)
    seg = jnp.asarray(lane[:, None] // (HEAD_DIM // 2) == lane[None, :] // (HEAD_DIM // 2), BF16)
    bsz = t // seq
    tiles_per_seq = seq // tm
    qkv_w = 3 * GROUP_WIDTH
    out_shape = [jax.ShapeDtypeStruct((t, qkv_w), BF16)]
    out_specs = [pl.BlockSpec((tm, qkv_w), lambda i: (i, 0))]
    for _, dil in ATTN_GROUPS[1:]:
        out_shape.append(jax.ShapeDtypeStruct((bsz * dil, seq // dil, qkv_w), BF16))
        out_specs.append(pl.BlockSpec((dil, tm // dil, qkv_w),
                                      lambda i: (i // tiles_per_seq, i % tiles_per_seq, 0)))
    out_shape += [jax.ShapeDtypeStruct((t, ssm_w), BF16), jax.ShapeDtypeStruct((t, gate_w), BF16)]
    out_specs += [pl.BlockSpec((tm, ssm_w), lambda i: (i, 0)), pl.BlockSpec((tm, gate_w), lambda i: (i, 0))]
    return pl.pallas_call(
        _in_proj_kernel,
        grid=(t // tm,),
        in_specs=[
            pl.BlockSpec((tm, d), lambda i: (i, 0)),
            _const_spec((1, d)),
            _const_spec((d, in_w)),
            _const_spec((1, ATTN_WIDTH)),
            _const_spec((1, ATTN_WIDTH)),
            pl.BlockSpec((tm, GROUP_WIDTH // 2), lambda i: (i % tiles_per_seq, 0)),
            pl.BlockSpec((tm, GROUP_WIDTH // 2), lambda i: (i % tiles_per_seq, 0)),
            _const_spec((GROUP_WIDTH, GROUP_WIDTH)),
        ],
        out_specs=out_specs,
        out_shape=out_shape,
        scratch_shapes=[pltpu.VMEM((d, in_w), BF16), pltpu.VMEM((qkv_w // LANES, tm, LANES), F32)],
        compiler_params=_params("arbitrary"),
    )(x2, norm1.reshape(1, d), w_in, gq, gk, cos, sin, seg)


def _attn_kernel(c_ref, kl_ref, vl_ref, kr_ref, vr_ref, o_ref, lse_ref, kcat_ref, vcat_ref, *, length):
    tq = c_ref.shape[1]
    gw = GROUP_WIDTH
    j = pl.program_id(1)
    kcat_ref[0:BAND] = kl_ref[0]
    kcat_ref[BAND:BAND + tq] = c_ref[0, :, gw:2 * gw]
    kcat_ref[BAND + tq:] = kr_ref[0]
    vcat_ref[0:BAND] = vl_ref[0]
    vcat_ref[BAND:BAND + tq] = c_ref[0, :, 2 * gw:]
    vcat_ref[BAND + tq:] = vr_ref[0]
    sub = min(ATTN_Q_SUB, tq)
    nk = sub + 2 * BAND
    nh = HEADS_PER_GROUP
    row = lax.broadcasted_iota(jnp.int32, (nh * sub, nk), 0) & (sub - 1)
    col = lax.broadcasted_iota(jnp.int32, (nh * sub, nk), 1)
    in_band = jnp.abs(col - BAND - row) <= BAND
    q_lane = lax.broadcasted_iota(jnp.int32, (1, gw), 1)
    q_head = (q_lane % (gw // 2)) // (HEAD_DIM // 2)
    v_head = q_lane // HEAD_DIM
    for s in range(tq // sub):
        q_s = c_ref[0, s * sub:(s + 1) * sub, 0:gw]
        k_s = kcat_ref[s * sub:s * sub + nk]
        v_s = vcat_ref[s * sub:s * sub + nk]
        kpos = j * tq + s * sub - BAND + col
        valid = in_band & (kpos >= 0) & (kpos < length)
        q4 = jnp.concatenate([jnp.where(q_head == hh, q_s, jnp.zeros_like(q_s)) for hh in range(nh)], axis=0)
        sc = lax.dot_general(q4, k_s, (((1,), (1,)), ((), ())), preferred_element_type=F32)
        sc = jnp.where(valid, sc, MASK_VALUE)
        m = jnp.max(sc, axis=-1, keepdims=True)
        p = jnp.exp(sc - m)
        den = jnp.sum(p, axis=-1, keepdims=True)
        o4 = jnp.dot(p.astype(BF16), v_s, preferred_element_type=F32) * (1.0 / den)
        l4 = m + jnp.log(den)
        out = o4[0:sub]
        lse = jnp.broadcast_to(l4[0:sub], (sub, gw))
        for hh in range(1, nh):
            out = jnp.where(v_head == hh, o4[hh * sub:(hh + 1) * sub], out)
            lse = jnp.where(v_head == hh, l4[hh * sub:(hh + 1) * sub], lse)
        o_ref[0, s * sub:(s + 1) * sub] = out.astype(BF16)
        lse_ref[0, s * sub:(s + 1) * sub] = lse


def _attention(qkv):
    n, length, _ = qkv.shape
    tq = min(ATTN_Q_TILE, length)
    gw = GROUP_WIDTH
    hb = tq // BAND
    last = length // BAND - 1

    def left(lane_block):
        return pl.BlockSpec((1, BAND, gw), lambda b, j: (b, jnp.maximum(j * hb - 1, 0), lane_block))

    def right(lane_block):
        return pl.BlockSpec((1, BAND, gw), lambda b, j: (b, jnp.minimum((j + 1) * hb, last), lane_block))

    return pl.pallas_call(
        functools.partial(_attn_kernel, length=length),
        grid=(n, length // tq),
        in_specs=[pl.BlockSpec((1, tq, 3 * gw), lambda b, j: (b, j, 0)), left(1), left(2), right(1), right(2)],
        out_specs=[pl.BlockSpec((1, tq, gw), lambda b, j: (b, j, 0)),
                   pl.BlockSpec((1, tq, gw), lambda b, j: (b, j, 0))],
        out_shape=[jax.ShapeDtypeStruct((n, length, gw), BF16), jax.ShapeDtypeStruct((n, length, gw), F32)],
        scratch_shapes=[pltpu.VMEM((tq + 2 * BAND, gw), BF16), pltpu.VMEM((tq + 2 * BAND, gw), BF16)],
        compiler_params=_params("parallel", "parallel"),
    )(qkv, qkv, qkv, qkv, qkv)


def _ssm_operators(lam_re, lam_im, log_dt, b_re, b_im, c_re, c_im, d_skip):
    L = SSM_CHUNK
    G, P = lam_re.shape[1:]
    C = d_skip.shape[-1]
    hp = lax.Precision.HIGHEST
    n = jnp.arange(L + 1, dtype=F32)[:, None, None]
    pad = SSM_POWER_ROWS - (L + 1)
    kern, pw, fbt, ct, consts = [], [], [], [], []
    for direction in range(2):
        lr, li = lam_re[direction].astype(F32), lam_im[direction].astype(F32)
        dt = jnp.exp(log_dt[direction].astype(F32))[:, None]
        mag = jnp.exp(lr * dt)
        ab_re, ab_im = mag * jnp.cos(li * dt), mag * jnp.sin(li * dt)
        den = lr * lr + li * li
        nr = ab_re - 1.0
        f_re = (nr * lr + ab_im * li) / den
        f_im = (ab_im * lr - nr * li) / den
        pmag = jnp.exp(n * (lr * dt)[None])
        p_re, p_im = pmag * jnp.cos(n * (li * dt)[None]), pmag * jnp.sin(n * (li * dt)[None])
        br, bi = b_re[direction].astype(F32), b_im[direction].astype(F32)
        fb_re = f_re[..., None] * br - f_im[..., None] * bi
        fb_im = f_re[..., None] * bi + f_im[..., None] * br
        cr, ci = c_re[direction].astype(F32), c_im[direction].astype(F32)
        pr, pi = p_re[:L].transpose(1, 2, 0)[..., None], p_im[:L].transpose(1, 2, 0)[..., None]
        crt, cit = cr.transpose(0, 2, 1)[:, :, None, :], ci.transpose(0, 2, 1)[:, :, None, :]
        e_re = (pr * crt - pi * cit).reshape(G, P, L * C)
        e_im = (pr * cit + pi * crt).reshape(G, P, L * C)
        kern.append((jnp.einsum('gpc,gpx->gcx', fb_re, e_re, precision=hp)
                     - jnp.einsum('gpc,gpx->gcx', fb_im, e_im, precision=hp)).reshape(G, C, L, C))
        for p in (p_re, p_im):
            p = jnp.pad(p.transpose(1, 0, 2), ((0, 0), (0, pad), (0, 0)))
            pw.append(jnp.concatenate([p, p, p, p], axis=-1))
        tr, ti = fb_re.transpose(0, 2, 1), fb_im.transpose(0, 2, 1)
        fbt.append(jnp.concatenate([tr, ti, ti, tr], axis=-1))
        fbt.append(jnp.concatenate([-ti, tr, tr, -ti], axis=-1))
        ct.append(jnp.concatenate([cr, -ci], axis=-1))
        ct.append(jnp.concatenate([-ci, -cr], axis=-1))
        a_re, a_im = p_re[L], p_im[L]
        consts.append(jnp.concatenate([a_re, a_re, a_re, a_re], axis=-1))
        consts.append(jnp.concatenate([-a_im, a_im, a_im, -a_im], axis=-1))
    kf, kb = kern
    zero = kf[:, :, 0] + kb[:, :, 0] + jnp.eye(C, dtype=F32)[None] * d_skip.astype(F32)[:, None, :]
    kfull = jnp.concatenate([kb[:, :, :0:-1], zero[:, :, None], kf[:, :, 1:], jnp.zeros_like(zero)[:, :, None]],
                            axis=2)
    flat = kfull.astype(BF16).reshape(G, C, 2 * L * C)
    return (flat, jnp.stack(pw, axis=1), jnp.stack(fbt, axis=1), jnp.stack(ct, axis=1),
            jnp.stack(consts + consts, axis=1))


def _ssm_kernel(u_ref, flat_ref, pw_ref, fbt_ref, ct_ref, cst_ref, z_ref,
                toe_ref, wst_ref, vt_ref, f_ref, s_ref, *, bsz):
    u = u_ref[0]
    nc = u.shape[0]
    sw = cst_ref.shape[-1]
    hw = sw // 2
    C = fbt_ref.shape[2]
    L = toe_ref.shape[0] // C
    lc = L * C

    per_tile = LANES // C
    span = (2 * L - per_tile) * C
    for b in range(per_tile):
        shifted = flat_ref[0, :, (per_tile - 1 - b) * C:(per_tile - 1 - b) * C + span]
        for a in range(L // per_tile):
            t_in = a * per_tile + b
            off = (L // per_tile - 1 - a) * LANES
            toe_ref[t_in * C:(t_in + 1) * C, :] = shifted[:, off:off + lc]

    fa_f, fb_f, fa_b, fb_b = fbt_ref[0, 0], fbt_ref[0, 1], fbt_ref[0, 2], fbt_ref[0, 3]
    ca_f, cb_f, ca_b, cb_b = ct_ref[0, 0], ct_ref[0, 1], ct_ref[0, 2], ct_ref[0, 3]

    def power(k, n):
        return pw_ref[0, k, n:n + 1, :]

    for t in range(L):
        rows = slice(t * C, (t + 1) * C)
        wst_ref[rows, 0:sw] = (fa_f * power(0, L - 1 - t) + fb_f * power(1, L - 1 - t)).astype(BF16)
        wst_ref[rows, sw:2 * sw] = (fa_b * power(2, t) + fb_b * power(3, t)).astype(BF16)
        vt_ref[rows, 0:hw] = (ca_f * power(0, t + 1)[:, :hw] + cb_f * power(1, t + 1)[:, :hw]).astype(BF16)
        vt_ref[rows, hw:sw] = (ca_b * power(2, L - t)[:, :hw] + cb_b * power(3, L - t)[:, :hw]).astype(BF16)

    f_ref[...] = jnp.dot(u, wst_ref[...], preferred_element_type=F32)
    cst = cst_ref[0]
    a1f, a2f, a1b, a2b = cst[0:1], cst[1:2], cst[2:3], cst[3:4]
    per = nc // bsz

    def swap(v):
        return jnp.concatenate([v[:, hw:], v[:, :hw]], axis=-1)

    rows = 8

    def step(i, carry):
        new = []
        for b in range(bsz):
            sf, sb = carry[2 * b], carry[2 * b + 1]
            base_f = pl.multiple_of(b * per + i * rows, rows)
            base_b = pl.multiple_of(b * per + per - rows - i * rows, rows)
            f_blk = f_ref[pl.ds(base_f, rows), 0:sw]
            b_blk = f_ref[pl.ds(base_b, rows), sw:2 * sw]
            f_rows, b_rows = [], []
            for r in range(rows):
                f_rows.append(sf[:, :hw])
                sf = a1f * sf + a2f * swap(sf) + f_blk[r:r + 1]
                b_rows.append(sb[:, :hw])
                sb = a1b * sb + a2b * swap(sb) + b_blk[rows - 1 - r:rows - r]
            s_ref[pl.ds(base_f, rows), 0:hw] = jnp.concatenate(f_rows, axis=0)
            s_ref[pl.ds(base_b, rows), hw:sw] = jnp.concatenate(b_rows[::-1], axis=0)
            new += [sf, sb]
        return tuple(new)

    lax.fori_loop(0, per // rows, step, tuple(jnp.zeros((1, sw), F32) for _ in range(2 * bsz)))
    y = jnp.dot(u, toe_ref[...], preferred_element_type=F32)
    y = y + lax.dot_general(s_ref[...].astype(BF16), vt_ref[...], (((1,), (1,)), ((), ())),
                            preferred_element_type=F32)
    z_ref[0] = jax.nn.gelu(y).astype(BF16)


def _ssm(u, bsz, lam_re, lam_im, log_dt, b_re, b_im, c_re, c_im, d_skip):
    t, width = u.shape
    L, C = SSM_CHUNK, d_skip.shape[-1]
    G = width // C
    nc = t // L
    flat, pw, fbt, ct, consts = _ssm_operators(lam_re, lam_im, log_dt, b_re, b_im, c_re, c_im, d_skip)
    sw = consts.shape[-1]
    ug = u.reshape(nc, L, G, C).transpose(2, 0, 1, 3).reshape(G, nc, L * C)

    def group_spec(a):
        return pl.BlockSpec((1,) + a.shape[1:], lambda g: (g,) + (0,) * (a.ndim - 1))

    z = pl.pallas_call(
        functools.partial(_ssm_kernel, bsz=bsz),
        grid=(G,),
        in_specs=[group_spec(a) for a in (ug, flat, pw, fbt, ct, consts)],
        out_specs=pl.BlockSpec((1, nc, L * C), lambda g: (g, 0, 0)),
        out_shape=jax.ShapeDtypeStruct((G, nc, L * C), BF16),
        scratch_shapes=[pltpu.VMEM((L * C, L * C), BF16), pltpu.VMEM((L * C, 2 * sw), BF16),
                        pltpu.VMEM((L * C, sw), BF16), pltpu.VMEM((nc, 2 * sw), F32), pltpu.VMEM((nc, sw), F32)],
        compiler_params=_params("parallel"),
    )(ug, flat, pw, fbt, ct, consts)
    return z.reshape(G, nc, L, C).transpose(1, 2, 0, 3).reshape(t, width)


def _post_kernel(o0_ref, l0_ref, o1_ref, l1_ref, o2_ref, l2_ref, z_ref, gate_ref, x_ref,
                 wa_ref, wglu_ref, bglu_ref, ws_ref, wo_ref, g2_ref, wr_ref, wrlo_ref, br_ref, tri_ref,
                 x1_ref, hp_ref, idx_ref, rank_ref, wgt_ref, cnt_ref, o_scr, l_scr, carry_ref):
    tm, d = x_ref.shape

    @pl.when(pl.program_id(0) == 0)
    def _():
        carry_ref[...] = jnp.zeros_like(carry_ref)

    outs = [o0_ref[...].astype(F32)]
    lses = [l0_ref[...]]
    for (o_ref, l_ref), (_, dil) in zip(((o1_ref, l1_ref), (o2_ref, l2_ref)), ATTN_GROUPS[1:]):
        nlb = o_scr.shape[0]
        for r in range(dil):
            for c in range(nlb):
                o_scr[c, pl.ds(r, tm // dil, stride=dil), :] = o_ref[r, :, c * LANES:(c + 1) * LANES].astype(F32)
                l_scr[c, pl.ds(r, tm // dil, stride=dil), :] = l_ref[r, :, c * LANES:(c + 1) * LANES]
        outs.append(jnp.concatenate([o_scr[c] for c in range(nlb)], axis=-1))
        lses.append(jnp.concatenate([l_scr[c] for c in range(nlb)], axis=-1))
    top = jnp.maximum(jnp.maximum(lses[0], lses[1]), lses[2])
    es = [jnp.exp(l - top) for l in lses]
    y_attn = (es[0] * outs[0] + es[1] * outs[1] + es[2] * outs[2]) / (es[0] + es[1] + es[2])
    ya = jnp.dot(y_attn.astype(BF16), wa_ref[...], preferred_element_type=F32)
    z = z_ref[...]
    glu = jnp.dot(z, wglu_ref[...], preferred_element_type=F32) + bglu_ref[...]
    zz = z.astype(F32) * jax.nn.sigmoid(glu)
    ys = jnp.dot(zz.astype(BF16), ws_ref[...], preferred_element_type=F32)
    mixed = gate_ref[:, :d].astype(F32) * ya + gate_ref[:, d:].astype(F32) * ys
    x1 = x_ref[...] + jnp.dot(mixed.astype(BF16), wo_ref[...], preferred_element_type=F32)
    x1_ref[...] = x1
    ms = jnp.mean(x1 * x1, axis=-1, keepdims=True)
    h2 = x1 * lax.rsqrt(ms + NORM_EPS) * g2_ref[...]
    hb = h2.astype(BF16)
    hp_ref[...] = h2

    h_lo = (h2 - hb.astype(F32)).astype(BF16)
    nt = (((1,), (1,)), ((), ()))
    logits = (lax.dot_general(wr_ref[...], hb, nt, preferred_element_type=F32)
              + lax.dot_general(wrlo_ref[...], hb, nt, preferred_element_type=F32)
              + lax.dot_general(wr_ref[...], h_lo, nt, preferred_element_type=F32) + br_ref[...])
    ne = logits.shape[0]
    e_iota = lax.broadcasted_iota(jnp.int32, logits.shape, 0)
    vals, idxs = [], []
    work = logits
    for _ in range(TOP_K):
        m = jnp.max(work, axis=0, keepdims=True)
        i = jnp.min(jnp.where(work == m, e_iota, ne), axis=0, keepdims=True)
        vals.append(m)
        idxs.append(i)
        work = jnp.where(e_iota == i, -jnp.inf, work)
    ex = [jnp.exp(v - vals[0]) for v in vals]
    tot = ex[0] + ex[1] + ex[2] + ex[3]
    onehot = jnp.zeros(logits.shape, F32)
    for i in idxs:
        onehot = onehot + (e_iota == i).astype(F32)
    before = jnp.dot(onehot.astype(BF16), tri_ref[...], preferred_element_type=F32) + carry_ref[:, 0:1]
    for k in range(TOP_K):
        idx_ref[k:k + 1, :] = idxs[k]
        rank_ref[k:k + 1, :] = jnp.sum(jnp.where(e_iota == idxs[k], before, 0.0), axis=0,
                                       keepdims=True).astype(jnp.int32)
        wgt_ref[k:k + 1, :] = ex[k] / tot
    carry_ref[...] = carry_ref[...] + jnp.sum(onehot, axis=1, keepdims=True)
    cnt_ref[...] = carry_ref[...].astype(jnp.int32)


def _post_mix(x2, seq, attn, z, gates, w_attn_branch, w_glu, b_glu, w_ssm_branch, w_out, norm2, w_router, b_router):
    t, d = x2.shape
    tm = ROW_TILE
    gw = GROUP_WIDTH
    tiles_per_seq = seq // tm
    ne = w_router.shape[1]
    ssm_w = z.shape[1]
    in_specs, args = [], []
    for (o, lse), (_, dil) in zip(attn, ATTN_GROUPS):
        if dil == 1:
            spec = pl.BlockSpec((tm, gw), lambda i: (i, 0))
            o, lse = o.reshape(t, gw), lse.reshape(t, gw)
        else:
            spec = pl.BlockSpec((dil, tm // dil, gw), lambda i: (i // tiles_per_seq, i % tiles_per_seq, 0))
        in_specs += [spec, spec]
        args += [o, lse]
    tri = jnp.asarray(np.arange(tm)[:, None] < np.arange(tm)[None, :], BF16)
    in_specs += [pl.BlockSpec((tm, ssm_w), lambda i: (i, 0)),
                 pl.BlockSpec((tm, 2 * d), lambda i: (i, 0)),
                 pl.BlockSpec((tm, d), lambda i: (i, 0)),
                 _const_spec((gw, d)), _const_spec((ssm_w, ssm_w)), _const_spec((1, ssm_w)),
                 _const_spec((ssm_w, d)), _const_spec((d, d)), _const_spec((1, d)),
                 _const_spec((ne, d)), _const_spec((ne, d)), _const_spec((ne, 1)), _const_spec((tm, tm))]
    wr_t = w_router.T.astype(F32)
    wr_hi = wr_t.astype(BF16)
    wr_lo = (wr_t - wr_hi.astype(F32)).astype(BF16)
    args += [z, gates, x2, w_attn_branch.astype(BF16), w_glu.astype(BF16), b_glu.reshape(1, ssm_w),
             w_ssm_branch.astype(BF16), w_out.astype(BF16), norm2.reshape(1, d),
             wr_hi, wr_lo, b_router.reshape(ne, 1), tri]
    tok_spec = pl.BlockSpec((TOP_K, tm), lambda i: (0, i))
    return pl.pallas_call(
        _post_kernel,
        grid=(t // tm,),
        in_specs=in_specs,
        out_specs=[pl.BlockSpec((tm, d), lambda i: (i, 0)),
                   pl.BlockSpec((tm, d), lambda i: (i, 0)),
                   tok_spec, tok_spec, tok_spec, pl.BlockSpec((ne, LANES), lambda i: (0, 0))],
        out_shape=[jax.ShapeDtypeStruct((t, d), F32), jax.ShapeDtypeStruct((t, d), F32),
                   jax.ShapeDtypeStruct((TOP_K, t), jnp.int32), jax.ShapeDtypeStruct((TOP_K, t), jnp.int32),
                   jax.ShapeDtypeStruct((TOP_K, t), F32), jax.ShapeDtypeStruct((ne, LANES), jnp.int32)],
        scratch_shapes=[pltpu.VMEM((gw // LANES, tm, LANES), F32), pltpu.VMEM((gw // LANES, tm, LANES), F32),
                        pltpu.VMEM((ne, LANES), F32)],
        compiler_params=_params("arbitrary"),
    )(*args)


def _row_copy(src_ref, src_row, dst_ref, dst_row, sem):
    return pltpu.make_async_copy(src_ref.at[pl.ds(src_row, 1)], dst_ref.at[pl.ds(dst_row, 1)], sem)


def _rows_wait(ref, n_rows, sem):
    pltpu.make_async_copy(ref.at[pl.ds(0, n_rows)], ref.at[pl.ds(0, n_rows)], sem).wait()


def _dispatch_kernel(be_ref, nu_ref, dest_ref, h_ref, xs_ref, zero_ref, sems):
    i = pl.program_id(0)
    tm = dest_ref.shape[1]
    tb = zero_ref.shape[0]
    n_blocks = be_ref.shape[0]

    @pl.when(i == 0)
    def _():
        zero_ref[...] = jnp.zeros_like(zero_ref)

        def holds_padding(b):
            return (b >= nu_ref[0] - 1) | (be_ref[b] != be_ref[jnp.minimum(b + 1, n_blocks - 1)])

        def zero_copy(b):
            return pltpu.make_async_copy(zero_ref, xs_ref.at[pl.ds(pl.multiple_of(b * tb, tb), tb)], sems.at[1])

        def start(b, c):
            @pl.when(holds_padding(b))
            def _():
                zero_copy(b).start()
            return c

        def wait(b, c):
            @pl.when(holds_padding(b))
            def _():
                zero_copy(b).wait()
            return c

        lax.fori_loop(0, n_blocks, start, 0)
        lax.fori_loop(0, n_blocks, wait, 0)

    def issue(t, c):
        for k in range(TOP_K):
            _row_copy(h_ref, t, xs_ref, dest_ref[k, t], sems.at[0]).start()
        return c

    lax.fori_loop(0, tm, issue, 0, unroll=4)
    _rows_wait(xs_ref, TOP_K * tm, sems.at[0])


def _dispatch(block_exp, n_used, dest, hp, n_rows):
    t = hp.shape[0]
    slab = hp.shape[1:]
    tm = DISPATCH_TILE
    grid_spec = pltpu.PrefetchScalarGridSpec(
        num_scalar_prefetch=2,
        grid=(t // tm,),
        in_specs=[pl.BlockSpec((TOP_K, tm), lambda i, be, nu: (0, i), memory_space=pltpu.SMEM),
                  pl.BlockSpec((tm,) + slab, lambda i, be, nu: (i,) + (0,) * len(slab))],
        out_specs=pl.BlockSpec(memory_space=pl.ANY),
        scratch_shapes=[pltpu.VMEM((EXPERT_ROWS,) + slab, hp.dtype), pltpu.SemaphoreType.DMA((2,))],
    )
    return pl.pallas_call(
        _dispatch_kernel,
        grid_spec=grid_spec,
        out_shape=jax.ShapeDtypeStruct((n_rows,) + slab, hp.dtype),
        compiler_params=_params("arbitrary"),
    )(block_exp, n_used, dest, hp)


def _expert_kernel(be_ref, nu_ref, xs_ref, wgu_ref, bgu_ref, wd_ref, bd_ref, ys_ref, wgu_bf, wd_bf):
    i = pl.program_id(0)
    active = i < nu_ref[0]

    @pl.when(active & ((i == 0) | (be_ref[i] != be_ref[jnp.maximum(i - 1, 0)])))
    def _():
        wgu_bf[...] = wgu_ref[0].astype(BF16)
        wd_bf[...] = wd_ref[0].astype(BF16)

    @pl.when(active)
    def _():
        gu = jnp.dot(xs_ref[...].astype(BF16), wgu_bf[...], preferred_element_type=F32) + bgu_ref[0]
        de = gu.shape[1] // 2
        gate = jnp.minimum(gu[:, :de], SWIGLU_LIMIT)
        up = jnp.clip(gu[:, de:], -SWIGLU_LIMIT, SWIGLU_LIMIT)
        act = gate * jax.nn.sigmoid(SWIGLU_ALPHA * gate) * (up + 1.0)
        ys_ref[...] = jnp.dot(act.astype(BF16), wd_bf[...], preferred_element_type=F32) + bd_ref[0]

    @pl.when(i >= nu_ref[0])
    def _():
        ys_ref[...] = jnp.zeros_like(ys_ref)


def _experts(block_exp, n_used, xs, w_gate_up, b_gate_up, w_down, b_down):
    n_rows = xs.shape[0]
    ne, d, de2 = w_gate_up.shape
    tb = EXPERT_ROWS

    def row_block(i, be, nu):
        return (jnp.minimum(i, nu[0] - 1), 0)

    def expert_block(i, be, nu):
        return (be[jnp.minimum(i, nu[0] - 1)], 0, 0)

    grid_spec = pltpu.PrefetchScalarGridSpec(
        num_scalar_prefetch=2,
        grid=(n_rows // tb,),
        in_specs=[pl.BlockSpec((tb,) + xs.shape[1:], row_block),
                  pl.BlockSpec((1, d, de2), expert_block),
                  pl.BlockSpec((1, 1, de2), expert_block),
                  pl.BlockSpec((1, de2 // 2, d), expert_block),
                  pl.BlockSpec((1, 1, d), expert_block)],
        out_specs=pl.BlockSpec((tb, d), lambda i, be, nu: (i, 0)),
        scratch_shapes=[pltpu.VMEM((d, de2), BF16), pltpu.VMEM((de2 // 2, d), BF16)],
    )
    return pl.pallas_call(
        _expert_kernel,
        grid_spec=grid_spec,
        out_shape=jax.ShapeDtypeStruct((n_rows, d), F32),
        compiler_params=_params("arbitrary"),
    )(block_exp, n_used, xs, w_gate_up, b_gate_up.reshape(ne, 1, de2), w_down, b_down.reshape(ne, 1, d))


def _combine_kernel(dest_ref, dest_next_ref, wgt_ref, x1_ref, ys_ref, out_ref, buf_ref, sems):
    i = pl.program_id(0)
    n = pl.num_programs(0)
    tm = x1_ref.shape[0]
    slot = i % 2

    def gather(d_ref, into):
        def issue(t, c):
            for k in range(TOP_K):
                _row_copy(ys_ref, d_ref[k, t], buf_ref.at[into, k], t, sems.at[into]).start()
            return c

        lax.fori_loop(0, tm, issue, 0, unroll=4)

    @pl.when(i == 0)
    def _():
        gather(dest_ref, 0)

    @pl.when(i + 1 < n)
    def _():
        gather(dest_next_ref, 1 - slot)

    for k in range(TOP_K):
        _rows_wait(buf_ref.at[slot, k], tm, sems.at[slot])
    acc = x1_ref[...]
    for k in range(TOP_K):
        acc = acc + wgt_ref[:, k:k + 1] * buf_ref[slot, k]
    out_ref[...] = acc


def _combine(dest, wgt_t, x1, ys):
    t, d = x1.shape
    tm = ROUTE_TILE
    n = t // tm
    return pl.pallas_call(
        _combine_kernel,
        grid=(n,),
        in_specs=[pl.BlockSpec((TOP_K, tm), lambda i: (0, i), memory_space=pltpu.SMEM),
                  pl.BlockSpec((TOP_K, tm), lambda i: (0, jnp.minimum(i + 1, n - 1)), memory_space=pltpu.SMEM),
                  pl.BlockSpec((tm, TOP_K), lambda i: (i, 0)),
                  pl.BlockSpec((tm, d), lambda i: (i, 0)),
                  pl.BlockSpec(memory_space=pl.ANY)],
        out_specs=pl.BlockSpec((tm, d), lambda i: (i, 0)),
        out_shape=jax.ShapeDtypeStruct((t, d), F32),
        scratch_shapes=[pltpu.VMEM((2, TOP_K, tm) + ys.shape[1:], F32), pltpu.SemaphoreType.DMA((2,))],
        compiler_params=_params("arbitrary"),
    )(dest, dest, wgt_t, x1, ys)


def _moe(x1, hp, idx, rank, wgt, counts, w_gate_up, b_gate_up, w_down, b_down):
    t = x1.shape[0]
    tb = EXPERT_ROWS
    ne = w_gate_up.shape[0]
    n_blocks = -(-(t * TOP_K) // tb) + ne
    padded = ((counts + tb - 1) // tb) * tb
    pad_end = jnp.cumsum(padded)
    pad_start = pad_end - padded
    dest = rank + jnp.sum(jnp.where(idx[None] == jnp.arange(ne, dtype=jnp.int32)[:, None, None],
                                    pad_start.astype(jnp.int32)[:, None, None], 0), axis=0)
    block_start = jnp.arange(n_blocks, dtype=jnp.int32) * tb
    block_exp = jnp.minimum(jnp.sum((pad_end[None, :] <= block_start[:, None]).astype(jnp.int32), axis=1), ne - 1)
    n_used = (pad_end[-1:] // tb).astype(jnp.int32)
    xs = _dispatch(block_exp, n_used, dest, hp, n_blocks * tb)
    ys = _experts(block_exp, n_used, xs, w_gate_up, b_gate_up, w_down, b_down)
    return _combine(dest, wgt.T, x1, ys)


def kernel(x, norm1, w_in, q_norm, k_norm, lam_re, lam_im, log_dt, b_re, b_im, c_re, c_im, d_skip, w_glu, b_glu, w_attn_branch, w_ssm_branch, w_out, norm2, w_router, b_router, w_gate_up, b_gate_up, w_down, b_down):
    bsz, seq, d = x.shape
    t = bsz * seq
    x2 = x.reshape(t, d)
    for l in range(norm1.shape[0]):
        qkv0, qkv1, qkv2, u, gates = _in_proj(x2, norm1[l], w_in[l], q_norm[l], k_norm[l], seq)
        attn = [_attention(qkv0.reshape(bsz, seq, -1)), _attention(qkv1), _attention(qkv2)]
        z = _ssm(u, bsz, lam_re[l], lam_im[l], log_dt[l], b_re[l], b_im[l], c_re[l], c_im[l], d_skip[l])
        x1, hp, idx, rank, wgt, counts = _post_mix(
            x2, seq, attn, z, gates, w_attn_branch[l], w_glu[l], b_glu[l], w_ssm_branch[l], w_out[l], norm2[l],
            w_router[l], b_router[l])
        x2 = _moe(x1, hp, idx, rank, wgt, counts[:, 0], w_gate_up[l], b_gate_up[l], w_down[l], b_down[l])
    return x2.reshape(bsz, seq, d)
```

```python
import functools
import math

import numpy as np
import jax
import jax.numpy as jnp
from jax import lax
from jax.experimental import pallas as pl
from jax.experimental.pallas import tpu as pltpu

F32 = jnp.float32
BF16 = jnp.bfloat16

HEAD_DIM = 64
HEADS_PER_GROUP = 4
ATTN_GROUPS = ((128, 1), (512, 4), (2048, 16))
GROUP_WIDTH = HEADS_PER_GROUP * HEAD_DIM
ATTN_WIDTH = GROUP_WIDTH * len(ATTN_GROUPS)
BAND = 64
ROPE_THETA = 10000.0
SSM_GROUP = 16
SSM_STATE = 64
N_EXPERTS = 32
TOP_K = 4
SWIGLU_ALPHA = 1.702
SWIGLU_LIMIT = 7.0
NORM_EPS = 1e-6
MASK_VALUE = -1e30

ROW_TILE = 512
ATTN_Q_TILE = 1024
ATTN_Q_SUB = 128
SSM_CHUNK = 64
SSM_POWER_ROWS = -(-(SSM_CHUNK + 1) // 8) * 8
DISPATCH_TILE = 1024
ROUTE_TILE = 512
EXPERT_ROWS = 512
VMEM_LIMIT = 56 * 1024 * 1024
LANES = 128


def _params(*sem):
    return pltpu.CompilerParams(dimension_semantics=sem, vmem_limit_bytes=VMEM_LIMIT)


def _const_spec(shape):
    return pl.BlockSpec(shape, lambda *_: (0,) * len(shape), pipeline_mode=pl.Buffered(1))


def _permute_qk(a):
    lead = a.shape[:-1]
    n = len(lead)
    a = a.reshape(*lead, len(ATTN_GROUPS), HEADS_PER_GROUP, 2, HEAD_DIM // 2)
    return a.transpose(*range(n), n, n + 2, n + 1, n + 3).reshape(*lead, ATTN_WIDTH)


def _permute_group_lanes(piece):
    hd, hh = HEAD_DIM, HEAD_DIM // 2
    firsts = [piece[:, h * hd:h * hd + hh] for h in range(HEADS_PER_GROUP)]
    seconds = [piece[:, h * hd + hh:(h + 1) * hd] for h in range(HEADS_PER_GROUP)]
    return jnp.concatenate(firsts + seconds, axis=-1)


def _in_proj_kernel(x_ref, g1_ref, wf_ref, gq_ref, gk_ref, cos_ref, sin_ref, seg_ref,
                    qkv0_ref, qkv1_ref, qkv2_ref, u_ref, gate_ref, w_ref, scr_ref):
    @pl.when(pl.program_id(0) == 0)
    def _():
        rows_per_step = 128

        def cast_rows(i, _):
            rows = pl.ds(pl.multiple_of(i * rows_per_step, rows_per_step), rows_per_step)
            for c0 in range(0, 2 * ATTN_WIDTH, GROUP_WIDTH):
                w_ref[rows, c0:c0 + GROUP_WIDTH] = _permute_group_lanes(wf_ref[rows, c0:c0 + GROUP_WIDTH]).astype(BF16)
            for c0 in range(2 * ATTN_WIDTH, wf_ref.shape[1], GROUP_WIDTH):
                w_ref[rows, c0:c0 + GROUP_WIDTH] = wf_ref[rows, c0:c0 + GROUP_WIDTH].astype(BF16)
            return 0

        lax.fori_loop(0, wf_ref.shape[0] // rows_per_step, cast_rows, 0)

    x = x_ref[...]
    ms = jnp.mean(x * x, axis=-1, keepdims=True)
    h = (x * lax.rsqrt(ms + NORM_EPS) * g1_ref[...]).astype(BF16)
    cos = cos_ref[...]
    sin = sin_ref[...]
    seg = seg_ref[...]
    half = GROUP_WIDTH // 2
    tm = x.shape[0]

    def proj(c0, width):
        return jnp.dot(h, w_ref[:, c0:c0 + width], preferred_element_type=F32)

    def sum_squares(t):
        return t[:, :half] * t[:, :half] + t[:, half:] * t[:, half:]

    def norm_rope(t, tot, gain_ref, g):
        t1, t2 = t[:, :half], t[:, half:]
        r = lax.rsqrt(tot * (1.0 / HEAD_DIM) + NORM_EPS)
        a = t1 * r * gain_ref[:, g * GROUP_WIDTH:g * GROUP_WIDTH + half]
        b = t2 * r * gain_ref[:, g * GROUP_WIDTH + half:(g + 1) * GROUP_WIDTH]
        return a * cos - b * sin, b * cos + a * sin

    outs = (qkv0_ref, qkv1_ref, qkv2_ref)
    for g, (_, dil) in enumerate(ATTN_GROUPS):
        tq = proj(g * GROUP_WIDTH, GROUP_WIDTH)
        tk = proj(ATTN_WIDTH + g * GROUP_WIDTH, GROUP_WIDTH)
        ss = jnp.concatenate([sum_squares(tq), sum_squares(tk)], axis=-1).astype(BF16)
        tot = jnp.dot(ss, seg, preferred_element_type=F32)
        q1, q2 = norm_rope(tq, tot[:, :half], gq_ref, g)
        k1, k2 = norm_rope(tk, tot[:, half:], gk_ref, g)
        v = proj(2 * ATTN_WIDTH + g * GROUP_WIDTH, GROUP_WIDTH)
        qkv = jnp.concatenate([q1, q2, k1, k2, v], axis=-1)
        if dil == 1:
            outs[g][...] = qkv.astype(BF16)
        else:
            for c in range(qkv.shape[1] // LANES):
                scr_ref[c] = qkv[:, c * LANES:(c + 1) * LANES]
            for r in range(dil):
                for c in range(qkv.shape[1] // LANES):
                    outs[g][r, :, c * LANES:(c + 1) * LANES] = (
                        scr_ref[c, pl.ds(r, tm // dil, stride=dil), :].astype(BF16))
    o_u = 3 * ATTN_WIDTH
    u_w = u_ref.shape[-1]
    u_ref[...] = proj(o_u, u_w).astype(BF16)
    gate_w = gate_ref.shape[-1]
    for c in range(0, gate_w, 1024):
        gate_ref[:, c:c + 1024] = jax.nn.sigmoid(proj(o_u + u_w + c, 1024)).astype(BF16)


def _in_proj(x2, norm1, w_in, q_norm, k_norm, seq):
    t, d = x2.shape
    tm = ROW_TILE
    in_w = w_in.shape[1]
    ssm_w = d // 2
    gate_w = in_w - 3 * ATTN_WIDTH - ssm_w
    gq = (_permute_qk(q_norm.reshape(-1)) * (HEAD_DIM ** -0.5)).reshape(1, ATTN_WIDTH)
    gk = _permute_qk(k_norm.reshape(-1)).reshape(1, ATTN_WIDTH)
    inv_freq = ROPE_THETA ** (-jnp.arange(0, HEAD_DIM, 2, dtype=F32) / HEAD_DIM)
    ang = jnp.arange(seq, dtype=F32)[:, None] * inv_freq[None, :]
    cos = jnp.tile(jnp.cos(ang), (1, HEADS_PER_GROUP))
    sin = jnp.tile(jnp.sin(ang), (1, HEADS_PER_GROUP))
    lane = np.arange(GROUP_WIDTH)
    seg = jnp.asarray(lane[:, None] // (HEAD_DIM // 2) == lane[None, :] // (HEAD_DIM // 2), BF16)
    bsz = t // seq
    tiles_per_seq = seq // tm
    qkv_w = 3 * GROUP_WIDTH
    out_shape = [jax.ShapeDtypeStruct((t, qkv_w), BF16)]
    out_specs = [pl.BlockSpec((tm, qkv_w), lambda i: (i, 0))]
    for _, dil in ATTN_GROUPS[1:]:
        out_shape.append(jax.ShapeDtypeStruct((bsz * dil, seq // dil, qkv_w), BF16))
        out_specs.append(pl.BlockSpec((dil, tm // dil, qkv_w),
                                      lambda i: (i // tiles_per_seq, i % tiles_per_seq, 0)))
    out_shape += [jax.ShapeDtypeStruct((t, ssm_w), BF16), jax.ShapeDtypeStruct((t, gate_w), BF16)]
    out_specs += [pl.BlockSpec((tm, ssm_w), lambda i: (i, 0)), pl.BlockSpec((tm, gate_w), lambda i: (i, 0))]
    return pl.pallas_call(
        _in_proj_kernel,
        grid=(t // tm,),
        in_specs=[
            pl.BlockSpec((tm, d), lambda i: (i, 0)),
            _const_spec((1, d)),
            _const_spec((d, in_w)),
            _const_spec((1, ATTN_WIDTH)),
            _const_spec((1, ATTN_WIDTH)),
            pl.BlockSpec((tm, GROUP_WIDTH // 2), lambda i: (i % tiles_per_seq, 0)),
            pl.BlockSpec((tm, GROUP_WIDTH // 2), lambda i: (i % tiles_per_seq, 0)),
            _const_spec((GROUP_WIDTH, GROUP_WIDTH)),
        ],
        out_specs=out_specs,
        out_shape=out_shape,
        scratch_shapes=[pltpu.VMEM((d, in_w), BF16), pltpu.VMEM((qkv_w // LANES, tm, LANES), F32)],
        compiler_params=_params("arbitrary"),
    )(x2, norm1.reshape(1, d), w_in, gq, gk, cos, sin, seg)


def _attn_kernel(c_ref, kl_ref, vl_ref, kr_ref, vr_ref, o_ref, lse_ref, kcat_ref, vcat_ref, *, length):
    tq = c_ref.shape[1]
    gw = GROUP_WIDTH
    j = pl.program_id(1)
    kcat_ref[0:BAND] = kl_ref[0]
    kcat_ref[BAND:BAND + tq] = c_ref[0, :, gw:2 * gw]
    kcat_ref[BAND + tq:] = kr_ref[0]
    vcat_ref[0:BAND] = vl_ref[0]
    vcat_ref[BAND:BAND + tq] = c_ref[0, :, 2 * gw:]
    vcat_ref[BAND + tq:] = vr_ref[0]
    sub = min(ATTN_Q_SUB, tq)
    nk = sub + 2 * BAND
    nh = HEADS_PER_GROUP
    row = lax.broadcasted_iota(jnp.int32, (nh * sub, nk), 0) & (sub - 1)
    col = lax.broadcasted_iota(jnp.int32, (nh * sub, nk), 1)
    in_band = jnp.abs(col - BAND - row) <= BAND
    q_lane = lax.broadcasted_iota(jnp.int32, (1, gw), 1)
    q_head = (q_lane % (gw // 2)) // (HEAD_DIM // 2)
    v_head = q_lane // HEAD_DIM
    for s in range(tq // sub):
        q_s = c_ref[0, s * sub:(s + 1) * sub, 0:gw]
        k_s = kcat_ref[s * sub:s * sub + nk]
        v_s = vcat_ref[s * sub:s * sub + nk]
        kpos = j * tq + s * sub - BAND + col
        valid = in_band & (kpos >= 0) & (kpos < length)
        q4 = jnp.concatenate([jnp.where(q_head == hh, q_s, jnp.zeros_like(q_s)) for hh in range(nh)], axis=0)
        sc = lax.dot_general(q4, k_s, (((1,), (1,)), ((), ())), preferred_element_type=F32)
        sc = jnp.where(valid, sc, MASK_VALUE)
        m = jnp.max(sc, axis=-1, keepdims=True)
        p = jnp.exp(sc - m)
        den = jnp.sum(p, axis=-1, keepdims=True)
        o4 = jnp.dot(p.astype(BF16), v_s, preferred_element_type=F32) * (1.0 / den)
        l4 = m + jnp.log(den)
        out = o4[0:sub]
        lse = jnp.broadcast_to(l4[0:sub], (sub, gw))
        for hh in range(1, nh):
            out = jnp.where(v_head == hh, o4[hh * sub:(hh + 1) * sub], out)
            lse = jnp.where(v_head == hh, l4[hh * sub:(hh + 1) * sub], lse)
        o_ref[0, s * sub:(s + 1) * sub] = out.astype(BF16)
        lse_ref[0, s * sub:(s + 1) * sub] = lse


def _attention(qkv):
    n, length, _ = qkv.shape
    tq = min(ATTN_Q_TILE, length)
    gw = GROUP_WIDTH
    hb = tq // BAND
    last = length // BAND - 1

    def left(lane_block):
        return pl.BlockSpec((1, BAND, gw), lambda b, j: (b, jnp.maximum(j * hb - 1, 0), lane_block))

    def right(lane_block):
        return pl.BlockSpec((1, BAND, gw), lambda b, j: (b, jnp.minimum((j + 1) * hb, last), lane_block))

    return pl.pallas_call(
        functools.partial(_attn_kernel, length=length),
        grid=(n, length // tq),
        in_specs=[pl.BlockSpec((1, tq, 3 * gw), lambda b, j: (b, j, 0)), left(1), left(2), right(1), right(2)],
        out_specs=[pl.BlockSpec((1, tq, gw), lambda b, j: (b, j, 0)),
                   pl.BlockSpec((1, tq, gw), lambda b, j: (b, j, 0))],
        out_shape=[jax.ShapeDtypeStruct((n, length, gw), BF16), jax.ShapeDtypeStruct((n, length, gw), F32)],
        scratch_shapes=[pltpu.VMEM((tq + 2 * BAND, gw), BF16), pltpu.VMEM((tq + 2 * BAND, gw), BF16)],
        compiler_params=_params("parallel", "parallel"),
    )(qkv, qkv, qkv, qkv, qkv)


def _ssm_operators(lam_re, lam_im, log_dt, b_re, b_im, c_re, c_im, d_skip):
    L = SSM_CHUNK
    G, P = lam_re.shape[1:]
    C = d_skip.shape[-1]
    hp = lax.Precision.HIGHEST
    n = jnp.arange(L + 1, dtype=F32)[:, None, None]
    pad = SSM_POWER_ROWS - (L + 1)
    kern, pw, fbt, ct, consts = [], [], [], [], []
    for direction in range(2):
        lr, li = lam_re[direction].astype(F32), lam_im[direction].astype(F32)
        dt = jnp.exp(log_dt[direction].astype(F32))[:, None]
        mag = jnp.exp(lr * dt)
        ab_re, ab_im = mag * jnp.cos(li * dt), mag * jnp.sin(li * dt)
        den = lr * lr + li * li
        nr = ab_re - 1.0
        f_re = (nr * lr + ab_im * li) / den
        f_im = (ab_im * lr - nr * li) / den
        pmag = jnp.exp(n * (lr * dt)[None])
        p_re, p_im = pmag * jnp.cos(n * (li * dt)[None]), pmag * jnp.sin(n * (li * dt)[None])
        br, bi = b_re[direction].astype(F32), b_im[direction].astype(F32)
        fb_re = f_re[..., None] * br - f_im[..., None] * bi
        fb_im = f_re[..., None] * bi + f_im[..., None] * br
        cr, ci = c_re[direction].astype(F32), c_im[direction].astype(F32)
        pr, pi = p_re[:L].transpose(1, 2, 0)[..., None], p_im[:L].transpose(1, 2, 0)[..., None]
        crt, cit = cr.transpose(0, 2, 1)[:, :, None, :], ci.transpose(0, 2, 1)[:, :, None, :]
        e_re = (pr * crt - pi * cit).reshape(G, P, L * C)
        e_im = (pr * cit + pi * crt).reshape(G, P, L * C)
        kern.append((jnp.einsum('gpc,gpx->gcx', fb_re, e_re, precision=hp)
                     - jnp.einsum('gpc,gpx->gcx', fb_im, e_im, precision=hp)).reshape(G, C, L, C))
        for p in (p_re, p_im):
            p = jnp.pad(p.transpose(1, 0, 2), ((0, 0), (0, pad), (0, 0)))
            pw.append(jnp.concatenate([p, p, p, p], axis=-1))
        tr, ti = fb_re.transpose(0, 2, 1), fb_im.transpose(0, 2, 1)
        fbt.append(jnp.concatenate([tr, ti, ti, tr], axis=-1))
        fbt.append(jnp.concatenate([-ti, tr, tr, -ti], axis=-1))
        ct.append(jnp.concatenate([cr, -ci], axis=-1))
        ct.append(jnp.concatenate([-ci, -cr], axis=-1))
        a_re, a_im = p_re[L], p_im[L]
        consts.append(jnp.concatenate([a_re, a_re, a_re, a_re], axis=-1))
        consts.append(jnp.concatenate([-a_im, a_im, a_im, -a_im], axis=-1))
    kf, kb = kern
    zero = kf[:, :, 0] + kb[:, :, 0] + jnp.eye(C, dtype=F32)[None] * d_skip.astype(F32)[:, None, :]
    kfull = jnp.concatenate([kb[:, :, :0:-1], zero[:, :, None], kf[:, :, 1:], jnp.zeros_like(zero)[:, :, None]],
                            axis=2)
    flat = kfull.astype(BF16).reshape(G, C, 2 * L * C)
    return (flat, jnp.stack(pw, axis=1), jnp.stack(fbt, axis=1), jnp.stack(ct, axis=1),
            jnp.stack(consts + consts, axis=1))


def _ssm_kernel(u_ref, flat_ref, pw_ref, fbt_ref, ct_ref, cst_ref, z_ref,
                toe_ref, wst_ref, vt_ref, f_ref, s_ref, *, bsz):
    u = u_ref[0]
    nc = u.shape[0]
    sw = cst_ref.shape[-1]
    hw = sw // 2
    C = fbt_ref.shape[2]
    L = toe_ref.shape[0] // C
    lc = L * C

    per_tile = LANES // C
    span = (2 * L - per_tile) * C
    for b in range(per_tile):
        shifted = flat_ref[0, :, (per_tile - 1 - b) * C:(per_tile - 1 - b) * C + span]
        for a in range(L // per_tile):
            t_in = a * per_tile + b
            off = (L // per_tile - 1 - a) * LANES
            toe_ref[t_in * C:(t_in + 1) * C, :] = shifted[:, off:off + lc]

    fa_f, fb_f, fa_b, fb_b = fbt_ref[0, 0], fbt_ref[0, 1], fbt_ref[0, 2], fbt_ref[0, 3]
    ca_f, cb_f, ca_b, cb_b = ct_ref[0, 0], ct_ref[0, 1], ct_ref[0, 2], ct_ref[0, 3]

    def power(k, n):
        return pw_ref[0, k, n:n + 1, :]

    for t in range(L):
        rows = slice(t * C, (t + 1) * C)
        wst_ref[rows, 0:sw] = (fa_f * power(0, L - 1 - t) + fb_f * power(1, L - 1 - t)).astype(BF16)
        wst_ref[rows, sw:2 * sw] = (fa_b * power(2, t) + fb_b * power(3, t)).astype(BF16)
        vt_ref[rows, 0:hw] = (ca_f * power(0, t + 1)[:, :hw] + cb_f * power(1, t + 1)[:, :hw]).astype(BF16)
        vt_ref[rows, hw:sw] = (ca_b * power(2, L - t)[:, :hw] + cb_b * power(3, L - t)[:, :hw]).astype(BF16)

    f_ref[...] = jnp.dot(u, wst_ref[...], preferred_element_type=F32)
    cst = cst_ref[0]
    a1f, a2f, a1b, a2b = cst[0:1], cst[1:2], cst[2:3], cst[3:4]
    per = nc // bsz

    def swap(v):
        return jnp.concatenate([v[:, hw:], v[:, :hw]], axis=-1)

    rows = 8

    def step(i, carry):
        new = []
        for b in range(bsz):
            sf, sb = carry[2 * b], carry[2 * b + 1]
            base_f = pl.multiple_of(b * per + i * rows, rows)
            base_b = pl.multiple_of(b * per + per - rows - i * rows, rows)
            f_blk = f_ref[pl.ds(base_f, rows), 0:sw]
            b_blk = f_ref[pl.ds(base_b, rows), sw:2 * sw]
            f_rows, b_rows = [], []
            for r in range(rows):
                f_rows.append(sf[:, :hw])
                sf = a1f * sf + a2f * swap(sf) + f_blk[r:r + 1]
                b_rows.append(sb[:, :hw])
                sb = a1b * sb + a2b * swap(sb) + b_blk[rows - 1 - r:rows - r]
            s_ref[pl.ds(base_f, rows), 0:hw] = jnp.concatenate(f_rows, axis=0)
            s_ref[pl.ds(base_b, rows), hw:sw] = jnp.concatenate(b_rows[::-1], axis=0)
            new += [sf, sb]
        return tuple(new)

    lax.fori_loop(0, per // rows, step, tuple(jnp.zeros((1, sw), F32) for _ in range(2 * bsz)))
    y = jnp.dot(u, toe_ref[...], preferred_element_type=F32)
    y = y + lax.dot_general(s_ref[...].astype(BF16), vt_ref[...], (((1,), (1,)), ((), ())),
                            preferred_element_type=F32)
    z_ref[0] = jax.nn.gelu(y).astype(BF16)


def _ssm(u, bsz, lam_re, lam_im, log_dt, b_re, b_im, c_re, c_im, d_skip):
    t, width = u.shape
    L, C = SSM_CHUNK, d_skip.shape[-1]
    G = width // C
    nc = t // L
    flat, pw, fbt, ct, consts = _ssm_operators(lam_re, lam_im, log_dt, b_re, b_im, c_re, c_im, d_skip)
    sw = consts.shape[-1]
    ug = u.reshape(nc, L, G, C).transpose(2, 0, 1, 3).reshape(G, nc, L * C)

    def group_spec(a):
        return pl.BlockSpec((1,) + a.shape[1:], lambda g: (g,) + (0,) * (a.ndim - 1))

    z = pl.pallas_call(
        functools.partial(_ssm_kernel, bsz=bsz),
        grid=(G,),
        in_specs=[group_spec(a) for a in (ug, flat, pw, fbt, ct, consts)],
        out_specs=pl.BlockSpec((1, nc, L * C), lambda g: (g, 0, 0)),
        out_shape=jax.ShapeDtypeStruct((G, nc, L * C), BF16),
        scratch_shapes=[pltpu.VMEM((L * C, L * C), BF16), pltpu.VMEM((L * C, 2 * sw), BF16),
                        pltpu.VMEM((L * C, sw), BF16), pltpu.VMEM((nc, 2 * sw), F32), pltpu.VMEM((nc, sw), F32)],
        compiler_params=_params("parallel"),
    )(ug, flat, pw, fbt, ct, consts)
    return z.reshape(G, nc, L, C).transpose(1, 2, 0, 3).reshape(t, width)


def _post_kernel(o0_ref, l0_ref, o1_ref, l1_ref, o2_ref, l2_ref, z_ref, gate_ref, x_ref,
                 wa_ref, wglu_ref, bglu_ref, ws_ref, wo_ref, g2_ref, wr_ref, wrlo_ref, br_ref, tri_ref,
                 x1_ref, hp_ref, idx_ref, rank_ref, wgt_ref, cnt_ref, o_scr, l_scr, carry_ref):
    tm, d = x_ref.shape

    @pl.when(pl.program_id(0) == 0)
    def _():
        carry_ref[...] = jnp.zeros_like(carry_ref)

    outs = [o0_ref[...].astype(F32)]
    lses = [l0_ref[...]]
    for (o_ref, l_ref), (_, dil) in zip(((o1_ref, l1_ref), (o2_ref, l2_ref)), ATTN_GROUPS[1:]):
        nlb = o_scr.shape[0]
        for r in range(dil):
            for c in range(nlb):
                o_scr[c, pl.ds(r, tm // dil, stride=dil), :] = o_ref[r, :, c * LANES:(c + 1) * LANES].astype(F32)
                l_scr[c, pl.ds(r, tm // dil, stride=dil), :] = l_ref[r, :, c * LANES:(c + 1) * LANES]
        outs.append(jnp.concatenate([o_scr[c] for c in range(nlb)], axis=-1))
        lses.append(jnp.concatenate([l_scr[c] for c in range(nlb)], axis=-1))
    top = jnp.maximum(jnp.maximum(lses[0], lses[1]), lses[2])
    es = [jnp.exp(l - top) for l in lses]
    y_attn = (es[0] * outs[0] + es[1] * outs[1] + es[2] * outs[2]) / (es[0] + es[1] + es[2])
    ya = jnp.dot(y_attn.astype(BF16), wa_ref[...], preferred_element_type=F32)
    z = z_ref[...]
    glu = jnp.dot(z, wglu_ref[...], preferred_element_type=F32) + bglu_ref[...]
    zz = z.astype(F32) * jax.nn.sigmoid(glu)
    ys = jnp.dot(zz.astype(BF16), ws_ref[...], preferred_element_type=F32)
    mixed = gate_ref[:, :d].astype(F32) * ya + gate_ref[:, d:].astype(F32) * ys
    x1 = x_ref[...] + jnp.dot(mixed.astype(BF16), wo_ref[...], preferred_element_type=F32)
    x1_ref[...] = x1
    ms = jnp.mean(x1 * x1, axis=-1, keepdims=True)
    h2 = x1 * lax.rsqrt(ms + NORM_EPS) * g2_ref[...]
    hb = h2.astype(BF16)
    hp_ref[...] = h2

    h_lo = (h2 - hb.astype(F32)).astype(BF16)
    nt = (((1,), (1,)), ((), ()))
    logits = (lax.dot_general(wr_ref[...], hb, nt, preferred_element_type=F32)
              + lax.dot_general(wrlo_ref[...], hb, nt, preferred_element_type=F32)
              + lax.dot_general(wr_ref[...], h_lo, nt, preferred_element_type=F32) + br_ref[...])
    ne = logits.shape[0]
    e_iota = lax.broadcasted_iota(jnp.int32, logits.shape, 0)
    vals, idxs = [], []
    work = logits
    for _ in range(TOP_K):
        m = jnp.max(work, axis=0, keepdims=True)
        i = jnp.min(jnp.where(work == m, e_iota, ne), axis=0, keepdims=True)
        vals.append(m)
        idxs.append(i)
        work = jnp.where(e_iota == i, -jnp.inf, work)
    ex = [jnp.exp(v - vals[0]) for v in vals]
    tot = ex[0] + ex[1] + ex[2] + ex[3]
    onehot = jnp.zeros(logits.shape, F32)
    for i in idxs:
        onehot = onehot + (e_iota == i).astype(F32)
    before = jnp.dot(onehot.astype(BF16), tri_ref[...], preferred_element_type=F32) + carry_ref[:, 0:1]
    for k in range(TOP_K):
        idx_ref[k:k + 1, :] = idxs[k]
        rank_ref[k:k + 1, :] = jnp.sum(jnp.where(e_iota == idxs[k], before, 0.0), axis=0,
                                       keepdims=True).astype(jnp.int32)
        wgt_ref[k:k + 1, :] = ex[k] / tot
    carry_ref[...] = carry_ref[...] + jnp.sum(onehot, axis=1, keepdims=True)
    cnt_ref[...] = carry_ref[...].astype(jnp.int32)


def _post_mix(x2, seq, attn, z, gates, w_attn_branch, w_glu, b_glu, w_ssm_branch, w_out, norm2, w_router, b_router):
    t, d = x2.shape
    tm = ROW_TILE
    gw = GROUP_WIDTH
    tiles_per_seq = seq // tm
    ne = w_router.shape[1]
    ssm_w = z.shape[1]
    in_specs, args = [], []
    for (o, lse), (_, dil) in zip(attn, ATTN_GROUPS):
        if dil == 1:
            spec = pl.BlockSpec((tm, gw), lambda i: (i, 0))
            o, lse = o.reshape(t, gw), lse.reshape(t, gw)
        else:
            spec = pl.BlockSpec((dil, tm // dil, gw), lambda i: (i // tiles_per_seq, i % tiles_per_seq, 0))
        in_specs += [spec, spec]
        args += [o, lse]
    tri = jnp.asarray(np.arange(tm)[:, None] < np.arange(tm)[None, :], BF16)
    in_specs += [pl.BlockSpec((tm, ssm_w), lambda i: (i, 0)),
                 pl.BlockSpec((tm, 2 * d), lambda i: (i, 0)),
                 pl.BlockSpec((tm, d), lambda i: (i, 0)),
                 _const_spec((gw, d)), _const_spec((ssm_w, ssm_w)), _const_spec((1, ssm_w)),
                 _const_spec((ssm_w, d)), _const_spec((d, d)), _const_spec((1, d)),
                 _const_spec((ne, d)), _const_spec((ne, d)), _const_spec((ne, 1)), _const_spec((tm, tm))]
    wr_t = w_router.T.astype(F32)
    wr_hi = wr_t.astype(BF16)
    wr_lo = (wr_t - wr_hi.astype(F32)).astype(BF16)
    args += [z, gates, x2, w_attn_branch.astype(BF16), w_glu.astype(BF16), b_glu.reshape(1, ssm_w),
             w_ssm_branch.astype(BF16), w_out.astype(BF16), norm2.reshape(1, d),
             wr_hi, wr_lo, b_router.reshape(ne, 1), tri]
    tok_spec = pl.BlockSpec((TOP_K, tm), lambda i: (0, i))
    return pl.pallas_call(
        _post_kernel,
        grid=(t // tm,),
        in_specs=in_specs,
        out_specs=[pl.BlockSpec((tm, d), lambda i: (i, 0)),
                   pl.BlockSpec((tm, d), lambda i: (i, 0)),
                   tok_spec, tok_spec, tok_spec, pl.BlockSpec((ne, LANES), lambda i: (0, 0))],
        out_shape=[jax.ShapeDtypeStruct((t, d), F32), jax.ShapeDtypeStruct((t, d), F32),
                   jax.ShapeDtypeStruct((TOP_K, t), jnp.int32), jax.ShapeDtypeStruct((TOP_K, t), jnp.int32),
                   jax.ShapeDtypeStruct((TOP_K, t), F32), jax.ShapeDtypeStruct((ne, LANES), jnp.int32)],
        scratch_shapes=[pltpu.VMEM((gw // LANES, tm, LANES), F32), pltpu.VMEM((gw // LANES, tm, LANES), F32),
                        pltpu.VMEM((ne, LANES), F32)],
        compiler_params=_params("arbitrary"),
    )(*args)


def _row_copy(src_ref, src_row, dst_ref, dst_row, sem):
    return pltpu.make_async_copy(src_ref.at[pl.ds(src_row, 1)], dst_ref.at[pl.ds(dst_row, 1)], sem)


def _tile_major(dest, tm):
    k, t = dest.shape
    return dest.reshape(k, t // tm, tm).transpose(1, 0, 2).reshape(-1)


def _rows_wait(ref, n_rows, sem):
    pltpu.make_async_copy(ref.at[pl.ds(0, n_rows)], ref.at[pl.ds(0, n_rows)], sem).wait()


def _dispatch_kernel(be_ref, nu_ref, dest_ref, h_ref, xs_ref, zero_ref, sems):
    i = pl.program_id(0)
    tm = h_ref.shape[0]
    tb = zero_ref.shape[0]
    n_blocks = be_ref.shape[0]

    @pl.when(i == 0)
    def _():
        zero_ref[...] = jnp.zeros_like(zero_ref)

        def holds_padding(b):
            return (b >= nu_ref[0] - 1) | (be_ref[b] != be_ref[jnp.minimum(b + 1, n_blocks - 1)])

        def zero_copy(b):
            return pltpu.make_async_copy(zero_ref, xs_ref.at[pl.ds(pl.multiple_of(b * tb, tb), tb)], sems.at[1])

        def start(b, c):
            @pl.when(holds_padding(b))
            def _():
                zero_copy(b).start()
            return c

        def wait(b, c):
            @pl.when(holds_padding(b))
            def _():
                zero_copy(b).wait()
            return c

        lax.fori_loop(0, n_blocks, start, 0)
        lax.fori_loop(0, n_blocks, wait, 0)

    def issue(t, c):
        for k in range(TOP_K):
            _row_copy(h_ref, t, xs_ref, dest_ref[k * tm + t], sems.at[0]).start(priority=k % 2)
        return c

    lax.fori_loop(0, tm, issue, 0, unroll=4)
    _rows_wait(xs_ref, TOP_K * tm, sems.at[0])


def _dispatch(block_exp, n_used, dest, hp, n_rows):
    t = hp.shape[0]
    slab = hp.shape[1:]
    tm = DISPATCH_TILE
    grid_spec = pltpu.PrefetchScalarGridSpec(
        num_scalar_prefetch=2,
        grid=(t // tm,),
        in_specs=[pl.BlockSpec((TOP_K * tm,), lambda i, be, nu: (i,), memory_space=pltpu.SMEM),
                  pl.BlockSpec((tm,) + slab, lambda i, be, nu: (i,) + (0,) * len(slab))],
        out_specs=pl.BlockSpec(memory_space=pl.ANY),
        scratch_shapes=[pltpu.VMEM((EXPERT_ROWS,) + slab, hp.dtype), pltpu.SemaphoreType.DMA((2,))],
    )
    return pl.pallas_call(
        _dispatch_kernel,
        grid_spec=grid_spec,
        out_shape=jax.ShapeDtypeStruct((n_rows,) + slab, hp.dtype),
        compiler_params=_params("arbitrary"),
    )(block_exp, n_used, _tile_major(dest, tm), hp)


def _expert_kernel(be_ref, nu_ref, xs_ref, wgu_ref, bgu_ref, wd_ref, bd_ref, ys_ref, wgu_bf, wd_bf):
    i = pl.program_id(0)
    active = i < nu_ref[0]

    @pl.when(active & ((i == 0) | (be_ref[i] != be_ref[jnp.maximum(i - 1, 0)])))
    def _():
        wgu_bf[...] = wgu_ref[0].astype(BF16)
        wd_bf[...] = wd_ref[0].astype(BF16)

    @pl.when(active)
    def _():
        gu = jnp.dot(xs_ref[...].astype(BF16), wgu_bf[...], preferred_element_type=F32) + bgu_ref[0]
        de = gu.shape[1] // 2
        gate = jnp.minimum(gu[:, :de], SWIGLU_LIMIT)
        up = jnp.clip(gu[:, de:], -SWIGLU_LIMIT, SWIGLU_LIMIT)
        act = gate * jax.nn.sigmoid(SWIGLU_ALPHA * gate) * (up + 1.0)
        ys_ref[...] = jnp.dot(act.astype(BF16), wd_bf[...], preferred_element_type=F32) + bd_ref[0]

    @pl.when(i >= nu_ref[0])
    def _():
        ys_ref[...] = jnp.zeros_like(ys_ref)


def _experts(block_exp, n_used, xs, w_gate_up, b_gate_up, w_down, b_down):
    n_rows = xs.shape[0]
    ne, d, de2 = w_gate_up.shape
    tb = EXPERT_ROWS

    def row_block(i, be, nu):
        return (jnp.minimum(i, nu[0] - 1), 0)

    def expert_block(i, be, nu):
        return (be[jnp.minimum(i, nu[0] - 1)], 0, 0)

    grid_spec = pltpu.PrefetchScalarGridSpec(
        num_scalar_prefetch=2,
        grid=(n_rows // tb,),
        in_specs=[pl.BlockSpec((tb,) + xs.shape[1:], row_block),
                  pl.BlockSpec((1, d, de2), expert_block),
                  pl.BlockSpec((1, 1, de2), expert_block),
                  pl.BlockSpec((1, de2 // 2, d), expert_block),
                  pl.BlockSpec((1, 1, d), expert_block)],
        out_specs=pl.BlockSpec((tb, d), lambda i, be, nu: (i, 0)),
        scratch_shapes=[pltpu.VMEM((d, de2), BF16), pltpu.VMEM((de2 // 2, d), BF16)],
    )
    return pl.pallas_call(
        _expert_kernel,
        grid_spec=grid_spec,
        out_shape=jax.ShapeDtypeStruct((n_rows, d), F32),
        compiler_params=_params("arbitrary"),
    )(block_exp, n_used, xs, w_gate_up, b_gate_up.reshape(ne, 1, de2), w_down, b_down.reshape(ne, 1, d))


def _combine_kernel(dest_ref, dest_next_ref, wgt_ref, x1_ref, ys_ref, out_ref, buf_ref, sems):
    i = pl.program_id(0)
    n = pl.num_programs(0)
    tm = x1_ref.shape[0]
    slot = i % 2

    def gather(d_ref, into):
        def issue(t, c):
            for k in range(TOP_K):
                _row_copy(ys_ref, d_ref[k * tm + t], buf_ref.at[into, k], t, sems.at[into]).start(priority=k % 2)
            return c

        lax.fori_loop(0, tm, issue, 0, unroll=4)

    @pl.when(i == 0)
    def _():
        gather(dest_ref, 0)

    @pl.when(i + 1 < n)
    def _():
        gather(dest_next_ref, 1 - slot)

    for k in range(TOP_K):
        _rows_wait(buf_ref.at[slot, k], tm, sems.at[slot])
    acc = x1_ref[...]
    for k in range(TOP_K):
        acc = acc + wgt_ref[:, k:k + 1] * buf_ref[slot, k]
    out_ref[...] = acc


def _combine(dest, wgt_t, x1, ys):
    t, d = x1.shape
    tm = ROUTE_TILE
    n = t // tm
    return pl.pallas_call(
        _combine_kernel,
        grid=(n,),
        in_specs=[pl.BlockSpec((TOP_K * tm,), lambda i: (i,), memory_space=pltpu.SMEM),
                  pl.BlockSpec((TOP_K * tm,), lambda i: (jnp.minimum(i + 1, n - 1),), memory_space=pltpu.SMEM),
                  pl.BlockSpec((tm, TOP_K), lambda i: (i, 0)),
                  pl.BlockSpec((tm, d), lambda i: (i, 0)),
                  pl.BlockSpec(memory_space=pl.ANY)],
        out_specs=pl.BlockSpec((tm, d), lambda i: (i, 0)),
        out_shape=jax.ShapeDtypeStruct((t, d), F32),
        scratch_shapes=[pltpu.VMEM((2, TOP_K, tm) + ys.shape[1:], F32), pltpu.SemaphoreType.DMA((2,))],
        compiler_params=_params("arbitrary"),
    )(_tile_major(dest, tm), _tile_major(dest, tm), wgt_t, x1, ys)


def _moe(x1, hp, idx, rank, wgt, counts, w_gate_up, b_gate_up, w_down, b_down):
    t = x1.shape[0]
    tb = EXPERT_ROWS
    ne = w_gate_up.shape[0]
    n_blocks = -(-(t * TOP_K) // tb) + ne
    padded = ((counts + tb - 1) // tb) * tb
    pad_end = jnp.cumsum(padded)
    pad_start = pad_end - padded
    dest = rank + jnp.sum(jnp.where(idx[None] == jnp.arange(ne, dtype=jnp.int32)[:, None, None],
                                    pad_start.astype(jnp.int32)[:, None, None], 0), axis=0)
    block_start = jnp.arange(n_blocks, dtype=jnp.int32) * tb
    block_exp = jnp.minimum(jnp.sum((pad_end[None, :] <= block_start[:, None]).astype(jnp.int32), axis=1), ne - 1)
    n_used = (pad_end[-1:] // tb).astype(jnp.int32)
    xs = _dispatch(block_exp, n_used, dest, hp, n_blocks * tb)
    ys = _experts(block_exp, n_used, xs, w_gate_up, b_gate_up, w_down, b_down)
    return _combine(dest, wgt.T, x1, ys)


def kernel(x, norm1, w_in, q_norm, k_norm, lam_re, lam_im, log_dt, b_re, b_im, c_re, c_im, d_skip, w_glu, b_glu, w_attn_branch, w_ssm_branch, w_out, norm2, w_router, b_router, w_gate_up, b_gate_up, w_down, b_down):
    bsz, seq, d = x.shape
    t = bsz * seq
    x2 = x.reshape(t, d)
    for l in range(norm1.shape[0]):
        qkv0, qkv1, qkv2, u, gates = _in_proj(x2, norm1[l], w_in[l], q_norm[l], k_norm[l], seq)
        attn = [_attention(qkv0.reshape(bsz, seq, -1)), _attention(qkv1), _attention(qkv2)]
        z = _ssm(u, bsz, lam_re[l], lam_im[l], log_dt[l], b_re[l], b_im[l], c_re[l], c_im[l], d_skip[l])
        x1, hp, idx, rank, wgt, counts = _post_mix(
            x2, seq, attn, z, gates, w_attn_branch[l], w_glu[l], b_glu[l], w_ssm_branch[l], w_out[l], norm2[l],
            w_router[l], b_router[l])
        x2 = _moe(x1, hp, idx, rank, wgt, counts[:, 0], w_gate_up[l], b_gate_up[l], w_down[l], b_down[l])
    return x2.reshape(bsz, seq, d)
```

```python
import functools
import math

import numpy as np
import jax
import jax.numpy as jnp
from jax import lax
from jax.experimental import pallas as pl
from jax.experimental.pallas import tpu as pltpu

F32 = jnp.float32
BF16 = jnp.bfloat16

HEAD_DIM = 64
HEADS_PER_GROUP = 4
ATTN_GROUPS = ((128, 1), (512, 4), (2048, 16))
GROUP_WIDTH = HEADS_PER_GROUP * HEAD_DIM
ATTN_WIDTH = GROUP_WIDTH * len(ATTN_GROUPS)
BAND = 64
ROPE_THETA = 10000.0
SSM_GROUP = 16
SSM_STATE = 64
N_EXPERTS = 32
TOP_K = 4
SWIGLU_ALPHA = 1.702
SWIGLU_LIMIT = 7.0
NORM_EPS = 1e-6
MASK_VALUE = -1e30

ROW_TILE = 512
ATTN_Q_TILE = 1024
ATTN_Q_SUB = 128
SSM_CHUNK = 64
SSM_POWER_ROWS = -(-(SSM_CHUNK + 1) // 8) * 8
DISPATCH_TILE = 1024
ROUTE_TILE = 512
ROW_DMA_UNROLL = 8
EXPERT_ROWS = 512
VMEM_LIMIT = 56 * 1024 * 1024
LANES = 128


def _params(*sem):
    return pltpu.CompilerParams(dimension_semantics=sem, vmem_limit_bytes=VMEM_LIMIT)


def _const_spec(shape):
    return pl.BlockSpec(shape, lambda *_: (0,) * len(shape), pipeline_mode=pl.Buffered(1))


def _permute_qk(a):
    lead = a.shape[:-1]
    n = len(lead)
    a = a.reshape(*lead, len(ATTN_GROUPS), HEADS_PER_GROUP, 2, HEAD_DIM // 2)
    return a.transpose(*range(n), n, n + 2, n + 1, n + 3).reshape(*lead, ATTN_WIDTH)


def _permute_group_lanes(piece):
    hd, hh = HEAD_DIM, HEAD_DIM // 2
    firsts = [piece[:, h * hd:h * hd + hh] for h in range(HEADS_PER_GROUP)]
    seconds = [piece[:, h * hd + hh:(h + 1) * hd] for h in range(HEADS_PER_GROUP)]
    return jnp.concatenate(firsts + seconds, axis=-1)


def _in_proj_kernel(x_ref, g1_ref, wf_ref, gq_ref, gk_ref, cos_ref, sin_ref, seg_ref,
                    qkv0_ref, qkv1_ref, qkv2_ref, u_ref, gate_ref, w_ref, scr_ref):
    @pl.when(pl.program_id(0) == 0)
    def _():
        rows_per_step = 128

        def cast_rows(i, _):
            rows = pl.ds(pl.multiple_of(i * rows_per_step, rows_per_step), rows_per_step)
            for c0 in range(0, 2 * ATTN_WIDTH, GROUP_WIDTH):
                w_ref[rows, c0:c0 + GROUP_WIDTH] = _permute_group_lanes(wf_ref[rows, c0:c0 + GROUP_WIDTH]).astype(BF16)
            for c0 in range(2 * ATTN_WIDTH, wf_ref.shape[1], GROUP_WIDTH):
                w_ref[rows, c0:c0 + GROUP_WIDTH] = wf_ref[rows, c0:c0 + GROUP_WIDTH].astype(BF16)
            return 0

        lax.fori_loop(0, wf_ref.shape[0] // rows_per_step, cast_rows, 0)

    x = x_ref[...]
    ms = jnp.mean(x * x, axis=-1, keepdims=True)
    h = (x * lax.rsqrt(ms + NORM_EPS) * g1_ref[...]).astype(BF16)
    cos = cos_ref[...]
    sin = sin_ref[...]
    seg = seg_ref[...]
    half = GROUP_WIDTH // 2
    tm = x.shape[0]

    def proj(c0, width):
        return jnp.dot(h, w_ref[:, c0:c0 + width], preferred_element_type=F32)

    def sum_squares(t):
        return t[:, :half] * t[:, :half] + t[:, half:] * t[:, half:]

    def norm_rope(t, tot, gain_ref, g):
        t1, t2 = t[:, :half], t[:, half:]
        r = lax.rsqrt(tot * (1.0 / HEAD_DIM) + NORM_EPS)
        a = t1 * r * gain_ref[:, g * GROUP_WIDTH:g * GROUP_WIDTH + half]
        b = t2 * r * gain_ref[:, g * GROUP_WIDTH + half:(g + 1) * GROUP_WIDTH]
        return a * cos - b * sin, b * cos + a * sin

    outs = (qkv0_ref, qkv1_ref, qkv2_ref)
    for g, (_, dil) in enumerate(ATTN_GROUPS):
        tq = proj(g * GROUP_WIDTH, GROUP_WIDTH)
        tk = proj(ATTN_WIDTH + g * GROUP_WIDTH, GROUP_WIDTH)
        ss = jnp.concatenate([sum_squares(tq), sum_squares(tk)], axis=-1).astype(BF16)
        tot = jnp.dot(ss, seg, preferred_element_type=F32)
        q1, q2 = norm_rope(tq, tot[:, :half], gq_ref, g)
        k1, k2 = norm_rope(tk, tot[:, half:], gk_ref, g)
        v = proj(2 * ATTN_WIDTH + g * GROUP_WIDTH, GROUP_WIDTH)
        qkv = jnp.concatenate([q1, q2, k1, k2, v], axis=-1)
        if dil == 1:
            outs[g][...] = qkv.astype(BF16)
        else:
            for c in range(qkv.shape[1] // LANES):
                scr_ref[c] = qkv[:, c * LANES:(c + 1) * LANES]
            for r in range(dil):
                for c in range(qkv.shape[1] // LANES):
                    outs[g][r, :, c * LANES:(c + 1) * LANES] = (
                        scr_ref[c, pl.ds(r, tm // dil, stride=dil), :].astype(BF16))
    o_u = 3 * ATTN_WIDTH
    u_w = u_ref.shape[-1]
    u_ref[...] = proj(o_u, u_w).astype(BF16)
    gate_w = gate_ref.shape[-1]
    for c in range(0, gate_w, 1024):
        gate_ref[:, c:c + 1024] = jax.nn.sigmoid(proj(o_u + u_w + c, 1024)).astype(BF16)


def _in_proj(x2, norm1, w_in, q_norm, k_norm, seq):
    t, d = x2.shape
    tm = ROW_TILE
    in_w = w_in.shape[1]
    ssm_w = d // 2
    gate_w = in_w - 3 * ATTN_WIDTH - ssm_w
    gq = (_permute_qk(q_norm.reshape(-1)) * (HEAD_DIM ** -0.5)).reshape(1, ATTN_WIDTH)
    gk = _permute_qk(k_norm.reshape(-1)).reshape(1, ATTN_WIDTH)
    inv_freq = ROPE_THETA ** (-jnp.arange(0, HEAD_DIM, 2, dtype=F32) / HEAD_DIM)
    ang = jnp.arange(seq, dtype=F32)[:, None] * inv_freq[None, :]
    cos = jnp.tile(jnp.cos(ang), (1, HEADS_PER_GROUP))
    sin = jnp.tile(jnp.sin(ang), (1, HEADS_PER_GROUP))
    lane = np.arange(GROUP_WIDTH)
    seg = jnp.asarray(lane[:, None] // (HEAD_DIM // 2) == lane[None, :] // (HEAD_DIM // 2), BF16)
    bsz = t // seq
    tiles_per_seq = seq // tm
    qkv_w = 3 * GROUP_WIDTH
    out_shape = [jax.ShapeDtypeStruct((t, qkv_w), BF16)]
    out_specs = [pl.BlockSpec((tm, qkv_w), lambda i: (i, 0))]
    for _, dil in ATTN_GROUPS[1:]:
        out_shape.append(jax.ShapeDtypeStruct((bsz * dil, seq // dil, qkv_w), BF16))
        out_specs.append(pl.BlockSpec((dil, tm // dil, qkv_w),
                                      lambda i: (i // tiles_per_seq, i % tiles_per_seq, 0)))
    out_shape += [jax.ShapeDtypeStruct((t, ssm_w), BF16), jax.ShapeDtypeStruct((t, gate_w), BF16)]
    out_specs += [pl.BlockSpec((tm, ssm_w), lambda i: (i, 0)), pl.BlockSpec((tm, gate_w), lambda i: (i, 0))]
    return pl.pallas_call(
        _in_proj_kernel,
        grid=(t // tm,),
        in_specs=[
            pl.BlockSpec((tm, d), lambda i: (i, 0)),
            _const_spec((1, d)),
            _const_spec((d, in_w)),
            _const_spec((1, ATTN_WIDTH)),
            _const_spec((1, ATTN_WIDTH)),
            pl.BlockSpec((tm, GROUP_WIDTH // 2), lambda i: (i % tiles_per_seq, 0)),
            pl.BlockSpec((tm, GROUP_WIDTH // 2), lambda i: (i % tiles_per_seq, 0)),
            _const_spec((GROUP_WIDTH, GROUP_WIDTH)),
        ],
        out_specs=out_specs,
        out_shape=out_shape,
        scratch_shapes=[pltpu.VMEM((d, in_w), BF16), pltpu.VMEM((qkv_w // LANES, tm, LANES), F32)],
        compiler_params=_params("arbitrary"),
    )(x2, norm1.reshape(1, d), w_in, gq, gk, cos, sin, seg)


def _attn_kernel(c_ref, kl_ref, vl_ref, kr_ref, vr_ref, o_ref, lse_ref, kcat_ref, vcat_ref, *, length):
    tq = c_ref.shape[1]
    gw = GROUP_WIDTH
    j = pl.program_id(1)
    kcat_ref[0:BAND] = kl_ref[0]
    kcat_ref[BAND:BAND + tq] = c_ref[0, :, gw:2 * gw]
    kcat_ref[BAND + tq:] = kr_ref[0]
    vcat_ref[0:BAND] = vl_ref[0]
    vcat_ref[BAND:BAND + tq] = c_ref[0, :, 2 * gw:]
    vcat_ref[BAND + tq:] = vr_ref[0]
    sub = min(ATTN_Q_SUB, tq)
    nk = sub + 2 * BAND
    nh = HEADS_PER_GROUP
    row = lax.broadcasted_iota(jnp.int32, (nh * sub, nk), 0) & (sub - 1)
    col = lax.broadcasted_iota(jnp.int32, (nh * sub, nk), 1)
    in_band = jnp.abs(col - BAND - row) <= BAND
    q_lane = lax.broadcasted_iota(jnp.int32, (1, gw), 1)
    q_head = (q_lane % (gw // 2)) // (HEAD_DIM // 2)
    v_head = q_lane // HEAD_DIM
    for s in range(tq // sub):
        q_s = c_ref[0, s * sub:(s + 1) * sub, 0:gw]
        k_s = kcat_ref[s * sub:s * sub + nk]
        v_s = vcat_ref[s * sub:s * sub + nk]
        kpos = j * tq + s * sub - BAND + col
        valid = in_band & (kpos >= 0) & (kpos < length)
        q4 = jnp.concatenate([jnp.where(q_head == hh, q_s, jnp.zeros_like(q_s)) for hh in range(nh)], axis=0)
        sc = lax.dot_general(q4, k_s, (((1,), (1,)), ((), ())), preferred_element_type=F32)
        sc = jnp.where(valid, sc, MASK_VALUE)
        m = jnp.max(sc, axis=-1, keepdims=True)
        p = jnp.exp(sc - m)
        den = jnp.sum(p, axis=-1, keepdims=True)
        o4 = jnp.dot(p.astype(BF16), v_s, preferred_element_type=F32) * (1.0 / den)
        l4 = m + jnp.log(den)
        out = o4[0:sub]
        lse = jnp.broadcast_to(l4[0:sub], (sub, gw))
        for hh in range(1, nh):
            out = jnp.where(v_head == hh, o4[hh * sub:(hh + 1) * sub], out)
            lse = jnp.where(v_head == hh, l4[hh * sub:(hh + 1) * sub], lse)
        o_ref[0, s * sub:(s + 1) * sub] = out.astype(BF16)
        lse_ref[0, s * sub:(s + 1) * sub] = lse


def _attention(qkv):
    n, length, _ = qkv.shape
    tq = min(ATTN_Q_TILE, length)
    gw = GROUP_WIDTH
    hb = tq // BAND
    last = length // BAND - 1

    def left(lane_block):
        return pl.BlockSpec((1, BAND, gw), lambda b, j: (b, jnp.maximum(j * hb - 1, 0), lane_block))

    def right(lane_block):
        return pl.BlockSpec((1, BAND, gw), lambda b, j: (b, jnp.minimum((j + 1) * hb, last), lane_block))

    return pl.pallas_call(
        functools.partial(_attn_kernel, length=length),
        grid=(n, length // tq),
        in_specs=[pl.BlockSpec((1, tq, 3 * gw), lambda b, j: (b, j, 0)), left(1), left(2), right(1), right(2)],
        out_specs=[pl.BlockSpec((1, tq, gw), lambda b, j: (b, j, 0)),
                   pl.BlockSpec((1, tq, gw), lambda b, j: (b, j, 0))],
        out_shape=[jax.ShapeDtypeStruct((n, length, gw), BF16), jax.ShapeDtypeStruct((n, length, gw), F32)],
        scratch_shapes=[pltpu.VMEM((tq + 2 * BAND, gw), BF16), pltpu.VMEM((tq + 2 * BAND, gw), BF16)],
        compiler_params=_params("parallel", "parallel"),
    )(qkv, qkv, qkv, qkv, qkv)


def _ssm_operators(lam_re, lam_im, log_dt, b_re, b_im, c_re, c_im, d_skip):
    L = SSM_CHUNK
    G, P = lam_re.shape[1:]
    C = d_skip.shape[-1]
    hp = lax.Precision.HIGHEST
    n = jnp.arange(L + 1, dtype=F32)[:, None, None]
    pad = SSM_POWER_ROWS - (L + 1)
    kern, pw, fbt, ct, consts = [], [], [], [], []
    for direction in range(2):
        lr, li = lam_re[direction].astype(F32), lam_im[direction].astype(F32)
        dt = jnp.exp(log_dt[direction].astype(F32))[:, None]
        mag = jnp.exp(lr * dt)
        ab_re, ab_im = mag * jnp.cos(li * dt), mag * jnp.sin(li * dt)
        den = lr * lr + li * li
        nr = ab_re - 1.0
        f_re = (nr * lr + ab_im * li) / den
        f_im = (ab_im * lr - nr * li) / den
        pmag = jnp.exp(n * (lr * dt)[None])
        p_re, p_im = pmag * jnp.cos(n * (li * dt)[None]), pmag * jnp.sin(n * (li * dt)[None])
        br, bi = b_re[direction].astype(F32), b_im[direction].astype(F32)
        fb_re = f_re[..., None] * br - f_im[..., None] * bi
        fb_im = f_re[..., None] * bi + f_im[..., None] * br
        cr, ci = c_re[direction].astype(F32), c_im[direction].astype(F32)
        pr, pi = p_re[:L].transpose(1, 2, 0)[..., None], p_im[:L].transpose(1, 2, 0)[..., None]
        crt, cit = cr.transpose(0, 2, 1)[:, :, None, :], ci.transpose(0, 2, 1)[:, :, None, :]
        e_re = (pr * crt - pi * cit).reshape(G, P, L * C)
        e_im = (pr * cit + pi * crt).reshape(G, P, L * C)
        kern.append((jnp.einsum('gpc,gpx->gcx', fb_re, e_re, precision=hp)
                     - jnp.einsum('gpc,gpx->gcx', fb_im, e_im, precision=hp)).reshape(G, C, L, C))
        for p in (p_re, p_im):
            p = jnp.pad(p.transpose(1, 0, 2), ((0, 0), (0, pad), (0, 0)))
            pw.append(jnp.concatenate([p, p, p, p], axis=-1))
        tr, ti = fb_re.transpose(0, 2, 1), fb_im.transpose(0, 2, 1)
        fbt.append(jnp.concatenate([tr, ti, ti, tr], axis=-1))
        fbt.append(jnp.concatenate([-ti, tr, tr, -ti], axis=-1))
        ct.append(jnp.concatenate([cr, -ci], axis=-1))
        ct.append(jnp.concatenate([-ci, -cr], axis=-1))
        a_re, a_im = p_re[L], p_im[L]
        consts.append(jnp.concatenate([a_re, a_re, a_re, a_re], axis=-1))
        consts.append(jnp.concatenate([-a_im, a_im, a_im, -a_im], axis=-1))
    kf, kb = kern
    zero = kf[:, :, 0] + kb[:, :, 0] + jnp.eye(C, dtype=F32)[None] * d_skip.astype(F32)[:, None, :]
    kfull = jnp.concatenate([kb[:, :, :0:-1], zero[:, :, None], kf[:, :, 1:], jnp.zeros_like(zero)[:, :, None]],
                            axis=2)
    flat = kfull.astype(BF16).reshape(G, C, 2 * L * C)
    return (flat, jnp.stack(pw, axis=1), jnp.stack(fbt, axis=1), jnp.stack(ct, axis=1),
            jnp.stack(consts + consts, axis=1))


def _ssm_kernel(u_ref, flat_ref, pw_ref, fbt_ref, ct_ref, cst_ref, z_ref,
                toe_ref, wst_ref, vt_ref, f_ref, s_ref, *, bsz):
    u = u_ref[0]
    nc = u.shape[0]
    sw = cst_ref.shape[-1]
    hw = sw // 2
    C = fbt_ref.shape[2]
    L = toe_ref.shape[0] // C
    lc = L * C

    per_tile = LANES // C
    span = (2 * L - per_tile) * C
    for b in range(per_tile):
        shifted = flat_ref[0, :, (per_tile - 1 - b) * C:(per_tile - 1 - b) * C + span]
        for a in range(L // per_tile):
            t_in = a * per_tile + b
            off = (L // per_tile - 1 - a) * LANES
            toe_ref[t_in * C:(t_in + 1) * C, :] = shifted[:, off:off + lc]

    fa_f, fb_f, fa_b, fb_b = fbt_ref[0, 0], fbt_ref[0, 1], fbt_ref[0, 2], fbt_ref[0, 3]
    ca_f, cb_f, ca_b, cb_b = ct_ref[0, 0], ct_ref[0, 1], ct_ref[0, 2], ct_ref[0, 3]

    def power(k, n):
        return pw_ref[0, k, n:n + 1, :]

    for t in range(L):
        rows = slice(t * C, (t + 1) * C)
        wst_ref[rows, 0:sw] = (fa_f * power(0, L - 1 - t) + fb_f * power(1, L - 1 - t)).astype(BF16)
        wst_ref[rows, sw:2 * sw] = (fa_b * power(2, t) + fb_b * power(3, t)).astype(BF16)
        vt_ref[rows, 0:hw] = (ca_f * power(0, t + 1)[:, :hw] + cb_f * power(1, t + 1)[:, :hw]).astype(BF16)
        vt_ref[rows, hw:sw] = (ca_b * power(2, L - t)[:, :hw] + cb_b * power(3, L - t)[:, :hw]).astype(BF16)

    f_ref[...] = jnp.dot(u, wst_ref[...], preferred_element_type=F32)
    cst = cst_ref[0]
    a1f, a2f, a1b, a2b = cst[0:1], cst[1:2], cst[2:3], cst[3:4]
    per = nc // bsz

    def swap(v):
        return jnp.concatenate([v[:, hw:], v[:, :hw]], axis=-1)

    rows = 8

    def step(i, carry):
        new = []
        for b in range(bsz):
            sf, sb = carry[2 * b], carry[2 * b + 1]
            base_f = pl.multiple_of(b * per + i * rows, rows)
            base_b = pl.multiple_of(b * per + per - rows - i * rows, rows)
            f_blk = f_ref[pl.ds(base_f, rows), 0:sw]
            b_blk = f_ref[pl.ds(base_b, rows), sw:2 * sw]
            f_rows, b_rows = [], []
            for r in range(rows):
                f_rows.append(sf[:, :hw])
                sf = a1f * sf + a2f * swap(sf) + f_blk[r:r + 1]
                b_rows.append(sb[:, :hw])
                sb = a1b * sb + a2b * swap(sb) + b_blk[rows - 1 - r:rows - r]
            s_ref[pl.ds(base_f, rows), 0:hw] = jnp.concatenate(f_rows, axis=0)
            s_ref[pl.ds(base_b, rows), hw:sw] = jnp.concatenate(b_rows[::-1], axis=0)
            new += [sf, sb]
        return tuple(new)

    lax.fori_loop(0, per // rows, step, tuple(jnp.zeros((1, sw), F32) for _ in range(2 * bsz)))
    y = jnp.dot(u, toe_ref[...], preferred_element_type=F32)
    y = y + lax.dot_general(s_ref[...].astype(BF16), vt_ref[...], (((1,), (1,)), ((), ())),
                            preferred_element_type=F32)
    z_ref[0] = jax.nn.gelu(y).astype(BF16)


def _ssm(u, bsz, lam_re, lam_im, log_dt, b_re, b_im, c_re, c_im, d_skip):
    t, width = u.shape
    L, C = SSM_CHUNK, d_skip.shape[-1]
    G = width // C
    nc = t // L
    flat, pw, fbt, ct, consts = _ssm_operators(lam_re, lam_im, log_dt, b_re, b_im, c_re, c_im, d_skip)
    sw = consts.shape[-1]
    ug = u.reshape(nc, L, G, C).transpose(2, 0, 1, 3).reshape(G, nc, L * C)

    def group_spec(a):
        return pl.BlockSpec((1,) + a.shape[1:], lambda g: (g,) + (0,) * (a.ndim - 1))

    z = pl.pallas_call(
        functools.partial(_ssm_kernel, bsz=bsz),
        grid=(G,),
        in_specs=[group_spec(a) for a in (ug, flat, pw, fbt, ct, consts)],
        out_specs=pl.BlockSpec((1, nc, L * C), lambda g: (g, 0, 0)),
        out_shape=jax.ShapeDtypeStruct((G, nc, L * C), BF16),
        scratch_shapes=[pltpu.VMEM((L * C, L * C), BF16), pltpu.VMEM((L * C, 2 * sw), BF16),
                        pltpu.VMEM((L * C, sw), BF16), pltpu.VMEM((nc, 2 * sw), F32), pltpu.VMEM((nc, sw), F32)],
        compiler_params=_params("parallel"),
    )(ug, flat, pw, fbt, ct, consts)
    return z.reshape(G, nc, L, C).transpose(1, 2, 0, 3).reshape(t, width)


def _post_kernel(o0_ref, l0_ref, o1_ref, l1_ref, o2_ref, l2_ref, z_ref, gate_ref, x_ref,
                 wa_ref, wglu_ref, bglu_ref, ws_ref, wo_ref, g2_ref, wr_ref, wrlo_ref, br_ref, tri_ref,
                 x1_ref, hp_ref, idx_ref, rank_ref, wgt_ref, cnt_ref, o_scr, l_scr, carry_ref):
    tm, d = x_ref.shape

    @pl.when(pl.program_id(0) == 0)
    def _():
        carry_ref[...] = jnp.zeros_like(carry_ref)

    outs = [o0_ref[...].astype(F32)]
    lses = [l0_ref[...]]
    for (o_ref, l_ref), (_, dil) in zip(((o1_ref, l1_ref), (o2_ref, l2_ref)), ATTN_GROUPS[1:]):
        nlb = o_scr.shape[0]
        for r in range(dil):
            for c in range(nlb):
                o_scr[c, pl.ds(r, tm // dil, stride=dil), :] = o_ref[r, :, c * LANES:(c + 1) * LANES].astype(F32)
                l_scr[c, pl.ds(r, tm // dil, stride=dil), :] = l_ref[r, :, c * LANES:(c + 1) * LANES]
        outs.append(jnp.concatenate([o_scr[c] for c in range(nlb)], axis=-1))
        lses.append(jnp.concatenate([l_scr[c] for c in range(nlb)], axis=-1))
    top = jnp.maximum(jnp.maximum(lses[0], lses[1]), lses[2])
    es = [jnp.exp(l - top) for l in lses]
    y_attn = (es[0] * outs[0] + es[1] * outs[1] + es[2] * outs[2]) / (es[0] + es[1] + es[2])
    ya = jnp.dot(y_attn.astype(BF16), wa_ref[...], preferred_element_type=F32)
    z = z_ref[...]
    glu = jnp.dot(z, wglu_ref[...], preferred_element_type=F32) + bglu_ref[...]
    zz = z.astype(F32) * jax.nn.sigmoid(glu)
    ys = jnp.dot(zz.astype(BF16), ws_ref[...], preferred_element_type=F32)
    mixed = gate_ref[:, :d].astype(F32) * ya + gate_ref[:, d:].astype(F32) * ys
    x1 = x_ref[...] + jnp.dot(mixed.astype(BF16), wo_ref[...], preferred_element_type=F32)
    x1_ref[...] = x1
    ms = jnp.mean(x1 * x1, axis=-1, keepdims=True)
    h2 = x1 * lax.rsqrt(ms + NORM_EPS) * g2_ref[...]
    hb = h2.astype(BF16)
    hp_ref[...] = h2

    h_lo = (h2 - hb.astype(F32)).astype(BF16)
    nt = (((1,), (1,)), ((), ()))
    logits = (lax.dot_general(wr_ref[...], hb, nt, preferred_element_type=F32)
              + lax.dot_general(wrlo_ref[...], hb, nt, preferred_element_type=F32)
              + lax.dot_general(wr_ref[...], h_lo, nt, preferred_element_type=F32) + br_ref[...])
    ne = logits.shape[0]
    e_iota = lax.broadcasted_iota(jnp.int32, logits.shape, 0)
    vals, idxs = [], []
    work = logits
    for _ in range(TOP_K):
        m = jnp.max(work, axis=0, keepdims=True)
        i = jnp.min(jnp.where(work == m, e_iota, ne), axis=0, keepdims=True)
        vals.append(m)
        idxs.append(i)
        work = jnp.where(e_iota == i, -jnp.inf, work)
    ex = [jnp.exp(v - vals[0]) for v in vals]
    tot = ex[0] + ex[1] + ex[2] + ex[3]
    onehot = jnp.zeros(logits.shape, F32)
    for i in idxs:
        onehot = onehot + (e_iota == i).astype(F32)
    before = jnp.dot(onehot.astype(BF16), tri_ref[...], preferred_element_type=F32) + carry_ref[:, 0:1]
    for k in range(TOP_K):
        idx_ref[k:k + 1, :] = idxs[k]
        rank_ref[k:k + 1, :] = jnp.sum(jnp.where(e_iota == idxs[k], before, 0.0), axis=0,
                                       keepdims=True).astype(jnp.int32)
        wgt_ref[k:k + 1, :] = ex[k] / tot
    carry_ref[...] = carry_ref[...] + jnp.sum(onehot, axis=1, keepdims=True)
    cnt_ref[...] = carry_ref[...].astype(jnp.int32)


def _post_mix(x2, seq, attn, z, gates, w_attn_branch, w_glu, b_glu, w_ssm_branch, w_out, norm2, w_router, b_router):
    t, d = x2.shape
    tm = ROW_TILE
    gw = GROUP_WIDTH
    tiles_per_seq = seq // tm
    ne = w_router.shape[1]
    ssm_w = z.shape[1]
    in_specs, args = [], []
    for (o, lse), (_, dil) in zip(attn, ATTN_GROUPS):
        if dil == 1:
            spec = pl.BlockSpec((tm, gw), lambda i: (i, 0))
            o, lse = o.reshape(t, gw), lse.reshape(t, gw)
        else:
            spec = pl.BlockSpec((dil, tm // dil, gw), lambda i: (i // tiles_per_seq, i % tiles_per_seq, 0))
        in_specs += [spec, spec]
        args += [o, lse]
    tri = jnp.asarray(np.arange(tm)[:, None] < np.arange(tm)[None, :], BF16)
    in_specs += [pl.BlockSpec((tm, ssm_w), lambda i: (i, 0)),
                 pl.BlockSpec((tm, 2 * d), lambda i: (i, 0)),
                 pl.BlockSpec((tm, d), lambda i: (i, 0)),
                 _const_spec((gw, d)), _const_spec((ssm_w, ssm_w)), _const_spec((1, ssm_w)),
                 _const_spec((ssm_w, d)), _const_spec((d, d)), _const_spec((1, d)),
                 _const_spec((ne, d)), _const_spec((ne, d)), _const_spec((ne, 1)), _const_spec((tm, tm))]
    wr_t = w_router.T.astype(F32)
    wr_hi = wr_t.astype(BF16)
    wr_lo = (wr_t - wr_hi.astype(F32)).astype(BF16)
    args += [z, gates, x2, w_attn_branch.astype(BF16), w_glu.astype(BF16), b_glu.reshape(1, ssm_w),
             w_ssm_branch.astype(BF16), w_out.astype(BF16), norm2.reshape(1, d),
             wr_hi, wr_lo, b_router.reshape(ne, 1), tri]
    tok_spec = pl.BlockSpec((TOP_K, tm), lambda i: (0, i))
    return pl.pallas_call(
        _post_kernel,
        grid=(t // tm,),
        in_specs=in_specs,
        out_specs=[pl.BlockSpec((tm, d), lambda i: (i, 0)),
                   pl.BlockSpec((tm, d), lambda i: (i, 0)),
                   tok_spec, tok_spec, tok_spec, pl.BlockSpec((ne, LANES), lambda i: (0, 0))],
        out_shape=[jax.ShapeDtypeStruct((t, d), F32), jax.ShapeDtypeStruct((t, d), F32),
                   jax.ShapeDtypeStruct((TOP_K, t), jnp.int32), jax.ShapeDtypeStruct((TOP_K, t), jnp.int32),
                   jax.ShapeDtypeStruct((TOP_K, t), F32), jax.ShapeDtypeStruct((ne, LANES), jnp.int32)],
        scratch_shapes=[pltpu.VMEM((gw // LANES, tm, LANES), F32), pltpu.VMEM((gw // LANES, tm, LANES), F32),
                        pltpu.VMEM((ne, LANES), F32)],
        compiler_params=_params("arbitrary"),
    )(*args)


def _row_copy(src_ref, src_row, dst_ref, dst_row, sem):
    return pltpu.make_async_copy(src_ref.at[pl.ds(src_row, 1)], dst_ref.at[pl.ds(dst_row, 1)], sem)


def _tile_major(dest, tm):
    k, t = dest.shape
    return dest.reshape(k, t // tm, tm).transpose(1, 0, 2).reshape(-1)


def _rows_wait(ref, n_rows, sem):
    pltpu.make_async_copy(ref.at[pl.ds(0, n_rows)], ref.at[pl.ds(0, n_rows)], sem).wait()


def _dispatch_kernel(be_ref, nu_ref, dest_ref, h_ref, xs_ref, zero_ref, sems):
    i = pl.program_id(0)
    tm = h_ref.shape[0]
    tb = zero_ref.shape[0]
    n_blocks = be_ref.shape[0]

    @pl.when(i == 0)
    def _():
        zero_ref[...] = jnp.zeros_like(zero_ref)

        def holds_padding(b):
            return (b >= nu_ref[0] - 1) | (be_ref[b] != be_ref[jnp.minimum(b + 1, n_blocks - 1)])

        def zero_copy(b):
            return pltpu.make_async_copy(zero_ref, xs_ref.at[pl.ds(pl.multiple_of(b * tb, tb), tb)], sems.at[1])

        def start(b, c):
            @pl.when(holds_padding(b))
            def _():
                zero_copy(b).start()
            return c

        def wait(b, c):
            @pl.when(holds_padding(b))
            def _():
                zero_copy(b).wait()
            return c

        lax.fori_loop(0, n_blocks, start, 0)
        lax.fori_loop(0, n_blocks, wait, 0)

    def issue(t, c):
        for k in range(TOP_K):
            _row_copy(h_ref, t, xs_ref, dest_ref[k * tm + t], sems.at[0]).start(priority=k % 2)
        return c

    lax.fori_loop(0, tm, issue, 0, unroll=ROW_DMA_UNROLL)
    _rows_wait(xs_ref, TOP_K * tm, sems.at[0])


def _dispatch(block_exp, n_used, dest, hp, n_rows):
    t = hp.shape[0]
    slab = hp.shape[1:]
    tm = DISPATCH_TILE
    grid_spec = pltpu.PrefetchScalarGridSpec(
        num_scalar_prefetch=2,
        grid=(t // tm,),
        in_specs=[pl.BlockSpec((TOP_K * tm,), lambda i, be, nu: (i,), memory_space=pltpu.SMEM),
                  pl.BlockSpec((tm,) + slab, lambda i, be, nu: (i,) + (0,) * len(slab))],
        out_specs=pl.BlockSpec(memory_space=pl.ANY),
        scratch_shapes=[pltpu.VMEM((EXPERT_ROWS,) + slab, hp.dtype), pltpu.SemaphoreType.DMA((2,))],
    )
    return pl.pallas_call(
        _dispatch_kernel,
        grid_spec=grid_spec,
        out_shape=jax.ShapeDtypeStruct((n_rows,) + slab, hp.dtype),
        compiler_params=_params("arbitrary"),
    )(block_exp, n_used, _tile_major(dest, tm), hp)


def _expert_kernel(be_ref, nu_ref, xs_ref, wgu_ref, bgu_ref, wd_ref, bd_ref, ys_ref, wgu_bf, wd_bf):
    i = pl.program_id(0)
    active = i < nu_ref[0]

    @pl.when(active & ((i == 0) | (be_ref[i] != be_ref[jnp.maximum(i - 1, 0)])))
    def _():
        wgu_bf[...] = wgu_ref[0].astype(BF16)
        wd_bf[...] = wd_ref[0].astype(BF16)

    @pl.when(active)
    def _():
        gu = jnp.dot(xs_ref[...].astype(BF16), wgu_bf[...], preferred_element_type=F32) + bgu_ref[0]
        de = gu.shape[1] // 2
        gate = jnp.minimum(gu[:, :de], SWIGLU_LIMIT)
        up = jnp.clip(gu[:, de:], -SWIGLU_LIMIT, SWIGLU_LIMIT)
        act = gate * jax.nn.sigmoid(SWIGLU_ALPHA * gate) * (up + 1.0)
        ys_ref[...] = jnp.dot(act.astype(BF16), wd_bf[...], preferred_element_type=F32) + bd_ref[0]

    @pl.when(i >= nu_ref[0])
    def _():
        ys_ref[...] = jnp.zeros_like(ys_ref)


def _experts(block_exp, n_used, xs, w_gate_up, b_gate_up, w_down, b_down):
    n_rows = xs.shape[0]
    ne, d, de2 = w_gate_up.shape
    tb = EXPERT_ROWS

    def row_block(i, be, nu):
        return (jnp.minimum(i, nu[0] - 1), 0)

    def expert_block(i, be, nu):
        return (be[jnp.minimum(i, nu[0] - 1)], 0, 0)

    grid_spec = pltpu.PrefetchScalarGridSpec(
        num_scalar_prefetch=2,
        grid=(n_rows // tb,),
        in_specs=[pl.BlockSpec((tb,) + xs.shape[1:], row_block),
                  pl.BlockSpec((1, d, de2), expert_block),
                  pl.BlockSpec((1, 1, de2), expert_block),
                  pl.BlockSpec((1, de2 // 2, d), expert_block),
                  pl.BlockSpec((1, 1, d), expert_block)],
        out_specs=pl.BlockSpec((tb, d), lambda i, be, nu: (i, 0)),
        scratch_shapes=[pltpu.VMEM((d, de2), BF16), pltpu.VMEM((de2 // 2, d), BF16)],
    )
    return pl.pallas_call(
        _expert_kernel,
        grid_spec=grid_spec,
        out_shape=jax.ShapeDtypeStruct((n_rows, d), F32),
        compiler_params=_params("arbitrary"),
    )(block_exp, n_used, xs, w_gate_up, b_gate_up.reshape(ne, 1, de2), w_down, b_down.reshape(ne, 1, d))


def _combine_kernel(dest_ref, dest_next_ref, wgt_ref, x1_ref, ys_ref, out_ref, buf_ref, sems):
    i = pl.program_id(0)
    n = pl.num_programs(0)
    tm = x1_ref.shape[0]
    slot = i % 2

    def gather(d_ref, into):
        def issue(t, c):
            for k in range(TOP_K):
                _row_copy(ys_ref, d_ref[k * tm + t], buf_ref.at[into, k], t, sems.at[into]).start(priority=k % 2)
            return c

        lax.fori_loop(0, tm, issue, 0, unroll=ROW_DMA_UNROLL)

    @pl.when(i == 0)
    def _():
        gather(dest_ref, 0)

    @pl.when(i + 1 < n)
    def _():
        gather(dest_next_ref, 1 - slot)

    for k in range(TOP_K):
        _rows_wait(buf_ref.at[slot, k], tm, sems.at[slot])
    acc = x1_ref[...]
    for k in range(TOP_K):
        acc = acc + wgt_ref[:, k:k + 1] * buf_ref[slot, k]
    out_ref[...] = acc


def _combine(dest, wgt_t, x1, ys):
    t, d = x1.shape
    tm = ROUTE_TILE
    n = t // tm
    return pl.pallas_call(
        _combine_kernel,
        grid=(n,),
        in_specs=[pl.BlockSpec((TOP_K * tm,), lambda i: (i,), memory_space=pltpu.SMEM),
                  pl.BlockSpec((TOP_K * tm,), lambda i: (jnp.minimum(i + 1, n - 1),), memory_space=pltpu.SMEM),
                  pl.BlockSpec((tm, TOP_K), lambda i: (i, 0)),
                  pl.BlockSpec((tm, d), lambda i: (i, 0)),
                  pl.BlockSpec(memory_space=pl.ANY)],
        out_specs=pl.BlockSpec((tm, d), lambda i: (i, 0)),
        out_shape=jax.ShapeDtypeStruct((t, d), F32),
        scratch_shapes=[pltpu.VMEM((2, TOP_K, tm) + ys.shape[1:], F32), pltpu.SemaphoreType.DMA((2,))],
        compiler_params=_params("arbitrary"),
    )(_tile_major(dest, tm), _tile_major(dest, tm), wgt_t, x1, ys)


def _moe(x1, hp, idx, rank, wgt, counts, w_gate_up, b_gate_up, w_down, b_down):
    t = x1.shape[0]
    tb = EXPERT_ROWS
    ne = w_gate_up.shape[0]
    n_blocks = -(-(t * TOP_K) // tb) + ne
    padded = ((counts + tb - 1) // tb) * tb
    pad_end = jnp.cumsum(padded)
    pad_start = pad_end - padded
    dest = rank + jnp.sum(jnp.where(idx[None] == jnp.arange(ne, dtype=jnp.int32)[:, None, None],
                                    pad_start.astype(jnp.int32)[:, None, None], 0), axis=0)
    block_start = jnp.arange(n_blocks, dtype=jnp.int32) * tb
    block_exp = jnp.minimum(jnp.sum((pad_end[None, :] <= block_start[:, None]).astype(jnp.int32), axis=1), ne - 1)
    n_used = (pad_end[-1:] // tb).astype(jnp.int32)
    xs = _dispatch(block_exp, n_used, dest, hp, n_blocks * tb)
    ys = _experts(block_exp, n_used, xs, w_gate_up, b_gate_up, w_down, b_down)
    return _combine(dest, wgt.T, x1, ys)


def kernel(x, norm1, w_in, q_norm, k_norm, lam_re, lam_im, log_dt, b_re, b_im, c_re, c_im, d_skip, w_glu, b_glu, w_attn_branch, w_ssm_branch, w_out, norm2, w_router, b_router, w_gate_up, b_gate_up, w_down, b_down):
    bsz, seq, d = x.shape
    t = bsz * seq
    x2 = x.reshape(t, d)
    for l in range(norm1.shape[0]):
        qkv0, qkv1, qkv2, u, gates = _in_proj(x2, norm1[l], w_in[l], q_norm[l], k_norm[l], seq)
        attn = [_attention(qkv0.reshape(bsz, seq, -1)), _attention(qkv1), _attention(qkv2)]
        z = _ssm(u, bsz, lam_re[l], lam_im[l], log_dt[l], b_re[l], b_im[l], c_re[l], c_im[l], d_skip[l])
        x1, hp, idx, rank, wgt, counts = _post_mix(
            x2, seq, attn, z, gates, w_attn_branch[l], w_glu[l], b_glu[l], w_ssm_branch[l], w_out[l], norm2[l],
            w_router[l], b_router[l])
        x2 = _moe(x1, hp, idx, rank, wgt, counts[:, 0], w_gate_up[l], b_gate_up[l], w_down[l], b_down[l])
    return x2.reshape(bsz, seq, d)
```

```python
import functools
import math

import numpy as np
import jax
import jax.numpy as jnp
from jax import lax
from jax.experimental import pallas as pl
from jax.experimental.pallas import tpu as pltpu

F32 = jnp.float32
BF16 = jnp.bfloat16

HEAD_DIM = 64
HEADS_PER_GROUP = 4
ATTN_GROUPS = ((128, 1), (512, 4), (2048, 16))
GROUP_WIDTH = HEADS_PER_GROUP * HEAD_DIM
ATTN_WIDTH = GROUP_WIDTH * len(ATTN_GROUPS)
BAND = 64
ROPE_THETA = 10000.0
SSM_GROUP = 16
SSM_STATE = 64
N_EXPERTS = 32
TOP_K = 4
SWIGLU_ALPHA = 1.702
SWIGLU_LIMIT = 7.0
NORM_EPS = 1e-6
MASK_VALUE = -1e30

ROW_TILE = 512
ATTN_Q_TILE = 1024
ATTN_Q_SUB = 128
SSM_CHUNK = 64
SSM_POWER_ROWS = -(-(SSM_CHUNK + 1) // 8) * 8
DISPATCH_TILE = 1024
ROUTE_TILE = 512
ROW_DMA_UNROLL = 8
EXPERT_ROWS = 512
VMEM_LIMIT = 56 * 1024 * 1024
LANES = 128


def _params(*sem):
    return pltpu.CompilerParams(dimension_semantics=sem, vmem_limit_bytes=VMEM_LIMIT)


def _const_spec(shape):
    return pl.BlockSpec(shape, lambda *_: (0,) * len(shape), pipeline_mode=pl.Buffered(1))


def _permute_qk(a):
    lead = a.shape[:-1]
    n = len(lead)
    a = a.reshape(*lead, len(ATTN_GROUPS), HEADS_PER_GROUP, 2, HEAD_DIM // 2)
    return a.transpose(*range(n), n, n + 2, n + 1, n + 3).reshape(*lead, ATTN_WIDTH)


def _permute_group_lanes(piece):
    hd, hh = HEAD_DIM, HEAD_DIM // 2
    firsts = [piece[:, h * hd:h * hd + hh] for h in range(HEADS_PER_GROUP)]
    seconds = [piece[:, h * hd + hh:(h + 1) * hd] for h in range(HEADS_PER_GROUP)]
    return jnp.concatenate(firsts + seconds, axis=-1)


def _in_proj_kernel(x_ref, g1_ref, wf_ref, gq_ref, gk_ref, cos_ref, sin_ref, seg_ref,
                    qkv0_ref, qkv1_ref, qkv2_ref, u_ref, gate_ref, w_ref, scr_ref):
    @pl.when(pl.program_id(0) == 0)
    def _():
        rows_per_step = 128

        def cast_rows(i, _):
            rows = pl.ds(pl.multiple_of(i * rows_per_step, rows_per_step), rows_per_step)
            for c0 in range(0, 2 * ATTN_WIDTH, GROUP_WIDTH):
                w_ref[rows, c0:c0 + GROUP_WIDTH] = _permute_group_lanes(wf_ref[rows, c0:c0 + GROUP_WIDTH]).astype(BF16)
            for c0 in range(2 * ATTN_WIDTH, wf_ref.shape[1], GROUP_WIDTH):
                w_ref[rows, c0:c0 + GROUP_WIDTH] = wf_ref[rows, c0:c0 + GROUP_WIDTH].astype(BF16)
            return 0

        lax.fori_loop(0, wf_ref.shape[0] // rows_per_step, cast_rows, 0)

    x = x_ref[...]
    ms = jnp.mean(x * x, axis=-1, keepdims=True)
    h = (x * lax.rsqrt(ms + NORM_EPS) * g1_ref[...]).astype(BF16)
    cos = cos_ref[...]
    sin = sin_ref[...]
    seg = seg_ref[...]
    half = GROUP_WIDTH // 2
    tm = x.shape[0]

    def proj(c0, width):
        return jnp.dot(h, w_ref[:, c0:c0 + width], preferred_element_type=F32)

    def sum_squares(t):
        return t[:, :half] * t[:, :half] + t[:, half:] * t[:, half:]

    def norm_rope(t, tot, gain_ref, g):
        t1, t2 = t[:, :half], t[:, half:]
        r = lax.rsqrt(tot * (1.0 / HEAD_DIM) + NORM_EPS)
        a = t1 * r * gain_ref[:, g * GROUP_WIDTH:g * GROUP_WIDTH + half]
        b = t2 * r * gain_ref[:, g * GROUP_WIDTH + half:(g + 1) * GROUP_WIDTH]
        return a * cos - b * sin, b * cos + a * sin

    outs = (qkv0_ref, qkv1_ref, qkv2_ref)
    for g, (_, dil) in enumerate(ATTN_GROUPS):
        tq = proj(g * GROUP_WIDTH, GROUP_WIDTH)
        tk = proj(ATTN_WIDTH + g * GROUP_WIDTH, GROUP_WIDTH)
        ss = jnp.concatenate([sum_squares(tq), sum_squares(tk)], axis=-1).astype(BF16)
        tot = jnp.dot(ss, seg, preferred_element_type=F32)
        q1, q2 = norm_rope(tq, tot[:, :half], gq_ref, g)
        k1, k2 = norm_rope(tk, tot[:, half:], gk_ref, g)
        v = proj(2 * ATTN_WIDTH + g * GROUP_WIDTH, GROUP_WIDTH)
        qkv = jnp.concatenate([q1, q2, k1, k2, v], axis=-1)
        if dil == 1:
            outs[g][...] = qkv.astype(BF16)
        else:
            for c in range(qkv.shape[1] // LANES):
                scr_ref[c] = qkv[:, c * LANES:(c + 1) * LANES]
            for r in range(dil):
                for c in range(qkv.shape[1] // LANES):
                    outs[g][r, :, c * LANES:(c + 1) * LANES] = (
                        scr_ref[c, pl.ds(r, tm // dil, stride=dil), :].astype(BF16))
    o_u = 3 * ATTN_WIDTH
    u_w = u_ref.shape[-1]
    u_ref[...] = proj(o_u, u_w).astype(BF16)
    gate_w = gate_ref.shape[-1]
    for c in range(0, gate_w, 1024):
        gate_ref[:, c:c + 1024] = jax.nn.sigmoid(proj(o_u + u_w + c, 1024)).astype(BF16)


def _in_proj(x2, norm1, w_in, q_norm, k_norm, seq):
    t, d = x2.shape
    tm = ROW_TILE
    in_w = w_in.shape[1]
    ssm_w = d // 2
    gate_w = in_w - 3 * ATTN_WIDTH - ssm_w
    gq = (_permute_qk(q_norm.reshape(-1)) * (HEAD_DIM ** -0.5)).reshape(1, ATTN_WIDTH)
    gk = _permute_qk(k_norm.reshape(-1)).reshape(1, ATTN_WIDTH)
    inv_freq = ROPE_THETA ** (-jnp.arange(0, HEAD_DIM, 2, dtype=F32) / HEAD_DIM)
    ang = jnp.arange(seq, dtype=F32)[:, None] * inv_freq[None, :]
    cos = jnp.tile(jnp.cos(ang), (1, HEADS_PER_GROUP))
    sin = jnp.tile(jnp.sin(ang), (1, HEADS_PER_GROUP))
    lane = np.arange(GROUP_WIDTH)
    seg = jnp.asarray(lane[:, None] // (HEAD_DIM // 2) == lane[None, :] // (HEAD_DIM // 2), BF16)
    bsz = t // seq
    tiles_per_seq = seq // tm
    qkv_w = 3 * GROUP_WIDTH
    out_shape = [jax.ShapeDtypeStruct((t, qkv_w), BF16)]
    out_specs = [pl.BlockSpec((tm, qkv_w), lambda i: (i, 0))]
    for _, dil in ATTN_GROUPS[1:]:
        out_shape.append(jax.ShapeDtypeStruct((bsz * dil, seq // dil, qkv_w), BF16))
        out_specs.append(pl.BlockSpec((dil, tm // dil, qkv_w),
                                      lambda i: (i // tiles_per_seq, i % tiles_per_seq, 0)))
    out_shape += [jax.ShapeDtypeStruct((t, ssm_w), BF16), jax.ShapeDtypeStruct((t, gate_w), BF16)]
    out_specs += [pl.BlockSpec((tm, ssm_w), lambda i: (i, 0)), pl.BlockSpec((tm, gate_w), lambda i: (i, 0))]
    return pl.pallas_call(
        _in_proj_kernel,
        grid=(t // tm,),
        in_specs=[
            pl.BlockSpec((tm, d), lambda i: (i, 0)),
            _const_spec((1, d)),
            _const_spec((d, in_w)),
            _const_spec((1, ATTN_WIDTH)),
            _const_spec((1, ATTN_WIDTH)),
            pl.BlockSpec((tm, GROUP_WIDTH // 2), lambda i: (i % tiles_per_seq, 0)),
            pl.BlockSpec((tm, GROUP_WIDTH // 2), lambda i: (i % tiles_per_seq, 0)),
            _const_spec((GROUP_WIDTH, GROUP_WIDTH)),
        ],
        out_specs=out_specs,
        out_shape=out_shape,
        scratch_shapes=[pltpu.VMEM((d, in_w), BF16), pltpu.VMEM((qkv_w // LANES, tm, LANES), F32)],
        compiler_params=_params("arbitrary"),
    )(x2, norm1.reshape(1, d), w_in, gq, gk, cos, sin, seg)


def _attn_kernel(c_ref, kl_ref, vl_ref, kr_ref, vr_ref, o_ref, lse_ref, kcat_ref, vcat_ref, *, length):
    tq = c_ref.shape[1]
    gw = GROUP_WIDTH
    j = pl.program_id(1)
    kcat_ref[0:BAND] = kl_ref[0]
    kcat_ref[BAND:BAND + tq] = c_ref[0, :, gw:2 * gw]
    kcat_ref[BAND + tq:] = kr_ref[0]
    vcat_ref[0:BAND] = vl_ref[0]
    vcat_ref[BAND:BAND + tq] = c_ref[0, :, 2 * gw:]
    vcat_ref[BAND + tq:] = vr_ref[0]
    sub = min(ATTN_Q_SUB, tq)
    nk = sub + 2 * BAND
    nh = HEADS_PER_GROUP
    row = lax.broadcasted_iota(jnp.int32, (nh * sub, nk), 0) & (sub - 1)
    col = lax.broadcasted_iota(jnp.int32, (nh * sub, nk), 1)
    in_band = jnp.abs(col - BAND - row) <= BAND
    q_lane = lax.broadcasted_iota(jnp.int32, (1, gw), 1)
    q_head = (q_lane % (gw // 2)) // (HEAD_DIM // 2)
    v_head = q_lane // HEAD_DIM
    for s in range(tq // sub):
        q_s = c_ref[0, s * sub:(s + 1) * sub, 0:gw]
        k_s = kcat_ref[s * sub:s * sub + nk]
        v_s = vcat_ref[s * sub:s * sub + nk]
        kpos = j * tq + s * sub - BAND + col
        valid = in_band & (kpos >= 0) & (kpos < length)
        q4 = jnp.concatenate([jnp.where(q_head == hh, q_s, jnp.zeros_like(q_s)) for hh in range(nh)], axis=0)
        sc = lax.dot_general(q4, k_s, (((1,), (1,)), ((), ())), preferred_element_type=F32)
        sc = jnp.where(valid, sc, MASK_VALUE)
        m = jnp.max(sc, axis=-1, keepdims=True)
        p = jnp.exp(sc - m)
        den = jnp.sum(p, axis=-1, keepdims=True)
        o4 = jnp.dot(p.astype(BF16), v_s, preferred_element_type=F32) * (1.0 / den)
        l4 = m + jnp.log(den)
        out = o4[0:sub]
        lse = jnp.broadcast_to(l4[0:sub], (sub, gw))
        for hh in range(1, nh):
            out = jnp.where(v_head == hh, o4[hh * sub:(hh + 1) * sub], out)
            lse = jnp.where(v_head == hh, l4[hh * sub:(hh + 1) * sub], lse)
        o_ref[0, s * sub:(s + 1) * sub] = out.astype(BF16)
        lse_ref[0, s * sub:(s + 1) * sub] = lse


def _attention(qkv):
    n, length, _ = qkv.shape
    tq = min(ATTN_Q_TILE, length)
    gw = GROUP_WIDTH
    hb = tq // BAND
    last = length // BAND - 1

    def left(lane_block):
        return pl.BlockSpec((1, BAND, gw), lambda b, j: (b, jnp.maximum(j * hb - 1, 0), lane_block))

    def right(lane_block):
        return pl.BlockSpec((1, BAND, gw), lambda b, j: (b, jnp.minimum((j + 1) * hb, last), lane_block))

    return pl.pallas_call(
        functools.partial(_attn_kernel, length=length),
        grid=(n, length // tq),
        in_specs=[pl.BlockSpec((1, tq, 3 * gw), lambda b, j: (b, j, 0)), left(1), left(2), right(1), right(2)],
        out_specs=[pl.BlockSpec((1, tq, gw), lambda b, j: (b, j, 0)),
                   pl.BlockSpec((1, tq, gw), lambda b, j: (b, j, 0))],
        out_shape=[jax.ShapeDtypeStruct((n, length, gw), BF16), jax.ShapeDtypeStruct((n, length, gw), F32)],
        scratch_shapes=[pltpu.VMEM((tq + 2 * BAND, gw), BF16), pltpu.VMEM((tq + 2 * BAND, gw), BF16)],
        compiler_params=_params("parallel", "parallel"),
    )(qkv, qkv, qkv, qkv, qkv)


def _ssm_operators(lam_re, lam_im, log_dt, b_re, b_im, c_re, c_im, d_skip):
    L = SSM_CHUNK
    G, P = lam_re.shape[1:]
    C = d_skip.shape[-1]
    hp = lax.Precision.HIGHEST
    n = jnp.arange(L + 1, dtype=F32)[:, None, None]
    pad = SSM_POWER_ROWS - (L + 1)
    kern, pw, fbt, ct, consts = [], [], [], [], []
    for direction in range(2):
        lr, li = lam_re[direction].astype(F32), lam_im[direction].astype(F32)
        dt = jnp.exp(log_dt[direction].astype(F32))[:, None]
        mag = jnp.exp(lr * dt)
        ab_re, ab_im = mag * jnp.cos(li * dt), mag * jnp.sin(li * dt)
        den = lr * lr + li * li
        nr = ab_re - 1.0
        f_re = (nr * lr + ab_im * li) / den
        f_im = (ab_im * lr - nr * li) / den
        pmag = jnp.exp(n * (lr * dt)[None])
        p_re, p_im = pmag * jnp.cos(n * (li * dt)[None]), pmag * jnp.sin(n * (li * dt)[None])
        br, bi = b_re[direction].astype(F32), b_im[direction].astype(F32)
        fb_re = f_re[..., None] * br - f_im[..., None] * bi
        fb_im = f_re[..., None] * bi + f_im[..., None] * br
        cr, ci = c_re[direction].astype(F32), c_im[direction].astype(F32)
        pr, pi = p_re[:L].transpose(1, 2, 0)[..., None], p_im[:L].transpose(1, 2, 0)[..., None]
        crt, cit = cr.transpose(0, 2, 1)[:, :, None, :], ci.transpose(0, 2, 1)[:, :, None, :]
        e_re = (pr * crt - pi * cit).reshape(G, P, L * C)
        e_im = (pr * cit + pi * crt).reshape(G, P, L * C)
        kern.append((jnp.einsum('gpc,gpx->gcx', fb_re, e_re, precision=hp)
                     - jnp.einsum('gpc,gpx->gcx', fb_im, e_im, precision=hp)).reshape(G, C, L, C))
        for p in (p_re, p_im):
            p = jnp.pad(p.transpose(1, 0, 2), ((0, 0), (0, pad), (0, 0)))
            pw.append(jnp.concatenate([p, p, p, p], axis=-1))
        tr, ti = fb_re.transpose(0, 2, 1), fb_im.transpose(0, 2, 1)
        fbt.append(jnp.concatenate([tr, ti, ti, tr], axis=-1))
        fbt.append(jnp.concatenate([-ti, tr, tr, -ti], axis=-1))
        ct.append(jnp.concatenate([cr, -ci], axis=-1))
        ct.append(jnp.concatenate([-ci, -cr], axis=-1))
        a_re, a_im = p_re[L], p_im[L]
        consts.append(jnp.concatenate([a_re, a_re, a_re, a_re], axis=-1))
        consts.append(jnp.concatenate([-a_im, a_im, a_im, -a_im], axis=-1))
    kf, kb = kern
    zero = kf[:, :, 0] + kb[:, :, 0] + jnp.eye(C, dtype=F32)[None] * d_skip.astype(F32)[:, None, :]
    kfull = jnp.concatenate([kb[:, :, :0:-1], zero[:, :, None], kf[:, :, 1:], jnp.zeros_like(zero)[:, :, None]],
                            axis=2)
    flat = kfull.astype(BF16).reshape(G, C, 2 * L * C)
    return (flat, jnp.stack(pw, axis=1), jnp.stack(fbt, axis=1), jnp.stack(ct, axis=1),
            jnp.stack(consts + consts, axis=1))


def _ssm_kernel(u_ref, flat_ref, pw_ref, fbt_ref, ct_ref, cst_ref, z_ref,
                toe_ref, wst_ref, vt_ref, f_ref, s_ref, *, bsz):
    u = u_ref[0]
    nc = u.shape[0]
    sw = cst_ref.shape[-1]
    hw = sw // 2
    C = fbt_ref.shape[2]
    L = toe_ref.shape[0] // C
    lc = L * C

    per_tile = LANES // C
    span = (2 * L - per_tile) * C
    for b in range(per_tile):
        shifted = flat_ref[0, :, (per_tile - 1 - b) * C:(per_tile - 1 - b) * C + span]
        for a in range(L // per_tile):
            t_in = a * per_tile + b
            off = (L // per_tile - 1 - a) * LANES
            toe_ref[t_in * C:(t_in + 1) * C, :] = shifted[:, off:off + lc]

    fa_f, fb_f, fa_b, fb_b = fbt_ref[0, 0], fbt_ref[0, 1], fbt_ref[0, 2], fbt_ref[0, 3]
    ca_f, cb_f, ca_b, cb_b = ct_ref[0, 0], ct_ref[0, 1], ct_ref[0, 2], ct_ref[0, 3]

    def power(k, n):
        return pw_ref[0, k, n:n + 1, :]

    for t in range(L):
        rows = slice(t * C, (t + 1) * C)
        wst_ref[rows, 0:sw] = (fa_f * power(0, L - 1 - t) + fb_f * power(1, L - 1 - t)).astype(BF16)
        wst_ref[rows, sw:2 * sw] = (fa_b * power(2, t) + fb_b * power(3, t)).astype(BF16)
        vt_ref[rows, 0:hw] = (ca_f * power(0, t + 1)[:, :hw] + cb_f * power(1, t + 1)[:, :hw]).astype(BF16)
        vt_ref[rows, hw:sw] = (ca_b * power(2, L - t)[:, :hw] + cb_b * power(3, L - t)[:, :hw]).astype(BF16)

    f_ref[...] = jnp.dot(u, wst_ref[...], preferred_element_type=F32)
    cst = cst_ref[0]
    a1f, a2f, a1b, a2b = cst[0:1], cst[1:2], cst[2:3], cst[3:4]
    per = nc // bsz

    def swap(v):
        return jnp.concatenate([v[:, hw:], v[:, :hw]], axis=-1)

    rows = 8

    def step(i, carry):
        new = []
        for b in range(bsz):
            sf, sb = carry[2 * b], carry[2 * b + 1]
            base_f = pl.multiple_of(b * per + i * rows, rows)
            base_b = pl.multiple_of(b * per + per - rows - i * rows, rows)
            f_blk = f_ref[pl.ds(base_f, rows), 0:sw]
            b_blk = f_ref[pl.ds(base_b, rows), sw:2 * sw]
            f_rows, b_rows = [], []
            for r in range(rows):
                f_rows.append(sf[:, :hw])
                sf = a1f * sf + a2f * swap(sf) + f_blk[r:r + 1]
                b_rows.append(sb[:, :hw])
                sb = a1b * sb + a2b * swap(sb) + b_blk[rows - 1 - r:rows - r]
            s_ref[pl.ds(base_f, rows), 0:hw] = jnp.concatenate(f_rows, axis=0)
            s_ref[pl.ds(base_b, rows), hw:sw] = jnp.concatenate(b_rows[::-1], axis=0)
            new += [sf, sb]
        return tuple(new)

    lax.fori_loop(0, per // rows, step, tuple(jnp.zeros((1, sw), F32) for _ in range(2 * bsz)))
    y = jnp.dot(u, toe_ref[...], preferred_element_type=F32)
    y = y + lax.dot_general(s_ref[...].astype(BF16), vt_ref[...], (((1,), (1,)), ((), ())),
                            preferred_element_type=F32)
    z_ref[0] = jax.nn.gelu(y).astype(BF16)


def _ssm(u, bsz, lam_re, lam_im, log_dt, b_re, b_im, c_re, c_im, d_skip):
    t, width = u.shape
    L, C = SSM_CHUNK, d_skip.shape[-1]
    G = width // C
    nc = t // L
    flat, pw, fbt, ct, consts = _ssm_operators(lam_re, lam_im, log_dt, b_re, b_im, c_re, c_im, d_skip)
    sw = consts.shape[-1]
    ug = u.reshape(nc, L, G, C).transpose(2, 0, 1, 3).reshape(G, nc, L * C)

    def group_spec(a):
        return pl.BlockSpec((1,) + a.shape[1:], lambda g: (g,) + (0,) * (a.ndim - 1))

    z = pl.pallas_call(
        functools.partial(_ssm_kernel, bsz=bsz),
        grid=(G,),
        in_specs=[group_spec(a) for a in (ug, flat, pw, fbt, ct, consts)],
        out_specs=pl.BlockSpec((1, nc, L * C), lambda g: (g, 0, 0)),
        out_shape=jax.ShapeDtypeStruct((G, nc, L * C), BF16),
        scratch_shapes=[pltpu.VMEM((L * C, L * C), BF16), pltpu.VMEM((L * C, 2 * sw), BF16),
                        pltpu.VMEM((L * C, sw), BF16), pltpu.VMEM((nc, 2 * sw), F32), pltpu.VMEM((nc, sw), F32)],
        compiler_params=_params("parallel"),
    )(ug, flat, pw, fbt, ct, consts)
    return z.reshape(G, nc, L, C).transpose(1, 2, 0, 3).reshape(t, width)


def _post_kernel(o0_ref, l0_ref, o1_ref, l1_ref, o2_ref, l2_ref, z_ref, gate_ref, x_ref,
                 wa_ref, wglu_ref, bglu_ref, ws_ref, wo_ref, g2_ref, wr_ref, wrlo_ref, br_ref, tri_ref,
                 x1_ref, hp_ref, idx_ref, rank_ref, wgt_ref, cnt_ref, o_scr, l_scr, carry_ref):
    tm, d = x_ref.shape

    @pl.when(pl.program_id(0) == 0)
    def _():
        carry_ref[...] = jnp.zeros_like(carry_ref)

    outs = [o0_ref[...].astype(F32)]
    lses = [l0_ref[...]]
    for (o_ref, l_ref), (_, dil) in zip(((o1_ref, l1_ref), (o2_ref, l2_ref)), ATTN_GROUPS[1:]):
        nlb = o_scr.shape[0]
        for r in range(dil):
            for c in range(nlb):
                o_scr[c, pl.ds(r, tm // dil, stride=dil), :] = o_ref[r, :, c * LANES:(c + 1) * LANES].astype(F32)
                l_scr[c, pl.ds(r, tm // dil, stride=dil), :] = l_ref[r, :, c * LANES:(c + 1) * LANES]
        outs.append(jnp.concatenate([o_scr[c] for c in range(nlb)], axis=-1))
        lses.append(jnp.concatenate([l_scr[c] for c in range(nlb)], axis=-1))
    top = jnp.maximum(jnp.maximum(lses[0], lses[1]), lses[2])
    es = [jnp.exp(l - top) for l in lses]
    y_attn = (es[0] * outs[0] + es[1] * outs[1] + es[2] * outs[2]) / (es[0] + es[1] + es[2])
    ya = jnp.dot(y_attn.astype(BF16), wa_ref[...], preferred_element_type=F32)
    z = z_ref[...]
    glu = jnp.dot(z, wglu_ref[...], preferred_element_type=F32) + bglu_ref[...]
    zz = z.astype(F32) * jax.nn.sigmoid(glu)
    ys = jnp.dot(zz.astype(BF16), ws_ref[...], preferred_element_type=F32)
    mixed = gate_ref[:, :d].astype(F32) * ya + gate_ref[:, d:].astype(F32) * ys
    x1 = x_ref[...] + jnp.dot(mixed.astype(BF16), wo_ref[...], preferred_element_type=F32)
    x1_ref[...] = x1
    ms = jnp.mean(x1 * x1, axis=-1, keepdims=True)
    h2 = x1 * lax.rsqrt(ms + NORM_EPS) * g2_ref[...]
    hb = h2.astype(BF16)
    hp_ref[...] = h2.reshape(hp_ref.shape)

    h_lo = (h2 - hb.astype(F32)).astype(BF16)
    nt = (((1,), (1,)), ((), ()))
    logits = (lax.dot_general(wr_ref[...], hb, nt, preferred_element_type=F32)
              + lax.dot_general(wrlo_ref[...], hb, nt, preferred_element_type=F32)
              + lax.dot_general(wr_ref[...], h_lo, nt, preferred_element_type=F32) + br_ref[...])
    ne = logits.shape[0]
    e_iota = lax.broadcasted_iota(jnp.int32, logits.shape, 0)
    vals, idxs = [], []
    work = logits
    for _ in range(TOP_K):
        m = jnp.max(work, axis=0, keepdims=True)
        i = jnp.min(jnp.where(work == m, e_iota, ne), axis=0, keepdims=True)
        vals.append(m)
        idxs.append(i)
        work = jnp.where(e_iota == i, -jnp.inf, work)
    ex = [jnp.exp(v - vals[0]) for v in vals]
    tot = ex[0] + ex[1] + ex[2] + ex[3]
    onehot = jnp.zeros(logits.shape, F32)
    for i in idxs:
        onehot = onehot + (e_iota == i).astype(F32)
    before = jnp.dot(onehot.astype(BF16), tri_ref[...], preferred_element_type=F32) + carry_ref[:, 0:1]
    for k in range(TOP_K):
        idx_ref[k:k + 1, :] = idxs[k]
        rank_ref[k:k + 1, :] = jnp.sum(jnp.where(e_iota == idxs[k], before, 0.0), axis=0,
                                       keepdims=True).astype(jnp.int32)
        wgt_ref[k:k + 1, :] = ex[k] / tot
    carry_ref[...] = carry_ref[...] + jnp.sum(onehot, axis=1, keepdims=True)
    cnt_ref[...] = carry_ref[...].astype(jnp.int32)


def _post_mix(x2, seq, attn, z, gates, w_attn_branch, w_glu, b_glu, w_ssm_branch, w_out, norm2, w_router, b_router):
    t, d = x2.shape
    tm = ROW_TILE
    gw = GROUP_WIDTH
    tiles_per_seq = seq // tm
    ne = w_router.shape[1]
    ssm_w = z.shape[1]
    in_specs, args = [], []
    for (o, lse), (_, dil) in zip(attn, ATTN_GROUPS):
        if dil == 1:
            spec = pl.BlockSpec((tm, gw), lambda i: (i, 0))
            o, lse = o.reshape(t, gw), lse.reshape(t, gw)
        else:
            spec = pl.BlockSpec((dil, tm // dil, gw), lambda i: (i // tiles_per_seq, i % tiles_per_seq, 0))
        in_specs += [spec, spec]
        args += [o, lse]
    tri = jnp.asarray(np.arange(tm)[:, None] < np.arange(tm)[None, :], BF16)
    in_specs += [pl.BlockSpec((tm, ssm_w), lambda i: (i, 0)),
                 pl.BlockSpec((tm, 2 * d), lambda i: (i, 0)),
                 pl.BlockSpec((tm, d), lambda i: (i, 0)),
                 _const_spec((gw, d)), _const_spec((ssm_w, ssm_w)), _const_spec((1, ssm_w)),
                 _const_spec((ssm_w, d)), _const_spec((d, d)), _const_spec((1, d)),
                 _const_spec((ne, d)), _const_spec((ne, d)), _const_spec((ne, 1)), _const_spec((tm, tm))]
    wr_t = w_router.T.astype(F32)
    wr_hi = wr_t.astype(BF16)
    wr_lo = (wr_t - wr_hi.astype(F32)).astype(BF16)
    args += [z, gates, x2, w_attn_branch.astype(BF16), w_glu.astype(BF16), b_glu.reshape(1, ssm_w),
             w_ssm_branch.astype(BF16), w_out.astype(BF16), norm2.reshape(1, d),
             wr_hi, wr_lo, b_router.reshape(ne, 1), tri]
    tok_spec = pl.BlockSpec((TOP_K, tm), lambda i: (0, i))
    return pl.pallas_call(
        _post_kernel,
        grid=(t // tm,),
        in_specs=in_specs,
        out_specs=[pl.BlockSpec((tm, d), lambda i: (i, 0)),
                   pl.BlockSpec((tm, d // LANES, LANES), lambda i: (i, 0, 0)),
                   tok_spec, tok_spec, tok_spec, pl.BlockSpec((ne, LANES), lambda i: (0, 0))],
        out_shape=[jax.ShapeDtypeStruct((t, d), F32), jax.ShapeDtypeStruct((t, d // LANES, LANES), F32),
                   jax.ShapeDtypeStruct((TOP_K, t), jnp.int32), jax.ShapeDtypeStruct((TOP_K, t), jnp.int32),
                   jax.ShapeDtypeStruct((TOP_K, t), F32), jax.ShapeDtypeStruct((ne, LANES), jnp.int32)],
        scratch_shapes=[pltpu.VMEM((gw // LANES, tm, LANES), F32), pltpu.VMEM((gw // LANES, tm, LANES), F32),
                        pltpu.VMEM((ne, LANES), F32)],
        compiler_params=_params("arbitrary"),
    )(*args)


def _row_copy(src_ref, src_row, dst_ref, dst_row, sem):
    return pltpu.make_async_copy(src_ref.at[pl.ds(src_row, 1)], dst_ref.at[pl.ds(dst_row, 1)], sem)


def _tile_major(dest, tm):
    k, t = dest.shape
    return dest.reshape(k, t // tm, tm).transpose(1, 0, 2).reshape(-1)


def _rows_wait(ref, n_rows, sem):
    pltpu.make_async_copy(ref.at[pl.ds(0, n_rows)], ref.at[pl.ds(0, n_rows)], sem).wait()


def _dispatch_kernel(be_ref, nu_ref, dest_ref, h_ref, xs_ref, zero_ref, sems):
    i = pl.program_id(0)
    tm = h_ref.shape[0]
    tb = zero_ref.shape[0]
    n_blocks = be_ref.shape[0]

    @pl.when(i == 0)
    def _():
        zero_ref[...] = jnp.zeros_like(zero_ref)

        def holds_padding(b):
            return (b >= nu_ref[0] - 1) | (be_ref[b] != be_ref[jnp.minimum(b + 1, n_blocks - 1)])

        def zero_copy(b):
            return pltpu.make_async_copy(zero_ref, xs_ref.at[pl.ds(pl.multiple_of(b * tb, tb), tb)], sems.at[1])

        def start(b, c):
            @pl.when(holds_padding(b))
            def _():
                zero_copy(b).start()
            return c

        def wait(b, c):
            @pl.when(holds_padding(b))
            def _():
                zero_copy(b).wait()
            return c

        lax.fori_loop(0, n_blocks, start, 0)
        lax.fori_loop(0, n_blocks, wait, 0)

    def issue(t, c):
        for k in range(TOP_K):
            _row_copy(h_ref, t, xs_ref, dest_ref[k * tm + t], sems.at[0]).start(priority=k % 2)
        return c

    lax.fori_loop(0, tm, issue, 0, unroll=ROW_DMA_UNROLL)
    _rows_wait(xs_ref, TOP_K * tm, sems.at[0])


def _dispatch(block_exp, n_used, dest, hp, n_rows):
    t = hp.shape[0]
    slab = hp.shape[1:]
    tm = DISPATCH_TILE
    grid_spec = pltpu.PrefetchScalarGridSpec(
        num_scalar_prefetch=2,
        grid=(t // tm,),
        in_specs=[pl.BlockSpec((TOP_K * tm,), lambda i, be, nu: (i,), memory_space=pltpu.SMEM),
                  pl.BlockSpec((tm,) + slab, lambda i, be, nu: (i,) + (0,) * len(slab))],
        out_specs=pl.BlockSpec(memory_space=pl.ANY),
        scratch_shapes=[pltpu.VMEM((EXPERT_ROWS,) + slab, hp.dtype), pltpu.SemaphoreType.DMA((2,))],
    )
    return pl.pallas_call(
        _dispatch_kernel,
        grid_spec=grid_spec,
        out_shape=jax.ShapeDtypeStruct((n_rows,) + slab, hp.dtype),
        compiler_params=_params("arbitrary"),
    )(block_exp, n_used, _tile_major(dest, tm), hp)


def _expert_kernel(be_ref, nu_ref, xs_ref, wgu_ref, bgu_ref, wd_ref, bd_ref, ys_ref, wgu_bf, wd_bf):
    i = pl.program_id(0)
    active = i < nu_ref[0]

    @pl.when(active & ((i == 0) | (be_ref[i] != be_ref[jnp.maximum(i - 1, 0)])))
    def _():
        wgu_bf[...] = wgu_ref[0].astype(BF16)
        wd_bf[...] = wd_ref[0].astype(BF16)

    @pl.when(active)
    def _():
        x = xs_ref[...].reshape(xs_ref.shape[0], -1).astype(BF16)
        gu = jnp.dot(x, wgu_bf[...], preferred_element_type=F32) + bgu_ref[0]
        de = gu.shape[1] // 2
        gate = jnp.minimum(gu[:, :de], SWIGLU_LIMIT)
        up = jnp.clip(gu[:, de:], -SWIGLU_LIMIT, SWIGLU_LIMIT)
        act = gate * jax.nn.sigmoid(SWIGLU_ALPHA * gate) * (up + 1.0)
        y = jnp.dot(act.astype(BF16), wd_bf[...], preferred_element_type=F32) + bd_ref[0]
        ys_ref[...] = y.reshape(ys_ref.shape)

    @pl.when(i >= nu_ref[0])
    def _():
        ys_ref[...] = jnp.zeros_like(ys_ref)


def _experts(block_exp, n_used, xs, w_gate_up, b_gate_up, w_down, b_down):
    n_rows = xs.shape[0]
    ne, d, de2 = w_gate_up.shape
    tb = EXPERT_ROWS

    def row_block(i, be, nu):
        return (jnp.minimum(i, nu[0] - 1),) + (0,) * (xs.ndim - 1)

    def expert_block(i, be, nu):
        return (be[jnp.minimum(i, nu[0] - 1)], 0, 0)

    grid_spec = pltpu.PrefetchScalarGridSpec(
        num_scalar_prefetch=2,
        grid=(n_rows // tb,),
        in_specs=[pl.BlockSpec((tb,) + xs.shape[1:], row_block),
                  pl.BlockSpec((1, d, de2), expert_block),
                  pl.BlockSpec((1, 1, de2), expert_block),
                  pl.BlockSpec((1, de2 // 2, d), expert_block),
                  pl.BlockSpec((1, 1, d), expert_block)],
        out_specs=pl.BlockSpec((tb,) + xs.shape[1:], lambda i, be, nu: (i,) + (0,) * (xs.ndim - 1)),
        scratch_shapes=[pltpu.VMEM((d, de2), BF16), pltpu.VMEM((de2 // 2, d), BF16)],
    )
    return pl.pallas_call(
        _expert_kernel,
        grid_spec=grid_spec,
        out_shape=jax.ShapeDtypeStruct(xs.shape, F32),
        compiler_params=_params("arbitrary"),
    )(block_exp, n_used, xs, w_gate_up, b_gate_up.reshape(ne, 1, de2), w_down, b_down.reshape(ne, 1, d))


def _combine_kernel(dest_ref, dest_next_ref, wgt_ref, x1_ref, ys_ref, out_ref, buf_ref, sems):
    i = pl.program_id(0)
    n = pl.num_programs(0)
    tm = x1_ref.shape[0]
    slot = i % 2

    def gather(d_ref, into):
        def issue(t, c):
            for k in range(TOP_K):
                _row_copy(ys_ref, d_ref[k * tm + t], buf_ref.at[into, k], t, sems.at[into]).start(priority=k % 2)
            return c

        lax.fori_loop(0, tm, issue, 0, unroll=ROW_DMA_UNROLL)

    @pl.when(i == 0)
    def _():
        gather(dest_ref, 0)

    @pl.when(i + 1 < n)
    def _():
        gather(dest_next_ref, 1 - slot)

    for k in range(TOP_K):
        _rows_wait(buf_ref.at[slot, k], tm, sems.at[slot])
    acc = x1_ref[...]
    for k in range(TOP_K):
        acc = acc + wgt_ref[:, k:k + 1] * buf_ref[slot, k].reshape(acc.shape)
    out_ref[...] = acc


def _combine(dest, wgt_t, x1, ys):
    t, d = x1.shape
    tm = ROUTE_TILE
    n = t // tm
    return pl.pallas_call(
        _combine_kernel,
        grid=(n,),
        in_specs=[pl.BlockSpec((TOP_K * tm,), lambda i: (i,), memory_space=pltpu.SMEM),
                  pl.BlockSpec((TOP_K * tm,), lambda i: (jnp.minimum(i + 1, n - 1),), memory_space=pltpu.SMEM),
                  pl.BlockSpec((tm, TOP_K), lambda i: (i, 0)),
                  pl.BlockSpec((tm, d), lambda i: (i, 0)),
                  pl.BlockSpec(memory_space=pl.ANY)],
        out_specs=pl.BlockSpec((tm, d), lambda i: (i, 0)),
        out_shape=jax.ShapeDtypeStruct((t, d), F32),
        scratch_shapes=[pltpu.VMEM((2, TOP_K, tm) + ys.shape[1:], F32), pltpu.SemaphoreType.DMA((2,))],
        compiler_params=_params("arbitrary"),
    )(_tile_major(dest, tm), _tile_major(dest, tm), wgt_t, x1, ys)


def _moe(x1, hp, idx, rank, wgt, counts, w_gate_up, b_gate_up, w_down, b_down):
    t = x1.shape[0]
    tb = EXPERT_ROWS
    ne = w_gate_up.shape[0]
    n_blocks = -(-(t * TOP_K) // tb) + ne
    padded = ((counts + tb - 1) // tb) * tb
    pad_end = jnp.cumsum(padded)
    pad_start = pad_end - padded
    dest = rank + jnp.sum(jnp.where(idx[None] == jnp.arange(ne, dtype=jnp.int32)[:, None, None],
                                    pad_start.astype(jnp.int32)[:, None, None], 0), axis=0)
    block_start = jnp.arange(n_blocks, dtype=jnp.int32) * tb
    block_exp = jnp.minimum(jnp.sum((pad_end[None, :] <= block_start[:, None]).astype(jnp.int32), axis=1), ne - 1)
    n_used = (pad_end[-1:] // tb).astype(jnp.int32)
    xs = _dispatch(block_exp, n_used, dest, hp, n_blocks * tb)
    ys = _experts(block_exp, n_used, xs, w_gate_up, b_gate_up, w_down, b_down)
    return _combine(dest, wgt.T, x1, ys)


def kernel(x, norm1, w_in, q_norm, k_norm, lam_re, lam_im, log_dt, b_re, b_im, c_re, c_im, d_skip, w_glu, b_glu, w_attn_branch, w_ssm_branch, w_out, norm2, w_router, b_router, w_gate_up, b_gate_up, w_down, b_down):
    bsz, seq, d = x.shape
    t = bsz * seq
    x2 = x.reshape(t, d)
    for l in range(norm1.shape[0]):
        qkv0, qkv1, qkv2, u, gates = _in_proj(x2, norm1[l], w_in[l], q_norm[l], k_norm[l], seq)
        attn = [_attention(qkv0.reshape(bsz, seq, -1)), _attention(qkv1), _attention(qkv2)]
        z = _ssm(u, bsz, lam_re[l], lam_im[l], log_dt[l], b_re[l], b_im[l], c_re[l], c_im[l], d_skip[l])
        x1, hp, idx, rank, wgt, counts = _post_mix(
            x2, seq, attn, z, gates, w_attn_branch[l], w_glu[l], b_glu[l], w_ssm_branch[l], w_out[l], norm2[l],
            w_router[l], b_router[l])
        x2 = _moe(x1, hp, idx, rank, wgt, counts[:, 0], w_gate_up[l], b_gate_up[l], w_down[l], b_down[l])
    return x2.reshape(bsz, seq, d)
```

```python
import functools
import math

import numpy as np
import jax
import jax.numpy as jnp
from jax import lax
from jax.experimental import pallas as pl
from jax.experimental.pallas import tpu as pltpu

F32 = jnp.float32
BF16 = jnp.bfloat16

HEAD_DIM = 64
HEADS_PER_GROUP = 4
ATTN_GROUPS = ((128, 1), (512, 4), (2048, 16))
GROUP_WIDTH = HEADS_PER_GROUP * HEAD_DIM
ATTN_WIDTH = GROUP_WIDTH * len(ATTN_GROUPS)
BAND = 64
ROPE_THETA = 10000.0
SSM_GROUP = 16
SSM_STATE = 64
N_EXPERTS = 32
TOP_K = 4
SWIGLU_ALPHA = 1.702
SWIGLU_LIMIT = 7.0
NORM_EPS = 1e-6
MASK_VALUE = -1e30

ROW_TILE = 512
ATTN_Q_TILE = 1024
ATTN_Q_SUB = 128
SSM_CHUNK = 64
SSM_POWER_ROWS = -(-(SSM_CHUNK + 1) // 8) * 8
DISPATCH_TILE = 1024
ROUTE_TILE = 512
ROW_DMA_UNROLL = 8
EXPERT_ROWS = 512
VMEM_LIMIT = 56 * 1024 * 1024
LANES = 128


def _params(*sem):
    return pltpu.CompilerParams(dimension_semantics=sem, vmem_limit_bytes=VMEM_LIMIT)


def _const_spec(shape):
    return pl.BlockSpec(shape, lambda *_: (0,) * len(shape), pipeline_mode=pl.Buffered(1))


def _permute_qk(a):
    lead = a.shape[:-1]
    n = len(lead)
    a = a.reshape(*lead, len(ATTN_GROUPS), HEADS_PER_GROUP, 2, HEAD_DIM // 2)
    return a.transpose(*range(n), n, n + 2, n + 1, n + 3).reshape(*lead, ATTN_WIDTH)


def _permute_group_lanes(piece):
    hd, hh = HEAD_DIM, HEAD_DIM // 2
    firsts = [piece[:, h * hd:h * hd + hh] for h in range(HEADS_PER_GROUP)]
    seconds = [piece[:, h * hd + hh:(h + 1) * hd] for h in range(HEADS_PER_GROUP)]
    return jnp.concatenate(firsts + seconds, axis=-1)


def _in_proj_kernel(x_ref, g1_ref, wf_ref, gq_ref, gk_ref, cos_ref, sin_ref, seg_ref,
                    qkv0_ref, qkv1_ref, qkv2_ref, u_ref, gate_ref, w_ref):
    @pl.when(pl.program_id(0) == 0)
    def _():
        rows_per_step = 128

        def cast_rows(i, _):
            rows = pl.ds(pl.multiple_of(i * rows_per_step, rows_per_step), rows_per_step)
            for c0 in range(0, 2 * ATTN_WIDTH, GROUP_WIDTH):
                w_ref[rows, c0:c0 + GROUP_WIDTH] = _permute_group_lanes(wf_ref[rows, c0:c0 + GROUP_WIDTH]).astype(BF16)
            for c0 in range(2 * ATTN_WIDTH, wf_ref.shape[1], GROUP_WIDTH):
                w_ref[rows, c0:c0 + GROUP_WIDTH] = wf_ref[rows, c0:c0 + GROUP_WIDTH].astype(BF16)
            return 0

        lax.fori_loop(0, wf_ref.shape[0] // rows_per_step, cast_rows, 0)

    x = x_ref[...]
    ms = jnp.mean(x * x, axis=-1, keepdims=True)
    h = (x * lax.rsqrt(ms + NORM_EPS) * g1_ref[...]).astype(BF16)
    cos = cos_ref[...]
    sin = sin_ref[...]
    seg = seg_ref[...]
    half = GROUP_WIDTH // 2
    tm = x.shape[0]

    def proj(c0, width):
        return jnp.dot(h, w_ref[:, c0:c0 + width], preferred_element_type=F32)

    def sum_squares(t):
        return t[:, :half] * t[:, :half] + t[:, half:] * t[:, half:]

    def norm_rope(t, tot, gain_ref, g):
        t1, t2 = t[:, :half], t[:, half:]
        r = lax.rsqrt(tot * (1.0 / HEAD_DIM) + NORM_EPS)
        a = t1 * r * gain_ref[:, g * GROUP_WIDTH:g * GROUP_WIDTH + half]
        b = t2 * r * gain_ref[:, g * GROUP_WIDTH + half:(g + 1) * GROUP_WIDTH]
        return a * cos - b * sin, b * cos + a * sin

    outs = (qkv0_ref, qkv1_ref, qkv2_ref)
    for g, (_, dil) in enumerate(ATTN_GROUPS):
        tq = proj(g * GROUP_WIDTH, GROUP_WIDTH)
        tk = proj(ATTN_WIDTH + g * GROUP_WIDTH, GROUP_WIDTH)
        ss = jnp.concatenate([sum_squares(tq), sum_squares(tk)], axis=-1).astype(BF16)
        tot = jnp.dot(ss, seg, preferred_element_type=F32)
        q1, q2 = norm_rope(tq, tot[:, :half], gq_ref, g)
        k1, k2 = norm_rope(tk, tot[:, half:], gk_ref, g)
        v = proj(2 * ATTN_WIDTH + g * GROUP_WIDTH, GROUP_WIDTH)
        qkv = jnp.concatenate([q1, q2, k1, k2, v], axis=-1)
        if dil == 1:
            outs[g][...] = qkv.astype(BF16)
        else:
            outs[g][...] = jnp.swapaxes(qkv.reshape(tm // dil, dil, qkv.shape[1]), 0, 1).astype(BF16)
    o_u = 3 * ATTN_WIDTH
    u_w = u_ref.shape[-1]
    u_ref[...] = proj(o_u, u_w).astype(BF16)
    gate_w = gate_ref.shape[-1]
    for c in range(0, gate_w, 1024):
        gate_ref[:, c:c + 1024] = jax.nn.sigmoid(proj(o_u + u_w + c, 1024)).astype(BF16)


def _in_proj(x2, norm1, w_in, q_norm, k_norm, seq):
    t, d = x2.shape
    tm = ROW_TILE
    in_w = w_in.shape[1]
    ssm_w = d // 2
    gate_w = in_w - 3 * ATTN_WIDTH - ssm_w
    gq = (_permute_qk(q_norm.reshape(-1)) * (HEAD_DIM ** -0.5)).reshape(1, ATTN_WIDTH)
    gk = _permute_qk(k_norm.reshape(-1)).reshape(1, ATTN_WIDTH)
    inv_freq = ROPE_THETA ** (-jnp.arange(0, HEAD_DIM, 2, dtype=F32) / HEAD_DIM)
    ang = jnp.arange(seq, dtype=F32)[:, None] * inv_freq[None, :]
    cos = jnp.tile(jnp.cos(ang), (1, HEADS_PER_GROUP))
    sin = jnp.tile(jnp.sin(ang), (1, HEADS_PER_GROUP))
    lane = np.arange(GROUP_WIDTH)
    seg = jnp.asarray(lane[:, None] // (HEAD_DIM // 2) == lane[None, :] // (HEAD_DIM // 2), BF16)
    bsz = t // seq
    tiles_per_seq = seq // tm
    qkv_w = 3 * GROUP_WIDTH
    out_shape = [jax.ShapeDtypeStruct((t, qkv_w), BF16)]
    out_specs = [pl.BlockSpec((tm, qkv_w), lambda i: (i, 0))]
    for _, dil in ATTN_GROUPS[1:]:
        out_shape.append(jax.ShapeDtypeStruct((bsz * dil, seq // dil, qkv_w), BF16))
        out_specs.append(pl.BlockSpec((dil, tm // dil, qkv_w),
                                      lambda i: (i // tiles_per_seq, i % tiles_per_seq, 0)))
    out_shape += [jax.ShapeDtypeStruct((t, ssm_w), BF16), jax.ShapeDtypeStruct((t, gate_w), BF16)]
    out_specs += [pl.BlockSpec((tm, ssm_w), lambda i: (i, 0)), pl.BlockSpec((tm, gate_w), lambda i: (i, 0))]
    return pl.pallas_call(
        _in_proj_kernel,
        grid=(t // tm,),
        in_specs=[
            pl.BlockSpec((tm, d), lambda i: (i, 0)),
            _const_spec((1, d)),
            _const_spec((d, in_w)),
            _const_spec((1, ATTN_WIDTH)),
            _const_spec((1, ATTN_WIDTH)),
            pl.BlockSpec((tm, GROUP_WIDTH // 2), lambda i: (i % tiles_per_seq, 0)),
            pl.BlockSpec((tm, GROUP_WIDTH // 2), lambda i: (i % tiles_per_seq, 0)),
            _const_spec((GROUP_WIDTH, GROUP_WIDTH)),
        ],
        out_specs=out_specs,
        out_shape=out_shape,
        scratch_shapes=[pltpu.VMEM((d, in_w), BF16)],
        compiler_params=_params("arbitrary"),
    )(x2, norm1.reshape(1, d), w_in, gq, gk, cos, sin, seg)


def _attn_kernel(c_ref, kl_ref, vl_ref, kr_ref, vr_ref, o_ref, lse_ref, kcat_ref, vcat_ref, *, length):
    tq = c_ref.shape[1]
    gw = GROUP_WIDTH
    j = pl.program_id(1)
    kcat_ref[0:BAND] = kl_ref[0]
    kcat_ref[BAND:BAND + tq] = c_ref[0, :, gw:2 * gw]
    kcat_ref[BAND + tq:] = kr_ref[0]
    vcat_ref[0:BAND] = vl_ref[0]
    vcat_ref[BAND:BAND + tq] = c_ref[0, :, 2 * gw:]
    vcat_ref[BAND + tq:] = vr_ref[0]
    sub = min(ATTN_Q_SUB, tq)
    nk = sub + 2 * BAND
    nh = HEADS_PER_GROUP
    row = lax.broadcasted_iota(jnp.int32, (nh * sub, nk), 0) & (sub - 1)
    col = lax.broadcasted_iota(jnp.int32, (nh * sub, nk), 1)
    in_band = jnp.abs(col - BAND - row) <= BAND
    q_lane = lax.broadcasted_iota(jnp.int32, (1, gw), 1)
    q_head = (q_lane % (gw // 2)) // (HEAD_DIM // 2)
    v_head = q_lane // HEAD_DIM
    for s in range(tq // sub):
        q_s = c_ref[0, s * sub:(s + 1) * sub, 0:gw]
        k_s = kcat_ref[s * sub:s * sub + nk]
        v_s = vcat_ref[s * sub:s * sub + nk]
        kpos = j * tq + s * sub - BAND + col
        valid = in_band & (kpos >= 0) & (kpos < length)
        q4 = jnp.concatenate([jnp.where(q_head == hh, q_s, jnp.zeros_like(q_s)) for hh in range(nh)], axis=0)
        sc = lax.dot_general(q4, k_s, (((1,), (1,)), ((), ())), preferred_element_type=F32)
        sc = jnp.where(valid, sc, MASK_VALUE)
        m = jnp.max(sc, axis=-1, keepdims=True)
        p = jnp.exp(sc - m)
        den = jnp.sum(p, axis=-1, keepdims=True)
        o4 = jnp.dot(p.astype(BF16), v_s, preferred_element_type=F32) * (1.0 / den)
        l4 = m + jnp.log(den)
        out = o4[0:sub]
        lse = jnp.broadcast_to(l4[0:sub], (sub, gw))
        for hh in range(1, nh):
            out = jnp.where(v_head == hh, o4[hh * sub:(hh + 1) * sub], out)
            lse = jnp.where(v_head == hh, l4[hh * sub:(hh + 1) * sub], lse)
        o_ref[0, s * sub:(s + 1) * sub] = out.astype(BF16)
        lse_ref[0, s * sub:(s + 1) * sub] = lse


def _attention(qkv):
    n, length, _ = qkv.shape
    tq = min(ATTN_Q_TILE, length)
    gw = GROUP_WIDTH
    hb = tq // BAND
    last = length // BAND - 1

    def left(lane_block):
        return pl.BlockSpec((1, BAND, gw), lambda b, j: (b, jnp.maximum(j * hb - 1, 0), lane_block))

    def right(lane_block):
        return pl.BlockSpec((1, BAND, gw), lambda b, j: (b, jnp.minimum((j + 1) * hb, last), lane_block))

    return pl.pallas_call(
        functools.partial(_attn_kernel, length=length),
        grid=(n, length // tq),
        in_specs=[pl.BlockSpec((1, tq, 3 * gw), lambda b, j: (b, j, 0)), left(1), left(2), right(1), right(2)],
        out_specs=[pl.BlockSpec((1, tq, gw), lambda b, j: (b, j, 0)),
                   pl.BlockSpec((1, tq, gw), lambda b, j: (b, j, 0))],
        out_shape=[jax.ShapeDtypeStruct((n, length, gw), BF16), jax.ShapeDtypeStruct((n, length, gw), F32)],
        scratch_shapes=[pltpu.VMEM((tq + 2 * BAND, gw), BF16), pltpu.VMEM((tq + 2 * BAND, gw), BF16)],
        compiler_params=_params("parallel", "parallel"),
    )(qkv, qkv, qkv, qkv, qkv)


def _ssm_operators(lam_re, lam_im, log_dt, b_re, b_im, c_re, c_im, d_skip):
    L = SSM_CHUNK
    G, P = lam_re.shape[1:]
    C = d_skip.shape[-1]
    hp = lax.Precision.HIGHEST
    n = jnp.arange(L + 1, dtype=F32)[:, None, None]
    pad = SSM_POWER_ROWS - (L + 1)
    kern, pw, fbt, ct, consts = [], [], [], [], []
    for direction in range(2):
        lr, li = lam_re[direction].astype(F32), lam_im[direction].astype(F32)
        dt = jnp.exp(log_dt[direction].astype(F32))[:, None]
        mag = jnp.exp(lr * dt)
        ab_re, ab_im = mag * jnp.cos(li * dt), mag * jnp.sin(li * dt)
        den = lr * lr + li * li
        nr = ab_re - 1.0
        f_re = (nr * lr + ab_im * li) / den
        f_im = (ab_im * lr - nr * li) / den
        pmag = jnp.exp(n * (lr * dt)[None])
        p_re, p_im = pmag * jnp.cos(n * (li * dt)[None]), pmag * jnp.sin(n * (li * dt)[None])
        br, bi = b_re[direction].astype(F32), b_im[direction].astype(F32)
        fb_re = f_re[..., None] * br - f_im[..., None] * bi
        fb_im = f_re[..., None] * bi + f_im[..., None] * br
        cr, ci = c_re[direction].astype(F32), c_im[direction].astype(F32)
        pr, pi = p_re[:L].transpose(1, 2, 0)[..., None], p_im[:L].transpose(1, 2, 0)[..., None]
        crt, cit = cr.transpose(0, 2, 1)[:, :, None, :], ci.transpose(0, 2, 1)[:, :, None, :]
        e_re = (pr * crt - pi * cit).reshape(G, P, L * C)
        e_im = (pr * cit + pi * crt).reshape(G, P, L * C)
        kern.append((jnp.einsum('gpc,gpx->gcx', fb_re, e_re, precision=hp)
                     - jnp.einsum('gpc,gpx->gcx', fb_im, e_im, precision=hp)).reshape(G, C, L, C))
        for p in (p_re, p_im):
            p = jnp.pad(p.transpose(1, 0, 2), ((0, 0), (0, pad), (0, 0)))
            pw.append(jnp.concatenate([p, p, p, p], axis=-1))
        tr, ti = fb_re.transpose(0, 2, 1), fb_im.transpose(0, 2, 1)
        fbt.append(jnp.concatenate([tr, ti, ti, tr], axis=-1))
        fbt.append(jnp.concatenate([-ti, tr, tr, -ti], axis=-1))
        ct.append(jnp.concatenate([cr, -ci], axis=-1))
        ct.append(jnp.concatenate([-ci, -cr], axis=-1))
        a_re, a_im = p_re[L], p_im[L]
        consts.append(jnp.concatenate([a_re, a_re, a_re, a_re], axis=-1))
        consts.append(jnp.concatenate([-a_im, a_im, a_im, -a_im], axis=-1))
    kf, kb = kern
    zero = kf[:, :, 0] + kb[:, :, 0] + jnp.eye(C, dtype=F32)[None] * d_skip.astype(F32)[:, None, :]
    kfull = jnp.concatenate([kb[:, :, :0:-1], zero[:, :, None], kf[:, :, 1:], jnp.zeros_like(zero)[:, :, None]],
                            axis=2)
    flat = kfull.astype(BF16).reshape(G, C, 2 * L * C)
    return (flat, jnp.stack(pw, axis=1), jnp.stack(fbt, axis=1), jnp.stack(ct, axis=1),
            jnp.stack(consts + consts, axis=1))


def _ssm_kernel(u_ref, flat_ref, pw_ref, fbt_ref, ct_ref, cst_ref, z_ref,
                toe_ref, wst_ref, vt_ref, f_ref, s_ref, *, bsz):
    u = u_ref[0]
    nc = u.shape[0]
    sw = cst_ref.shape[-1]
    hw = sw // 2
    C = fbt_ref.shape[2]
    L = toe_ref.shape[0] // C
    lc = L * C

    per_tile = LANES // C
    span = (2 * L - per_tile) * C
    for b in range(per_tile):
        shifted = flat_ref[0, :, (per_tile - 1 - b) * C:(per_tile - 1 - b) * C + span]
        for a in range(L // per_tile):
            t_in = a * per_tile + b
            off = (L // per_tile - 1 - a) * LANES
            toe_ref[t_in * C:(t_in + 1) * C, :] = shifted[:, off:off + lc]

    fa_f, fb_f, fa_b, fb_b = fbt_ref[0, 0], fbt_ref[0, 1], fbt_ref[0, 2], fbt_ref[0, 3]
    ca_f, cb_f, ca_b, cb_b = ct_ref[0, 0], ct_ref[0, 1], ct_ref[0, 2], ct_ref[0, 3]

    def power(k, n):
        return pw_ref[0, k, n:n + 1, :]

    for t in range(L):
        rows = slice(t * C, (t + 1) * C)
        wst_ref[rows, 0:sw] = (fa_f * power(0, L - 1 - t) + fb_f * power(1, L - 1 - t)).astype(BF16)
        wst_ref[rows, sw:2 * sw] = (fa_b * power(2, t) + fb_b * power(3, t)).astype(BF16)
        vt_ref[rows, 0:hw] = (ca_f * power(0, t + 1)[:, :hw] + cb_f * power(1, t + 1)[:, :hw]).astype(BF16)
        vt_ref[rows, hw:sw] = (ca_b * power(2, L - t)[:, :hw] + cb_b * power(3, L - t)[:, :hw]).astype(BF16)

    f_ref[...] = jnp.dot(u, wst_ref[...], preferred_element_type=F32)
    cst = cst_ref[0]
    a1f, a2f, a1b, a2b = cst[0:1], cst[1:2], cst[2:3], cst[3:4]
    per = nc // bsz

    def swap(v):
        return jnp.concatenate([v[:, hw:], v[:, :hw]], axis=-1)

    rows = 8

    def step(i, carry):
        new = []
        for b in range(bsz):
            sf, sb = carry[2 * b], carry[2 * b + 1]
            base_f = pl.multiple_of(b * per + i * rows, rows)
            base_b = pl.multiple_of(b * per + per - rows - i * rows, rows)
            f_blk = f_ref[pl.ds(base_f, rows), 0:sw]
            b_blk = f_ref[pl.ds(base_b, rows), sw:2 * sw]
            f_rows, b_rows = [], []
            for r in range(rows):
                f_rows.append(sf[:, :hw])
                sf = a1f * sf + a2f * swap(sf) + f_blk[r:r + 1]
                b_rows.append(sb[:, :hw])
                sb = a1b * sb + a2b * swap(sb) + b_blk[rows - 1 - r:rows - r]
            s_ref[pl.ds(base_f, rows), 0:hw] = jnp.concatenate(f_rows, axis=0)
            s_ref[pl.ds(base_b, rows), hw:sw] = jnp.concatenate(b_rows[::-1], axis=0)
            new += [sf, sb]
        return tuple(new)

    lax.fori_loop(0, per // rows, step, tuple(jnp.zeros((1, sw), F32) for _ in range(2 * bsz)))
    y = jnp.dot(u, toe_ref[...], preferred_element_type=F32)
    y = y + lax.dot_general(s_ref[...].astype(BF16), vt_ref[...], (((1,), (1,)), ((), ())),
                            preferred_element_type=F32)
    z_ref[0] = jax.nn.gelu(y).astype(BF16)


def _ssm(u, bsz, lam_re, lam_im, log_dt, b_re, b_im, c_re, c_im, d_skip):
    t, width = u.shape
    L, C = SSM_CHUNK, d_skip.shape[-1]
    G = width // C
    nc = t // L
    flat, pw, fbt, ct, consts = _ssm_operators(lam_re, lam_im, log_dt, b_re, b_im, c_re, c_im, d_skip)
    sw = consts.shape[-1]
    ug = u.reshape(nc, L, G, C).transpose(2, 0, 1, 3).reshape(G, nc, L * C)

    def group_spec(a):
        return pl.BlockSpec((1,) + a.shape[1:], lambda g: (g,) + (0,) * (a.ndim - 1))

    z = pl.pallas_call(
        functools.partial(_ssm_kernel, bsz=bsz),
        grid=(G,),
        in_specs=[group_spec(a) for a in (ug, flat, pw, fbt, ct, consts)],
        out_specs=pl.BlockSpec((1, nc, L * C), lambda g: (g, 0, 0)),
        out_shape=jax.ShapeDtypeStruct((G, nc, L * C), BF16),
        scratch_shapes=[pltpu.VMEM((L * C, L * C), BF16), pltpu.VMEM((L * C, 2 * sw), BF16),
                        pltpu.VMEM((L * C, sw), BF16), pltpu.VMEM((nc, 2 * sw), F32), pltpu.VMEM((nc, sw), F32)],
        compiler_params=_params("parallel"),
    )(ug, flat, pw, fbt, ct, consts)
    return z.reshape(G, nc, L, C).transpose(1, 2, 0, 3).reshape(t, width)


def _post_kernel(o0_ref, l0_ref, o1_ref, l1_ref, o2_ref, l2_ref, z_ref, gate_ref, x_ref,
                 wa_ref, wglu_ref, bglu_ref, ws_ref, wo_ref, g2_ref, wr_ref, wrlo_ref, br_ref, tri_ref,
                 x1_ref, hp_ref, idx_ref, rank_ref, wgt_ref, cnt_ref, o_scr, l_scr, carry_ref):
    tm, d = x_ref.shape

    @pl.when(pl.program_id(0) == 0)
    def _():
        carry_ref[...] = jnp.zeros_like(carry_ref)

    outs = [o0_ref[...].astype(F32)]
    lses = [l0_ref[...]]
    for (o_ref, l_ref), (_, dil) in zip(((o1_ref, l1_ref), (o2_ref, l2_ref)), ATTN_GROUPS[1:]):
        nlb = o_scr.shape[0]
        for r in range(dil):
            for c in range(nlb):
                o_scr[c, pl.ds(r, tm // dil, stride=dil), :] = o_ref[r, :, c * LANES:(c + 1) * LANES].astype(F32)
                l_scr[c, pl.ds(r, tm // dil, stride=dil), :] = l_ref[r, :, c * LANES:(c + 1) * LANES]
        outs.append(jnp.concatenate([o_scr[c] for c in range(nlb)], axis=-1))
        lses.append(jnp.concatenate([l_scr[c] for c in range(nlb)], axis=-1))
    top = jnp.maximum(jnp.maximum(lses[0], lses[1]), lses[2])
    es = [jnp.exp(l - top) for l in lses]
    y_attn = (es[0] * outs[0] + es[1] * outs[1] + es[2] * outs[2]) / (es[0] + es[1] + es[2])
    ya = jnp.dot(y_attn.astype(BF16), wa_ref[...], preferred_element_type=F32)
    z = z_ref[...]
    glu = jnp.dot(z, wglu_ref[...], preferred_element_type=F32) + bglu_ref[...]
    zz = z.astype(F32) * jax.nn.sigmoid(glu)
    ys = jnp.dot(zz.astype(BF16), ws_ref[...], preferred_element_type=F32)
    mixed = gate_ref[:, :d].astype(F32) * ya + gate_ref[:, d:].astype(F32) * ys
    x1 = x_ref[...] + jnp.dot(mixed.astype(BF16), wo_ref[...], preferred_element_type=F32)
    x1_ref[...] = x1
    ms = jnp.mean(x1 * x1, axis=-1, keepdims=True)
    h2 = x1 * lax.rsqrt(ms + NORM_EPS) * g2_ref[...]
    hb = h2.astype(BF16)
    hp_ref[...] = h2.reshape(hp_ref.shape)

    h_lo = (h2 - hb.astype(F32)).astype(BF16)
    nt = (((1,), (1,)), ((), ()))
    logits = (lax.dot_general(wr_ref[...], hb, nt, preferred_element_type=F32)
              + lax.dot_general(wrlo_ref[...], hb, nt, preferred_element_type=F32)
              + lax.dot_general(wr_ref[...], h_lo, nt, preferred_element_type=F32) + br_ref[...])
    ne = logits.shape[0]
    e_iota = lax.broadcasted_iota(jnp.int32, logits.shape, 0)
    vals, idxs = [], []
    work = logits
    for _ in range(TOP_K):
        m = jnp.max(work, axis=0, keepdims=True)
        i = jnp.min(jnp.where(work == m, e_iota, ne), axis=0, keepdims=True)
        vals.append(m)
        idxs.append(i)
        work = jnp.where(e_iota == i, -jnp.inf, work)
    ex = [jnp.exp(v - vals[0]) for v in vals]
    tot = ex[0] + ex[1] + ex[2] + ex[3]
    onehot = jnp.zeros(logits.shape, F32)
    for i in idxs:
        onehot = onehot + (e_iota == i).astype(F32)
    before = jnp.dot(onehot.astype(BF16), tri_ref[...], preferred_element_type=F32) + carry_ref[:, 0:1]
    for k in range(TOP_K):
        idx_ref[k:k + 1, :] = idxs[k]
        rank_ref[k:k + 1, :] = jnp.sum(jnp.where(e_iota == idxs[k], before, 0.0), axis=0,
                                       keepdims=True).astype(jnp.int32)
        wgt_ref[k:k + 1, :] = ex[k] / tot
    carry_ref[...] = carry_ref[...] + jnp.sum(onehot, axis=1, keepdims=True)
    cnt_ref[...] = carry_ref[...].astype(jnp.int32)


def _post_mix(x2, seq, attn, z, gates, w_attn_branch, w_glu, b_glu, w_ssm_branch, w_out, norm2, w_router, b_router):
    t, d = x2.shape
    tm = ROW_TILE
    gw = GROUP_WIDTH
    tiles_per_seq = seq // tm
    ne = w_router.shape[1]
    ssm_w = z.shape[1]
    in_specs, args = [], []
    for (o, lse), (_, dil) in zip(attn, ATTN_GROUPS):
        if dil == 1:
            spec = pl.BlockSpec((tm, gw), lambda i: (i, 0))
            o, lse = o.reshape(t, gw), lse.reshape(t, gw)
        else:
            spec = pl.BlockSpec((dil, tm // dil, gw), lambda i: (i // tiles_per_seq, i % tiles_per_seq, 0))
        in_specs += [spec, spec]
        args += [o, lse]
    tri = jnp.asarray(np.arange(tm)[:, None] < np.arange(tm)[None, :], BF16)
    in_specs += [pl.BlockSpec((tm, ssm_w), lambda i: (i, 0)),
                 pl.BlockSpec((tm, 2 * d), lambda i: (i, 0)),
                 pl.BlockSpec((tm, d), lambda i: (i, 0)),
                 _const_spec((gw, d)), _const_spec((ssm_w, ssm_w)), _const_spec((1, ssm_w)),
                 _const_spec((ssm_w, d)), _const_spec((d, d)), _const_spec((1, d)),
                 _const_spec((ne, d)), _const_spec((ne, d)), _const_spec((ne, 1)), _const_spec((tm, tm))]
    wr_t = w_router.T.astype(F32)
    wr_hi = wr_t.astype(BF16)
    wr_lo = (wr_t - wr_hi.astype(F32)).astype(BF16)
    args += [z, gates, x2, w_attn_branch.astype(BF16), w_glu.astype(BF16), b_glu.reshape(1, ssm_w),
             w_ssm_branch.astype(BF16), w_out.astype(BF16), norm2.reshape(1, d),
             wr_hi, wr_lo, b_router.reshape(ne, 1), tri]
    tok_spec = pl.BlockSpec((TOP_K, tm), lambda i: (0, i))
    return pl.pallas_call(
        _post_kernel,
        grid=(t // tm,),
        in_specs=in_specs,
        out_specs=[pl.BlockSpec((tm, d), lambda i: (i, 0)),
                   pl.BlockSpec((tm, d // LANES, LANES), lambda i: (i, 0, 0)),
                   tok_spec, tok_spec, tok_spec, pl.BlockSpec((ne, LANES), lambda i: (0, 0))],
        out_shape=[jax.ShapeDtypeStruct((t, d), F32), jax.ShapeDtypeStruct((t, d // LANES, LANES), F32),
                   jax.ShapeDtypeStruct((TOP_K, t), jnp.int32), jax.ShapeDtypeStruct((TOP_K, t), jnp.int32),
                   jax.ShapeDtypeStruct((TOP_K, t), F32), jax.ShapeDtypeStruct((ne, LANES), jnp.int32)],
        scratch_shapes=[pltpu.VMEM((gw // LANES, tm, LANES), F32), pltpu.VMEM((gw // LANES, tm, LANES), F32),
                        pltpu.VMEM((ne, LANES), F32)],
        compiler_params=_params("arbitrary"),
    )(*args)


def _row_copy(src_ref, src_row, dst_ref, dst_row, sem):
    return pltpu.make_async_copy(src_ref.at[pl.ds(src_row, 1)], dst_ref.at[pl.ds(dst_row, 1)], sem)


def _tile_major(dest, tm):
    k, t = dest.shape
    return dest.reshape(k, t // tm, tm).transpose(1, 0, 2).reshape(-1)


def _rows_wait(ref, n_rows, sem):
    pltpu.make_async_copy(ref.at[pl.ds(0, n_rows)], ref.at[pl.ds(0, n_rows)], sem).wait()


def _dispatch_kernel(be_ref, nu_ref, dest_ref, h_ref, xs_ref, zero_ref, sems):
    i = pl.program_id(0)
    tm = h_ref.shape[0]
    tb = zero_ref.shape[0]
    n_blocks = be_ref.shape[0]

    @pl.when(i == 0)
    def _():
        zero_ref[...] = jnp.zeros_like(zero_ref)

        def holds_padding(b):
            return (b >= nu_ref[0] - 1) | (be_ref[b] != be_ref[jnp.minimum(b + 1, n_blocks - 1)])

        def zero_copy(b):
            return pltpu.make_async_copy(zero_ref, xs_ref.at[pl.ds(pl.multiple_of(b * tb, tb), tb)], sems.at[1])

        def start(b, c):
            @pl.when(holds_padding(b))
            def _():
                zero_copy(b).start()
            return c

        def wait(b, c):
            @pl.when(holds_padding(b))
            def _():
                zero_copy(b).wait()
            return c

        lax.fori_loop(0, n_blocks, start, 0)
        lax.fori_loop(0, n_blocks, wait, 0)

    def issue(t, c):
        for k in range(TOP_K):
            _row_copy(h_ref, t, xs_ref, dest_ref[k * tm + t], sems.at[0]).start(priority=k % 2)
        return c

    lax.fori_loop(0, tm, issue, 0, unroll=ROW_DMA_UNROLL)
    _rows_wait(xs_ref, TOP_K * tm, sems.at[0])


def _dispatch(block_exp, n_used, dest, hp, n_rows):
    t = hp.shape[0]
    slab = hp.shape[1:]
    tm = DISPATCH_TILE
    grid_spec = pltpu.PrefetchScalarGridSpec(
        num_scalar_prefetch=2,
        grid=(t // tm,),
        in_specs=[pl.BlockSpec((TOP_K * tm,), lambda i, be, nu: (i,), memory_space=pltpu.SMEM),
                  pl.BlockSpec((tm,) + slab, lambda i, be, nu: (i,) + (0,) * len(slab))],
        out_specs=pl.BlockSpec(memory_space=pl.ANY),
        scratch_shapes=[pltpu.VMEM((EXPERT_ROWS,) + slab, hp.dtype), pltpu.SemaphoreType.DMA((2,))],
    )
    return pl.pallas_call(
        _dispatch_kernel,
        grid_spec=grid_spec,
        out_shape=jax.ShapeDtypeStruct((n_rows,) + slab, hp.dtype),
        compiler_params=_params("arbitrary"),
    )(block_exp, n_used, _tile_major(dest, tm), hp)


def _expert_kernel(be_ref, nu_ref, xs_ref, wgu_ref, bgu_ref, wd_ref, bd_ref, ys_ref, wgu_bf, wd_bf):
    i = pl.program_id(0)
    active = i < nu_ref[0]

    @pl.when(active & ((i == 0) | (be_ref[i] != be_ref[jnp.maximum(i - 1, 0)])))
    def _():
        wgu_bf[...] = wgu_ref[0].astype(BF16)
        wd_bf[...] = wd_ref[0].astype(BF16)

    @pl.when(active)
    def _():
        x = xs_ref[...].reshape(xs_ref.shape[0], -1).astype(BF16)
        gu = jnp.dot(x, wgu_bf[...], preferred_element_type=F32) + bgu_ref[0]
        de = gu.shape[1] // 2
        gate = jnp.minimum(gu[:, :de], SWIGLU_LIMIT)
        up = jnp.clip(gu[:, de:], -SWIGLU_LIMIT, SWIGLU_LIMIT)
        act = gate * jax.nn.sigmoid(SWIGLU_ALPHA * gate) * (up + 1.0)
        y = jnp.dot(act.astype(BF16), wd_bf[...], preferred_element_type=F32) + bd_ref[0]
        ys_ref[...] = y.reshape(ys_ref.shape)

    @pl.when(i >= nu_ref[0])
    def _():
        ys_ref[...] = jnp.zeros_like(ys_ref)


def _experts(block_exp, n_used, xs, w_gate_up, b_gate_up, w_down, b_down):
    n_rows = xs.shape[0]
    ne, d, de2 = w_gate_up.shape
    tb = EXPERT_ROWS

    def row_block(i, be, nu):
        return (jnp.minimum(i, nu[0] - 1),) + (0,) * (xs.ndim - 1)

    def expert_block(i, be, nu):
        return (be[jnp.minimum(i, nu[0] - 1)], 0, 0)

    grid_spec = pltpu.PrefetchScalarGridSpec(
        num_scalar_prefetch=2,
        grid=(n_rows // tb,),
        in_specs=[pl.BlockSpec((tb,) + xs.shape[1:], row_block),
                  pl.BlockSpec((1, d, de2), expert_block),
                  pl.BlockSpec((1, 1, de2), expert_block),
                  pl.BlockSpec((1, de2 // 2, d), expert_block),
                  pl.BlockSpec((1, 1, d), expert_block)],
        out_specs=pl.BlockSpec((tb,) + xs.shape[1:], lambda i, be, nu: (i,) + (0,) * (xs.ndim - 1)),
        scratch_shapes=[pltpu.VMEM((d, de2), BF16), pltpu.VMEM((de2 // 2, d), BF16)],
    )
    return pl.pallas_call(
        _expert_kernel,
        grid_spec=grid_spec,
        out_shape=jax.ShapeDtypeStruct(xs.shape, F32),
        compiler_params=_params("arbitrary"),
    )(block_exp, n_used, xs, w_gate_up, b_gate_up.reshape(ne, 1, de2), w_down, b_down.reshape(ne, 1, d))


def _combine_kernel(dest_ref, dest_next_ref, wgt_ref, x1_ref, ys_ref, out_ref, buf_ref, sems):
    i = pl.program_id(0)
    n = pl.num_programs(0)
    tm = x1_ref.shape[0]
    slot = i % 2

    def gather(d_ref, into):
        def issue(t, c):
            for k in range(TOP_K):
                _row_copy(ys_ref, d_ref[k * tm + t], buf_ref.at[into, k], t, sems.at[into]).start(priority=k % 2)
            return c

        lax.fori_loop(0, tm, issue, 0, unroll=ROW_DMA_UNROLL)

    @pl.when(i == 0)
    def _():
        gather(dest_ref, 0)

    @pl.when(i + 1 < n)
    def _():
        gather(dest_next_ref, 1 - slot)

    for k in range(TOP_K):
        _rows_wait(buf_ref.at[slot, k], tm, sems.at[slot])
    acc = x1_ref[...]
    for k in range(TOP_K):
        acc = acc + wgt_ref[:, k:k + 1] * buf_ref[slot, k].reshape(acc.shape)
    out_ref[...] = acc


def _combine(dest, wgt_t, x1, ys):
    t, d = x1.shape
    tm = ROUTE_TILE
    n = t // tm
    return pl.pallas_call(
        _combine_kernel,
        grid=(n,),
        in_specs=[pl.BlockSpec((TOP_K * tm,), lambda i: (i,), memory_space=pltpu.SMEM),
                  pl.BlockSpec((TOP_K * tm,), lambda i: (jnp.minimum(i + 1, n - 1),), memory_space=pltpu.SMEM),
                  pl.BlockSpec((tm, TOP_K), lambda i: (i, 0)),
                  pl.BlockSpec((tm, d), lambda i: (i, 0)),
                  pl.BlockSpec(memory_space=pl.ANY)],
        out_specs=pl.BlockSpec((tm, d), lambda i: (i, 0)),
        out_shape=jax.ShapeDtypeStruct((t, d), F32),
        scratch_shapes=[pltpu.VMEM((2, TOP_K, tm) + ys.shape[1:], F32), pltpu.SemaphoreType.DMA((2,))],
        compiler_params=_params("arbitrary"),
    )(_tile_major(dest, tm), _tile_major(dest, tm), wgt_t, x1, ys)


def _moe(x1, hp, idx, rank, wgt, counts, w_gate_up, b_gate_up, w_down, b_down):
    t = x1.shape[0]
    tb = EXPERT_ROWS
    ne = w_gate_up.shape[0]
    n_blocks = -(-(t * TOP_K) // tb) + ne
    padded = ((counts + tb - 1) // tb) * tb
    pad_end = jnp.cumsum(padded)
    pad_start = pad_end - padded
    dest = rank + jnp.sum(jnp.where(idx[None] == jnp.arange(ne, dtype=jnp.int32)[:, None, None],
                                    pad_start.astype(jnp.int32)[:, None, None], 0), axis=0)
    block_start = jnp.arange(n_blocks, dtype=jnp.int32) * tb
    block_exp = jnp.minimum(jnp.sum((pad_end[None, :] <= block_start[:, None]).astype(jnp.int32), axis=1), ne - 1)
    n_used = (pad_end[-1:] // tb).astype(jnp.int32)
    xs = _dispatch(block_exp, n_used, dest, hp, n_blocks * tb)
    ys = _experts(block_exp, n_used, xs, w_gate_up, b_gate_up, w_down, b_down)
    return _combine(dest, wgt.T, x1, ys)


def kernel(x, norm1, w_in, q_norm, k_norm, lam_re, lam_im, log_dt, b_re, b_im, c_re, c_im, d_skip, w_glu, b_glu, w_attn_branch, w_ssm_branch, w_out, norm2, w_router, b_router, w_gate_up, b_gate_up, w_down, b_down):
    bsz, seq, d = x.shape
    t = bsz * seq
    x2 = x.reshape(t, d)
    for l in range(norm1.shape[0]):
        qkv0, qkv1, qkv2, u, gates = _in_proj(x2, norm1[l], w_in[l], q_norm[l], k_norm[l], seq)
        attn = [_attention(qkv0.reshape(bsz, seq, -1)), _attention(qkv1), _attention(qkv2)]
        z = _ssm(u, bsz, lam_re[l], lam_im[l], log_dt[l], b_re[l], b_im[l], c_re[l], c_im[l], d_skip[l])
        x1, hp, idx, rank, wgt, counts = _post_mix(
            x2, seq, attn, z, gates, w_attn_branch[l], w_glu[l], b_glu[l], w_ssm_branch[l], w_out[l], norm2[l],
            w_router[l], b_router[l])
        x2 = _moe(x1, hp, idx, rank, wgt, counts[:, 0], w_gate_up[l], b_gate_up[l], w_down[l], b_down[l])
    return x2.reshape(bsz, seq, d)
```

```python
import functools

import numpy as np
import jax
import jax.numpy as jnp
from jax import lax
from jax.experimental import pallas as pl
from jax.experimental.pallas import tpu as pltpu

F32 = jnp.float32
BF16 = jnp.bfloat16

HEAD_DIM = 64
HEADS_PER_GROUP = 4
ATTN_GROUPS = ((128, 1), (512, 4), (2048, 16))
GROUP_WIDTH = HEADS_PER_GROUP * HEAD_DIM
ATTN_WIDTH = GROUP_WIDTH * len(ATTN_GROUPS)
BAND = 64
ROPE_THETA = 10000.0
TOP_K = 4
SWIGLU_ALPHA = 1.702
SWIGLU_LIMIT = 7.0
NORM_EPS = 1e-6
MASK_VALUE = -1e30

ROW_TILE = 512
ATTN_Q_TILE = 1024
ATTN_Q_SUB = 128
SSM_CHUNK = 64
SSM_POWER_ROWS = -(-(SSM_CHUNK + 1) // 8) * 8
DISPATCH_TILE = 1024
ROUTE_TILE = 512
ROW_DMA_UNROLL = 8
EXPERT_ROWS = 512
V7X_VMEM_BYTES = 64 * 1024 * 1024
VMEM_LIMIT = V7X_VMEM_BYTES - 8 * 1024 * 1024
LANES = 128


def _params(*sem):
    return pltpu.CompilerParams(dimension_semantics=sem, vmem_limit_bytes=VMEM_LIMIT)


def _const_spec(shape):
    return pl.BlockSpec(shape, lambda *_: (0,) * len(shape), pipeline_mode=pl.Buffered(1))


def _permute_qk(a):
    lead = a.shape[:-1]
    n = len(lead)
    a = a.reshape(*lead, len(ATTN_GROUPS), HEADS_PER_GROUP, 2, HEAD_DIM // 2)
    return a.transpose(*range(n), n, n + 2, n + 1, n + 3).reshape(*lead, ATTN_WIDTH)


def _permute_group_lanes(piece):
    hd, hh = HEAD_DIM, HEAD_DIM // 2
    firsts = [piece[:, h * hd:h * hd + hh] for h in range(HEADS_PER_GROUP)]
    seconds = [piece[:, h * hd + hh:(h + 1) * hd] for h in range(HEADS_PER_GROUP)]
    return jnp.concatenate(firsts + seconds, axis=-1)


def _in_proj_kernel(x_ref, g1_ref, wf_ref, gq_ref, gk_ref, cos_ref, sin_ref, seg_ref,
                    qkv0_ref, qkv1_ref, qkv2_ref, u_ref, gate_ref, w_ref):
    @pl.when(pl.program_id(0) == 0)
    def _():
        rows_per_step = 128

        def cast_rows(i, _):
            rows = pl.ds(pl.multiple_of(i * rows_per_step, rows_per_step), rows_per_step)
            for c0 in range(0, 2 * ATTN_WIDTH, GROUP_WIDTH):
                w_ref[rows, c0:c0 + GROUP_WIDTH] = _permute_group_lanes(wf_ref[rows, c0:c0 + GROUP_WIDTH]).astype(BF16)
            for c0 in range(2 * ATTN_WIDTH, wf_ref.shape[1], GROUP_WIDTH):
                w_ref[rows, c0:c0 + GROUP_WIDTH] = wf_ref[rows, c0:c0 + GROUP_WIDTH].astype(BF16)
            return 0

        lax.fori_loop(0, wf_ref.shape[0] // rows_per_step, cast_rows, 0)

    x = x_ref[...]
    ms = jnp.mean(x * x, axis=-1, keepdims=True)
    h = (x * lax.rsqrt(ms + NORM_EPS) * g1_ref[...]).astype(BF16)
    cos = cos_ref[...]
    sin = sin_ref[...]
    seg = seg_ref[...]
    half = GROUP_WIDTH // 2
    tm = x.shape[0]

    def proj(c0, width):
        return jnp.dot(h, w_ref[:, c0:c0 + width], preferred_element_type=F32)

    def sum_squares(t):
        return t[:, :half] * t[:, :half] + t[:, half:] * t[:, half:]

    def norm_rope(t, tot, gain_ref, g):
        t1, t2 = t[:, :half], t[:, half:]
        r = lax.rsqrt(tot * (1.0 / HEAD_DIM) + NORM_EPS)
        a = t1 * r * gain_ref[:, g * GROUP_WIDTH:g * GROUP_WIDTH + half]
        b = t2 * r * gain_ref[:, g * GROUP_WIDTH + half:(g + 1) * GROUP_WIDTH]
        return a * cos - b * sin, b * cos + a * sin

    outs = (qkv0_ref, qkv1_ref, qkv2_ref)
    for g, (_, dil) in enumerate(ATTN_GROUPS):
        tq = proj(g * GROUP_WIDTH, GROUP_WIDTH)
        tk = proj(ATTN_WIDTH + g * GROUP_WIDTH, GROUP_WIDTH)
        ss = jnp.concatenate([sum_squares(tq), sum_squares(tk)], axis=-1).astype(BF16)
        tot = jnp.dot(ss, seg, preferred_element_type=F32)
        q1, q2 = norm_rope(tq, tot[:, :half], gq_ref, g)
        k1, k2 = norm_rope(tk, tot[:, half:], gk_ref, g)
        v = proj(2 * ATTN_WIDTH + g * GROUP_WIDTH, GROUP_WIDTH)
        qkv = jnp.concatenate([q1, q2, k1, k2, v], axis=-1)
        if dil == 1:
            outs[g][...] = qkv.astype(BF16)
        else:
            outs[g][...] = jnp.swapaxes(qkv.reshape(tm // dil, dil, qkv.shape[1]), 0, 1).astype(BF16)
    o_u = 3 * ATTN_WIDTH
    u_w = u_ref.shape[-1]
    u_ref[...] = proj(o_u, u_w).astype(BF16)
    gate_w = gate_ref.shape[-1]
    for c in range(0, gate_w, 1024):
        gate_ref[:, c:c + 1024] = jax.nn.sigmoid(proj(o_u + u_w + c, 1024)).astype(BF16)


def _in_proj(x2, norm1, w_in, q_norm, k_norm, seq):
    t, d = x2.shape
    tm = ROW_TILE
    in_w = w_in.shape[1]
    ssm_w = d // 2
    gate_w = in_w - 3 * ATTN_WIDTH - ssm_w
    gq = (_permute_qk(q_norm.reshape(-1)) * (HEAD_DIM ** -0.5)).reshape(1, ATTN_WIDTH)
    gk = _permute_qk(k_norm.reshape(-1)).reshape(1, ATTN_WIDTH)
    inv_freq = ROPE_THETA ** (-jnp.arange(0, HEAD_DIM, 2, dtype=F32) / HEAD_DIM)
    ang = jnp.arange(seq, dtype=F32)[:, None] * inv_freq[None, :]
    cos = jnp.tile(jnp.cos(ang), (1, HEADS_PER_GROUP))
    sin = jnp.tile(jnp.sin(ang), (1, HEADS_PER_GROUP))
    lane = np.arange(GROUP_WIDTH)
    seg = jnp.asarray(lane[:, None] // (HEAD_DIM // 2) == lane[None, :] // (HEAD_DIM // 2), BF16)
    bsz = t // seq
    tiles_per_seq = seq // tm
    qkv_w = 3 * GROUP_WIDTH
    out_shape = [jax.ShapeDtypeStruct((t, qkv_w), BF16)]
    out_specs = [pl.BlockSpec((tm, qkv_w), lambda i: (i, 0))]
    for _, dil in ATTN_GROUPS[1:]:
        out_shape.append(jax.ShapeDtypeStruct((bsz * dil, seq // dil, qkv_w), BF16))
        out_specs.append(pl.BlockSpec((dil, tm // dil, qkv_w),
                                      lambda i: (i // tiles_per_seq, i % tiles_per_seq, 0)))
    out_shape += [jax.ShapeDtypeStruct((t, ssm_w), BF16), jax.ShapeDtypeStruct((t, gate_w), BF16)]
    out_specs += [pl.BlockSpec((tm, ssm_w), lambda i: (i, 0)), pl.BlockSpec((tm, gate_w), lambda i: (i, 0))]
    return pl.pallas_call(
        _in_proj_kernel,
        grid=(t // tm,),
        in_specs=[
            pl.BlockSpec((tm, d), lambda i: (i, 0)),
            _const_spec((1, d)),
            _const_spec((d, in_w)),
            _const_spec((1, ATTN_WIDTH)),
            _const_spec((1, ATTN_WIDTH)),
            pl.BlockSpec((tm, GROUP_WIDTH // 2), lambda i: (i % tiles_per_seq, 0)),
            pl.BlockSpec((tm, GROUP_WIDTH // 2), lambda i: (i % tiles_per_seq, 0)),
            _const_spec((GROUP_WIDTH, GROUP_WIDTH)),
        ],
        out_specs=out_specs,
        out_shape=out_shape,
        scratch_shapes=[pltpu.VMEM((d, in_w), BF16)],
        compiler_params=_params("arbitrary"),
    )(x2, norm1.reshape(1, d), w_in, gq, gk, cos, sin, seg)


def _attn_kernel(c_ref, kl_ref, vl_ref, kr_ref, vr_ref, o_ref, lse_ref, kcat_ref, vcat_ref, *, length):
    tq = c_ref.shape[1]
    gw = GROUP_WIDTH
    j = pl.program_id(1)
    kcat_ref[0:BAND] = kl_ref[0]
    kcat_ref[BAND:BAND + tq] = c_ref[0, :, gw:2 * gw]
    kcat_ref[BAND + tq:] = kr_ref[0]
    vcat_ref[0:BAND] = vl_ref[0]
    vcat_ref[BAND:BAND + tq] = c_ref[0, :, 2 * gw:]
    vcat_ref[BAND + tq:] = vr_ref[0]
    sub = min(ATTN_Q_SUB, tq)
    nk = sub + 2 * BAND
    nh = HEADS_PER_GROUP
    row = lax.broadcasted_iota(jnp.int32, (nh * sub, nk), 0) & (sub - 1)
    col = lax.broadcasted_iota(jnp.int32, (nh * sub, nk), 1)
    in_band = jnp.abs(col - BAND - row) <= BAND
    q_lane = lax.broadcasted_iota(jnp.int32, (1, gw), 1)
    q_head = (q_lane % (gw // 2)) // (HEAD_DIM // 2)
    v_head = q_lane // HEAD_DIM
    for s in range(tq // sub):
        q_s = c_ref[0, s * sub:(s + 1) * sub, 0:gw]
        k_s = kcat_ref[s * sub:s * sub + nk]
        v_s = vcat_ref[s * sub:s * sub + nk]
        kpos = j * tq + s * sub - BAND + col
        valid = in_band & (kpos >= 0) & (kpos < length)
        q4 = jnp.concatenate([jnp.where(q_head == hh, q_s, jnp.zeros_like(q_s)) for hh in range(nh)], axis=0)
        sc = lax.dot_general(q4, k_s, (((1,), (1,)), ((), ())), preferred_element_type=F32)
        sc = jnp.where(valid, sc, MASK_VALUE)
        m = jnp.max(sc, axis=-1, keepdims=True)
        p = jnp.exp(sc - m)
        den = jnp.sum(p, axis=-1, keepdims=True)
        o4 = jnp.dot(p.astype(BF16), v_s, preferred_element_type=F32) * (1.0 / den)
        l4 = m + jnp.log(den)
        out = o4[0:sub]
        lse = jnp.broadcast_to(l4[0:sub], (sub, gw))
        for hh in range(1, nh):
            out = jnp.where(v_head == hh, o4[hh * sub:(hh + 1) * sub], out)
            lse = jnp.where(v_head == hh, l4[hh * sub:(hh + 1) * sub], lse)
        o_ref[0, s * sub:(s + 1) * sub] = out.astype(BF16)
        lse_ref[0, s * sub:(s + 1) * sub] = lse


def _attention(qkv):
    n, length, _ = qkv.shape
    tq = min(ATTN_Q_TILE, length)
    gw = GROUP_WIDTH
    hb = tq // BAND
    last = length // BAND - 1

    def left(lane_block):
        return pl.BlockSpec((1, BAND, gw), lambda b, j: (b, jnp.maximum(j * hb - 1, 0), lane_block))

    def right(lane_block):
        return pl.BlockSpec((1, BAND, gw), lambda b, j: (b, jnp.minimum((j + 1) * hb, last), lane_block))

    return pl.pallas_call(
        functools.partial(_attn_kernel, length=length),
        grid=(n, length // tq),
        in_specs=[pl.BlockSpec((1, tq, 3 * gw), lambda b, j: (b, j, 0)), left(1), left(2), right(1), right(2)],
        out_specs=[pl.BlockSpec((1, tq, gw), lambda b, j: (b, j, 0)),
                   pl.BlockSpec((1, tq, gw), lambda b, j: (b, j, 0))],
        out_shape=[jax.ShapeDtypeStruct((n, length, gw), BF16), jax.ShapeDtypeStruct((n, length, gw), F32)],
        scratch_shapes=[pltpu.VMEM((tq + 2 * BAND, gw), BF16), pltpu.VMEM((tq + 2 * BAND, gw), BF16)],
        compiler_params=_params("parallel", "parallel"),
    )(qkv, qkv, qkv, qkv, qkv)


def _ssm_operators(lam_re, lam_im, log_dt, b_re, b_im, c_re, c_im, d_skip):
    L = SSM_CHUNK
    G, P = lam_re.shape[1:]
    C = d_skip.shape[-1]
    hp = lax.Precision.HIGHEST
    n = jnp.arange(L + 1, dtype=F32)[:, None, None]
    pad = SSM_POWER_ROWS - (L + 1)
    kern, pw, fbt, ct, consts = [], [], [], [], []
    for direction in range(2):
        lr, li = lam_re[direction].astype(F32), lam_im[direction].astype(F32)
        dt = jnp.exp(log_dt[direction].astype(F32))[:, None]
        mag = jnp.exp(lr * dt)
        ab_re, ab_im = mag * jnp.cos(li * dt), mag * jnp.sin(li * dt)
        den = lr * lr + li * li
        nr = ab_re - 1.0
        f_re = (nr * lr + ab_im * li) / den
        f_im = (ab_im * lr - nr * li) / den
        pmag = jnp.exp(n * (lr * dt)[None])
        p_re, p_im = pmag * jnp.cos(n * (li * dt)[None]), pmag * jnp.sin(n * (li * dt)[None])
        br, bi = b_re[direction].astype(F32), b_im[direction].astype(F32)
        fb_re = f_re[..., None] * br - f_im[..., None] * bi
        fb_im = f_re[..., None] * bi + f_im[..., None] * br
        cr, ci = c_re[direction].astype(F32), c_im[direction].astype(F32)
        pr, pi = p_re[:L].transpose(1, 2, 0)[..., None], p_im[:L].transpose(1, 2, 0)[..., None]
        crt, cit = cr.transpose(0, 2, 1)[:, :, None, :], ci.transpose(0, 2, 1)[:, :, None, :]
        e_re = (pr * crt - pi * cit).reshape(G, P, L * C)
        e_im = (pr * cit + pi * crt).reshape(G, P, L * C)
        kern.append((jnp.einsum('gpc,gpx->gcx', fb_re, e_re, precision=hp)
                     - jnp.einsum('gpc,gpx->gcx', fb_im, e_im, precision=hp)).reshape(G, C, L, C))
        for p in (p_re, p_im):
            p = jnp.pad(p.transpose(1, 0, 2), ((0, 0), (0, pad), (0, 0)))
            pw.append(jnp.concatenate([p, p, p, p], axis=-1))
        tr, ti = fb_re.transpose(0, 2, 1), fb_im.transpose(0, 2, 1)
        fbt.append(jnp.concatenate([tr, ti, ti, tr], axis=-1))
        fbt.append(jnp.concatenate([-ti, tr, tr, -ti], axis=-1))
        ct.append(jnp.concatenate([cr, -ci], axis=-1))
        ct.append(jnp.concatenate([-ci, -cr], axis=-1))
        a_re, a_im = p_re[L], p_im[L]
        consts.append(jnp.concatenate([a_re, a_re, a_re, a_re], axis=-1))
        consts.append(jnp.concatenate([-a_im, a_im, a_im, -a_im], axis=-1))
    kf, kb = kern
    zero = kf[:, :, 0] + kb[:, :, 0] + jnp.eye(C, dtype=F32)[None] * d_skip.astype(F32)[:, None, :]
    kfull = jnp.concatenate([kb[:, :, :0:-1], zero[:, :, None], kf[:, :, 1:], jnp.zeros_like(zero)[:, :, None]],
                            axis=2)
    flat = kfull.astype(BF16).reshape(G, C, 2 * L * C)
    return (flat, jnp.stack(pw, axis=1), jnp.stack(fbt, axis=1), jnp.stack(ct, axis=1),
            jnp.stack(consts + consts, axis=1))


def _ssm_kernel(u_ref, flat_ref, pw_ref, fbt_ref, ct_ref, cst_ref, z_ref,
                toe_ref, wst_ref, vt_ref, f_ref, s_ref, *, bsz):
    u = u_ref[0]
    nc = u.shape[0]
    sw = cst_ref.shape[-1]
    hw = sw // 2
    C = fbt_ref.shape[2]
    L = toe_ref.shape[0] // C
    lc = L * C

    per_tile = LANES // C
    span = (2 * L - per_tile) * C
    for b in range(per_tile):
        shifted = flat_ref[0, :, (per_tile - 1 - b) * C:(per_tile - 1 - b) * C + span]
        for a in range(L // per_tile):
            t_in = a * per_tile + b
            off = (L // per_tile - 1 - a) * LANES
            toe_ref[t_in * C:(t_in + 1) * C, :] = shifted[:, off:off + lc]

    fa_f, fb_f, fa_b, fb_b = fbt_ref[0, 0], fbt_ref[0, 1], fbt_ref[0, 2], fbt_ref[0, 3]
    ca_f, cb_f, ca_b, cb_b = ct_ref[0, 0], ct_ref[0, 1], ct_ref[0, 2], ct_ref[0, 3]

    def power(k, n):
        return pw_ref[0, k, n:n + 1, :]

    for t in range(L):
        rows = slice(t * C, (t + 1) * C)
        wst_ref[rows, 0:sw] = (fa_f * power(0, L - 1 - t) + fb_f * power(1, L - 1 - t)).astype(BF16)
        wst_ref[rows, sw:2 * sw] = (fa_b * power(2, t) + fb_b * power(3, t)).astype(BF16)
        vt_ref[rows, 0:hw] = (ca_f * power(0, t + 1)[:, :hw] + cb_f * power(1, t + 1)[:, :hw]).astype(BF16)
        vt_ref[rows, hw:sw] = (ca_b * power(2, L - t)[:, :hw] + cb_b * power(3, L - t)[:, :hw]).astype(BF16)

    f_ref[...] = jnp.dot(u, wst_ref[...], preferred_element_type=F32)
    cst = cst_ref[0]
    a1f, a2f, a1b, a2b = cst[0:1], cst[1:2], cst[2:3], cst[3:4]
    per = nc // bsz

    def swap(v):
        return jnp.concatenate([v[:, hw:], v[:, :hw]], axis=-1)

    rows = 8

    def step(i, carry):
        new = []
        for b in range(bsz):
            sf, sb = carry[2 * b], carry[2 * b + 1]
            base_f = pl.multiple_of(b * per + i * rows, rows)
            base_b = pl.multiple_of(b * per + per - rows - i * rows, rows)
            f_blk = f_ref[pl.ds(base_f, rows), 0:sw]
            b_blk = f_ref[pl.ds(base_b, rows), sw:2 * sw]
            f_rows, b_rows = [], []
            for r in range(rows):
                f_rows.append(sf[:, :hw])
                sf = a1f * sf + a2f * swap(sf) + f_blk[r:r + 1]
                b_rows.append(sb[:, :hw])
                sb = a1b * sb + a2b * swap(sb) + b_blk[rows - 1 - r:rows - r]
            s_ref[pl.ds(base_f, rows), 0:hw] = jnp.concatenate(f_rows, axis=0)
            s_ref[pl.ds(base_b, rows), hw:sw] = jnp.concatenate(b_rows[::-1], axis=0)
            new += [sf, sb]
        return tuple(new)

    lax.fori_loop(0, per // rows, step, tuple(jnp.zeros((1, sw), F32) for _ in range(2 * bsz)))
    y = jnp.dot(u, toe_ref[...], preferred_element_type=F32)
    y = y + lax.dot_general(s_ref[...].astype(BF16), vt_ref[...], (((1,), (1,)), ((), ())),
                            preferred_element_type=F32)
    z_ref[0] = jax.nn.gelu(y).astype(BF16)


def _ssm(u, bsz, lam_re, lam_im, log_dt, b_re, b_im, c_re, c_im, d_skip):
    t, width = u.shape
    L, C = SSM_CHUNK, d_skip.shape[-1]
    G = width // C
    nc = t // L
    flat, pw, fbt, ct, consts = _ssm_operators(lam_re, lam_im, log_dt, b_re, b_im, c_re, c_im, d_skip)
    sw = consts.shape[-1]
    ug = u.reshape(nc, L, G, C).transpose(2, 0, 1, 3).reshape(G, nc, L * C)

    def group_spec(a):
        return pl.BlockSpec((1,) + a.shape[1:], lambda g: (g,) + (0,) * (a.ndim - 1))

    z = pl.pallas_call(
        functools.partial(_ssm_kernel, bsz=bsz),
        grid=(G,),
        in_specs=[group_spec(a) for a in (ug, flat, pw, fbt, ct, consts)],
        out_specs=pl.BlockSpec((1, nc, L * C), lambda g: (g, 0, 0)),
        out_shape=jax.ShapeDtypeStruct((G, nc, L * C), BF16),
        scratch_shapes=[pltpu.VMEM((L * C, L * C), BF16), pltpu.VMEM((L * C, 2 * sw), BF16),
                        pltpu.VMEM((L * C, sw), BF16), pltpu.VMEM((nc, 2 * sw), F32), pltpu.VMEM((nc, sw), F32)],
        compiler_params=_params("parallel"),
    )(ug, flat, pw, fbt, ct, consts)
    return z.reshape(G, nc, L, C).transpose(1, 2, 0, 3).reshape(t, width)


def _post_kernel(o0_ref, l0_ref, o1_ref, l1_ref, o2_ref, l2_ref, z_ref, gate_ref, x_ref,
                 wa_ref, wglu_ref, bglu_ref, ws_ref, wo_ref, g2_ref, wr_ref, wrlo_ref, br_ref, tri_ref,
                 x1_ref, hp_ref, idx_ref, rank_ref, wgt_ref, cnt_ref, o_scr, l_scr, carry_ref):
    tm, d = x_ref.shape

    @pl.when(pl.program_id(0) == 0)
    def _():
        carry_ref[...] = jnp.zeros_like(carry_ref)

    outs = [o0_ref[...].astype(F32)]
    lses = [l0_ref[...]]
    for (o_ref, l_ref), (_, dil) in zip(((o1_ref, l1_ref), (o2_ref, l2_ref)), ATTN_GROUPS[1:]):
        nlb = o_scr.shape[0]
        for r in range(dil):
            for c in range(nlb):
                o_scr[c, pl.ds(r, tm // dil, stride=dil), :] = o_ref[r, :, c * LANES:(c + 1) * LANES].astype(F32)
                l_scr[c, pl.ds(r, tm // dil, stride=dil), :] = l_ref[r, :, c * LANES:(c + 1) * LANES]
        outs.append(jnp.concatenate([o_scr[c] for c in range(nlb)], axis=-1))
        lses.append(jnp.concatenate([l_scr[c] for c in range(nlb)], axis=-1))
    top = jnp.maximum(jnp.maximum(lses[0], lses[1]), lses[2])
    es = [jnp.exp(l - top) for l in lses]
    y_attn = (es[0] * outs[0] + es[1] * outs[1] + es[2] * outs[2]) / (es[0] + es[1] + es[2])
    ya = jnp.dot(y_attn.astype(BF16), wa_ref[...], preferred_element_type=F32)
    z = z_ref[...]
    glu = jnp.dot(z, wglu_ref[...], preferred_element_type=F32) + bglu_ref[...]
    zz = z.astype(F32) * jax.nn.sigmoid(glu)
    ys = jnp.dot(zz.astype(BF16), ws_ref[...], preferred_element_type=F32)
    mixed = gate_ref[:, :d].astype(F32) * ya + gate_ref[:, d:].astype(F32) * ys
    x1 = x_ref[...] + jnp.dot(mixed.astype(BF16), wo_ref[...], preferred_element_type=F32)
    x1_ref[...] = x1
    ms = jnp.mean(x1 * x1, axis=-1, keepdims=True)
    h2 = x1 * lax.rsqrt(ms + NORM_EPS) * g2_ref[...]
    hb = h2.astype(BF16)
    hp_ref[...] = h2.reshape(hp_ref.shape)

    h_lo = (h2 - hb.astype(F32)).astype(BF16)
    nt = (((1,), (1,)), ((), ()))
    logits = (lax.dot_general(wr_ref[...], hb, nt, preferred_element_type=F32)
              + lax.dot_general(wrlo_ref[...], hb, nt, preferred_element_type=F32)
              + lax.dot_general(wr_ref[...], h_lo, nt, preferred_element_type=F32) + br_ref[...])
    ne = logits.shape[0]
    e_iota = lax.broadcasted_iota(jnp.int32, logits.shape, 0)
    vals, idxs = [], []
    work = logits
    for _ in range(TOP_K):
        m = jnp.max(work, axis=0, keepdims=True)
        i = jnp.min(jnp.where(work == m, e_iota, ne), axis=0, keepdims=True)
        vals.append(m)
        idxs.append(i)
        work = jnp.where(e_iota == i, -jnp.inf, work)
    ex = [jnp.exp(v - vals[0]) for v in vals]
    tot = ex[0] + ex[1] + ex[2] + ex[3]
    onehot = jnp.zeros(logits.shape, F32)
    for i in idxs:
        onehot = onehot + (e_iota == i).astype(F32)
    before = jnp.dot(onehot.astype(BF16), tri_ref[...], preferred_element_type=F32) + carry_ref[:, 0:1]
    for k in range(TOP_K):
        idx_ref[k:k + 1, :] = idxs[k]
        rank_ref[k:k + 1, :] = jnp.sum(jnp.where(e_iota == idxs[k], before, 0.0), axis=0,
                                       keepdims=True).astype(jnp.int32)
        wgt_ref[k:k + 1, :] = ex[k] / tot
    carry_ref[...] = carry_ref[...] + jnp.sum(onehot, axis=1, keepdims=True)
    cnt_ref[...] = carry_ref[...].astype(jnp.int32)


def _post_mix(x2, seq, attn, z, gates, w_attn_branch, w_glu, b_glu, w_ssm_branch, w_out, norm2, w_router, b_router):
    t, d = x2.shape
    tm = ROW_TILE
    gw = GROUP_WIDTH
    tiles_per_seq = seq // tm
    ne = w_router.shape[1]
    ssm_w = z.shape[1]
    in_specs, args = [], []
    for (o, lse), (_, dil) in zip(attn, ATTN_GROUPS):
        if dil == 1:
            spec = pl.BlockSpec((tm, gw), lambda i: (i, 0))
            o, lse = o.reshape(t, gw), lse.reshape(t, gw)
        else:
            spec = pl.BlockSpec((dil, tm // dil, gw), lambda i: (i // tiles_per_seq, i % tiles_per_seq, 0))
        in_specs += [spec, spec]
        args += [o, lse]
    tri = jnp.asarray(np.arange(tm)[:, None] < np.arange(tm)[None, :], BF16)
    in_specs += [pl.BlockSpec((tm, ssm_w), lambda i: (i, 0)),
                 pl.BlockSpec((tm, 2 * d), lambda i: (i, 0)),
                 pl.BlockSpec((tm, d), lambda i: (i, 0)),
                 _const_spec((gw, d)), _const_spec((ssm_w, ssm_w)), _const_spec((1, ssm_w)),
                 _const_spec((ssm_w, d)), _const_spec((d, d)), _const_spec((1, d)),
                 _const_spec((ne, d)), _const_spec((ne, d)), _const_spec((ne, 1)), _const_spec((tm, tm))]
    wr_t = w_router.T.astype(F32)
    wr_hi = wr_t.astype(BF16)
    wr_lo = (wr_t - wr_hi.astype(F32)).astype(BF16)
    args += [z, gates, x2, w_attn_branch.astype(BF16), w_glu.astype(BF16), b_glu.reshape(1, ssm_w),
             w_ssm_branch.astype(BF16), w_out.astype(BF16), norm2.reshape(1, d),
             wr_hi, wr_lo, b_router.reshape(ne, 1), tri]
    tok_spec = pl.BlockSpec((TOP_K, tm), lambda i: (0, i))
    return pl.pallas_call(
        _post_kernel,
        grid=(t // tm,),
        in_specs=in_specs,
        out_specs=[pl.BlockSpec((tm, d), lambda i: (i, 0)),
                   pl.BlockSpec((tm, d // LANES, LANES), lambda i: (i, 0, 0)),
                   tok_spec, tok_spec, tok_spec, pl.BlockSpec((ne, LANES), lambda i: (0, 0))],
        out_shape=[jax.ShapeDtypeStruct((t, d), F32), jax.ShapeDtypeStruct((t, d // LANES, LANES), F32),
                   jax.ShapeDtypeStruct((TOP_K, t), jnp.int32), jax.ShapeDtypeStruct((TOP_K, t), jnp.int32),
                   jax.ShapeDtypeStruct((TOP_K, t), F32), jax.ShapeDtypeStruct((ne, LANES), jnp.int32)],
        scratch_shapes=[pltpu.VMEM((gw // LANES, tm, LANES), F32), pltpu.VMEM((gw // LANES, tm, LANES), F32),
                        pltpu.VMEM((ne, LANES), F32)],
        compiler_params=_params("arbitrary"),
    )(*args)


def _row_copy(src_ref, src_row, dst_ref, dst_row, sem):
    return pltpu.make_async_copy(src_ref.at[pl.ds(src_row, 1)], dst_ref.at[pl.ds(dst_row, 1)], sem)


def _tile_major(dest, tm):
    k, t = dest.shape
    return dest.reshape(k, t // tm, tm).transpose(1, 0, 2).reshape(-1)


def _rows_wait(ref, n_rows, sem):
    pltpu.make_async_copy(ref.at[pl.ds(0, n_rows)], ref.at[pl.ds(0, n_rows)], sem).wait()


def _dispatch_kernel(be_ref, nu_ref, dest_ref, h_ref, xs_ref, zero_ref, sems):
    i = pl.program_id(0)
    tm = h_ref.shape[0]
    tb = zero_ref.shape[0]
    n_blocks = be_ref.shape[0]

    @pl.when(i == 0)
    def _():
        zero_ref[...] = jnp.zeros_like(zero_ref)

        def holds_padding(b):
            return (b >= nu_ref[0] - 1) | (be_ref[b] != be_ref[jnp.minimum(b + 1, n_blocks - 1)])

        def zero_copy(b):
            return pltpu.make_async_copy(zero_ref, xs_ref.at[pl.ds(pl.multiple_of(b * tb, tb), tb)], sems.at[1])

        def start(b, c):
            @pl.when(holds_padding(b))
            def _():
                zero_copy(b).start()
            return c

        def wait(b, c):
            @pl.when(holds_padding(b))
            def _():
                zero_copy(b).wait()
            return c

        lax.fori_loop(0, n_blocks, start, 0)
        lax.fori_loop(0, n_blocks, wait, 0)

    def issue(t, c):
        for k in range(TOP_K):
            _row_copy(h_ref, t, xs_ref, dest_ref[k * tm + t], sems.at[0]).start(priority=k % 2)
        return c

    lax.fori_loop(0, tm, issue, 0, unroll=ROW_DMA_UNROLL)
    _rows_wait(xs_ref, TOP_K * tm, sems.at[0])


def _dispatch(block_exp, n_used, dest, hp, n_rows):
    t = hp.shape[0]
    slab = hp.shape[1:]
    tm = DISPATCH_TILE
    grid_spec = pltpu.PrefetchScalarGridSpec(
        num_scalar_prefetch=2,
        grid=(t // tm,),
        in_specs=[pl.BlockSpec((TOP_K * tm,), lambda i, be, nu: (i,), memory_space=pltpu.SMEM),
                  pl.BlockSpec((tm,) + slab, lambda i, be, nu: (i,) + (0,) * len(slab))],
        out_specs=pl.BlockSpec(memory_space=pl.ANY),
        scratch_shapes=[pltpu.VMEM((EXPERT_ROWS,) + slab, hp.dtype), pltpu.SemaphoreType.DMA((2,))],
    )
    return pl.pallas_call(
        _dispatch_kernel,
        grid_spec=grid_spec,
        out_shape=jax.ShapeDtypeStruct((n_rows,) + slab, hp.dtype),
        compiler_params=_params("arbitrary"),
    )(block_exp, n_used, _tile_major(dest, tm), hp)


def _expert_kernel(be_ref, nu_ref, xs_ref, wgu_ref, bgu_ref, wd_ref, bd_ref, ys_ref, wgu_bf, wd_bf):
    i = pl.program_id(0)
    active = i < nu_ref[0]

    @pl.when(active & ((i == 0) | (be_ref[i] != be_ref[jnp.maximum(i - 1, 0)])))
    def _():
        wgu_bf[...] = wgu_ref[0].astype(BF16)
        wd_bf[...] = wd_ref[0].astype(BF16)

    @pl.when(active)
    def _():
        x = xs_ref[...].reshape(xs_ref.shape[0], -1).astype(BF16)
        gu = jnp.dot(x, wgu_bf[...], preferred_element_type=F32) + bgu_ref[0]
        de = gu.shape[1] // 2
        gate = jnp.minimum(gu[:, :de], SWIGLU_LIMIT)
        up = jnp.clip(gu[:, de:], -SWIGLU_LIMIT, SWIGLU_LIMIT)
        act = gate * jax.nn.sigmoid(SWIGLU_ALPHA * gate) * (up + 1.0)
        y = jnp.dot(act.astype(BF16), wd_bf[...], preferred_element_type=F32) + bd_ref[0]
        ys_ref[...] = y.reshape(ys_ref.shape)

    @pl.when(i >= nu_ref[0])
    def _():
        ys_ref[...] = jnp.zeros_like(ys_ref)


def _experts(block_exp, n_used, xs, w_gate_up, b_gate_up, w_down, b_down):
    n_rows = xs.shape[0]
    ne, d, de2 = w_gate_up.shape
    tb = EXPERT_ROWS

    def row_block(i, be, nu):
        return (jnp.minimum(i, nu[0] - 1),) + (0,) * (xs.ndim - 1)

    def expert_block(i, be, nu):
        return (be[jnp.minimum(i, nu[0] - 1)], 0, 0)

    grid_spec = pltpu.PrefetchScalarGridSpec(
        num_scalar_prefetch=2,
        grid=(n_rows // tb,),
        in_specs=[pl.BlockSpec((tb,) + xs.shape[1:], row_block),
                  pl.BlockSpec((1, d, de2), expert_block),
                  pl.BlockSpec((1, 1, de2), expert_block),
                  pl.BlockSpec((1, de2 // 2, d), expert_block),
                  pl.BlockSpec((1, 1, d), expert_block)],
        out_specs=pl.BlockSpec((tb,) + xs.shape[1:], lambda i, be, nu: (i,) + (0,) * (xs.ndim - 1)),
        scratch_shapes=[pltpu.VMEM((d, de2), BF16), pltpu.VMEM((de2 // 2, d), BF16)],
    )
    return pl.pallas_call(
        _expert_kernel,
        grid_spec=grid_spec,
        out_shape=jax.ShapeDtypeStruct(xs.shape, F32),
        compiler_params=_params("arbitrary"),
    )(block_exp, n_used, xs, w_gate_up, b_gate_up.reshape(ne, 1, de2), w_down, b_down.reshape(ne, 1, d))


def _combine_kernel(dest_ref, dest_next_ref, wgt_ref, x1_ref, ys_ref, out_ref, buf_ref, sems):
    i = pl.program_id(0)
    n = pl.num_programs(0)
    tm = x1_ref.shape[0]
    slot = i % 2

    def gather(d_ref, into):
        def issue(t, c):
            for k in range(TOP_K):
                _row_copy(ys_ref, d_ref[k * tm + t], buf_ref.at[into, k], t, sems.at[into]).start(priority=k % 2)
            return c

        lax.fori_loop(0, tm, issue, 0, unroll=ROW_DMA_UNROLL)

    @pl.when(i == 0)
    def _():
        gather(dest_ref, 0)

    @pl.when(i + 1 < n)
    def _():
        gather(dest_next_ref, 1 - slot)

    for k in range(TOP_K):
        _rows_wait(buf_ref.at[slot, k], tm, sems.at[slot])
    acc = x1_ref[...]
    for k in range(TOP_K):
        acc = acc + wgt_ref[:, k:k + 1] * buf_ref[slot, k].reshape(acc.shape)
    out_ref[...] = acc


def _combine(dest, wgt_t, x1, ys):
    t, d = x1.shape
    tm = ROUTE_TILE
    n = t // tm
    return pl.pallas_call(
        _combine_kernel,
        grid=(n,),
        in_specs=[pl.BlockSpec((TOP_K * tm,), lambda i: (i,), memory_space=pltpu.SMEM),
                  pl.BlockSpec((TOP_K * tm,), lambda i: (jnp.minimum(i + 1, n - 1),), memory_space=pltpu.SMEM),
                  pl.BlockSpec((tm, TOP_K), lambda i: (i, 0)),
                  pl.BlockSpec((tm, d), lambda i: (i, 0)),
                  pl.BlockSpec(memory_space=pl.ANY)],
        out_specs=pl.BlockSpec((tm, d), lambda i: (i, 0)),
        out_shape=jax.ShapeDtypeStruct((t, d), F32),
        scratch_shapes=[pltpu.VMEM((2, TOP_K, tm) + ys.shape[1:], F32), pltpu.SemaphoreType.DMA((2,))],
        compiler_params=_params("arbitrary"),
    )(_tile_major(dest, tm), _tile_major(dest, tm), wgt_t, x1, ys)


def _moe(x1, hp, idx, rank, wgt, counts, w_gate_up, b_gate_up, w_down, b_down):
    t = x1.shape[0]
    tb = EXPERT_ROWS
    ne = w_gate_up.shape[0]
    n_blocks = -(-(t * TOP_K) // tb) + ne
    padded = ((counts + tb - 1) // tb) * tb
    pad_end = jnp.cumsum(padded)
    pad_start = pad_end - padded
    dest = rank + jnp.sum(jnp.where(idx[None] == jnp.arange(ne, dtype=jnp.int32)[:, None, None],
                                    pad_start.astype(jnp.int32)[:, None, None], 0), axis=0)
    block_start = jnp.arange(n_blocks, dtype=jnp.int32) * tb
    block_exp = jnp.minimum(jnp.sum((pad_end[None, :] <= block_start[:, None]).astype(jnp.int32), axis=1), ne - 1)
    n_used = (pad_end[-1:] // tb).astype(jnp.int32)
    xs = _dispatch(block_exp, n_used, dest, hp, n_blocks * tb)
    ys = _experts(block_exp, n_used, xs, w_gate_up, b_gate_up, w_down, b_down)
    return _combine(dest, wgt.T, x1, ys)


def kernel(x, norm1, w_in, q_norm, k_norm, lam_re, lam_im, log_dt, b_re, b_im, c_re, c_im, d_skip, w_glu, b_glu, w_attn_branch, w_ssm_branch, w_out, norm2, w_router, b_router, w_gate_up, b_gate_up, w_down, b_down):
    bsz, seq, d = x.shape
    t = bsz * seq
    x2 = x.reshape(t, d)
    for l in range(norm1.shape[0]):
        qkv0, qkv1, qkv2, u, gates = _in_proj(x2, norm1[l], w_in[l], q_norm[l], k_norm[l], seq)
        attn = [_attention(qkv0.reshape(bsz, seq, -1)), _attention(qkv1), _attention(qkv2)]
        z = _ssm(u, bsz, lam_re[l], lam_im[l], log_dt[l], b_re[l], b_im[l], c_re[l], c_im[l], d_skip[l])
        x1, hp, idx, rank, wgt, counts = _post_mix(
            x2, seq, attn, z, gates, w_attn_branch[l], w_glu[l], b_glu[l], w_ssm_branch[l], w_out[l], norm2[l],
            w_router[l], b_router[l])
        x2 = _moe(x1, hp, idx, rank, wgt, counts[:, 0], w_gate_up[l], b_gate_up[l], w_down[l], b_down[l])
    return x2.reshape(bsz, seq, d)
```

```python
import functools

import numpy as np
import jax
import jax.numpy as jnp
from jax import lax
from jax.experimental import pallas as pl
from jax.experimental.pallas import tpu as pltpu

F32 = jnp.float32
BF16 = jnp.bfloat16

HEAD_DIM = 64
HEADS_PER_GROUP = 4
ATTN_GROUPS = ((128, 1), (512, 4), (2048, 16))
GROUP_WIDTH = HEADS_PER_GROUP * HEAD_DIM
ATTN_WIDTH = GROUP_WIDTH * len(ATTN_GROUPS)
BAND = 64
ROPE_THETA = 10000.0
TOP_K = 4
SWIGLU_ALPHA = 1.702
SWIGLU_LIMIT = 7.0
NORM_EPS = 1e-6
MASK_VALUE = -1e30

ROW_TILE = 512
ATTN_Q_TILE = 2048
ATTN_Q_SUB = 128
SSM_CHUNK = 64
SSM_POWER_ROWS = -(-(SSM_CHUNK + 1) // 8) * 8
DISPATCH_TILE = 1024
ROUTE_TILE = 512
ROW_DMA_UNROLL = 8
EXPERT_ROWS = 512
V7X_VMEM_BYTES = 64 * 1024 * 1024
VMEM_LIMIT = V7X_VMEM_BYTES - 8 * 1024 * 1024
LANES = 128


def _params(*sem):
    return pltpu.CompilerParams(dimension_semantics=sem, vmem_limit_bytes=VMEM_LIMIT)


def _const_spec(shape):
    return pl.BlockSpec(shape, lambda *_: (0,) * len(shape), pipeline_mode=pl.Buffered(1))


def _permute_qk(a):
    lead = a.shape[:-1]
    n = len(lead)
    a = a.reshape(*lead, len(ATTN_GROUPS), HEADS_PER_GROUP, 2, HEAD_DIM // 2)
    return a.transpose(*range(n), n, n + 2, n + 1, n + 3).reshape(*lead, ATTN_WIDTH)


def _permute_group_lanes(piece):
    hd, hh = HEAD_DIM, HEAD_DIM // 2
    firsts = [piece[:, h * hd:h * hd + hh] for h in range(HEADS_PER_GROUP)]
    seconds = [piece[:, h * hd + hh:(h + 1) * hd] for h in range(HEADS_PER_GROUP)]
    return jnp.concatenate(firsts + seconds, axis=-1)


def _in_proj_kernel(x_ref, g1_ref, wf_ref, gq_ref, gk_ref, cos_ref, sin_ref, seg_ref,
                    qkv0_ref, qkv1_ref, qkv2_ref, u_ref, gate_ref, w_ref):
    @pl.when(pl.program_id(0) == 0)
    def _():
        rows_per_step = 128

        def cast_rows(i, _):
            rows = pl.ds(pl.multiple_of(i * rows_per_step, rows_per_step), rows_per_step)
            for c0 in range(0, 2 * ATTN_WIDTH, GROUP_WIDTH):
                w_ref[rows, c0:c0 + GROUP_WIDTH] = _permute_group_lanes(wf_ref[rows, c0:c0 + GROUP_WIDTH]).astype(BF16)
            for c0 in range(2 * ATTN_WIDTH, wf_ref.shape[1], GROUP_WIDTH):
                w_ref[rows, c0:c0 + GROUP_WIDTH] = wf_ref[rows, c0:c0 + GROUP_WIDTH].astype(BF16)
            return 0

        lax.fori_loop(0, wf_ref.shape[0] // rows_per_step, cast_rows, 0)

    x = x_ref[...]
    ms = jnp.mean(x * x, axis=-1, keepdims=True)
    h = (x * lax.rsqrt(ms + NORM_EPS) * g1_ref[...]).astype(BF16)
    cos = cos_ref[...]
    sin = sin_ref[...]
    seg = seg_ref[...]
    half = GROUP_WIDTH // 2
    tm = x.shape[0]

    def proj(c0, width):
        return jnp.dot(h, w_ref[:, c0:c0 + width], preferred_element_type=F32)

    def sum_squares(t):
        return t[:, :half] * t[:, :half] + t[:, half:] * t[:, half:]

    def norm_rope(t, tot, gain_ref, g):
        t1, t2 = t[:, :half], t[:, half:]
        r = lax.rsqrt(tot * (1.0 / HEAD_DIM) + NORM_EPS)
        a = t1 * r * gain_ref[:, g * GROUP_WIDTH:g * GROUP_WIDTH + half]
        b = t2 * r * gain_ref[:, g * GROUP_WIDTH + half:(g + 1) * GROUP_WIDTH]
        return a * cos - b * sin, b * cos + a * sin

    outs = (qkv0_ref, qkv1_ref, qkv2_ref)
    for g, (_, dil) in enumerate(ATTN_GROUPS):
        tq = proj(g * GROUP_WIDTH, GROUP_WIDTH)
        tk = proj(ATTN_WIDTH + g * GROUP_WIDTH, GROUP_WIDTH)
        ss = jnp.concatenate([sum_squares(tq), sum_squares(tk)], axis=-1).astype(BF16)
        tot = jnp.dot(ss, seg, preferred_element_type=F32)
        q1, q2 = norm_rope(tq, tot[:, :half], gq_ref, g)
        k1, k2 = norm_rope(tk, tot[:, half:], gk_ref, g)
        v = proj(2 * ATTN_WIDTH + g * GROUP_WIDTH, GROUP_WIDTH)
        qkv = jnp.concatenate([q1, q2, k1, k2, v], axis=-1)
        if dil == 1:
            outs[g][...] = qkv.astype(BF16)
        else:
            outs[g][...] = jnp.swapaxes(qkv.reshape(tm // dil, dil, qkv.shape[1]), 0, 1).astype(BF16)
    o_u = 3 * ATTN_WIDTH
    u_w = u_ref.shape[-1]
    u_ref[...] = proj(o_u, u_w).astype(BF16)
    gate_w = gate_ref.shape[-1]
    for c in range(0, gate_w, 1024):
        gate_ref[:, c:c + 1024] = jax.nn.sigmoid(proj(o_u + u_w + c, 1024)).astype(BF16)


def _in_proj(x2, norm1, w_in, q_norm, k_norm, seq):
    t, d = x2.shape
    tm = ROW_TILE
    in_w = w_in.shape[1]
    ssm_w = d // 2
    gate_w = in_w - 3 * ATTN_WIDTH - ssm_w
    gq = (_permute_qk(q_norm.reshape(-1)) * (HEAD_DIM ** -0.5)).reshape(1, ATTN_WIDTH)
    gk = _permute_qk(k_norm.reshape(-1)).reshape(1, ATTN_WIDTH)
    inv_freq = ROPE_THETA ** (-jnp.arange(0, HEAD_DIM, 2, dtype=F32) / HEAD_DIM)
    ang = jnp.arange(seq, dtype=F32)[:, None] * inv_freq[None, :]
    cos = jnp.tile(jnp.cos(ang), (1, HEADS_PER_GROUP))
    sin = jnp.tile(jnp.sin(ang), (1, HEADS_PER_GROUP))
    lane = np.arange(GROUP_WIDTH)
    seg = jnp.asarray(lane[:, None] // (HEAD_DIM // 2) == lane[None, :] // (HEAD_DIM // 2), BF16)
    bsz = t // seq
    tiles_per_seq = seq // tm
    qkv_w = 3 * GROUP_WIDTH
    out_shape = [jax.ShapeDtypeStruct((t, qkv_w), BF16)]
    out_specs = [pl.BlockSpec((tm, qkv_w), lambda i: (i, 0))]
    for _, dil in ATTN_GROUPS[1:]:
        out_shape.append(jax.ShapeDtypeStruct((bsz * dil, seq // dil, qkv_w), BF16))
        out_specs.append(pl.BlockSpec((dil, tm // dil, qkv_w),
                                      lambda i: (i // tiles_per_seq, i % tiles_per_seq, 0)))
    out_shape += [jax.ShapeDtypeStruct((t, ssm_w), BF16), jax.ShapeDtypeStruct((t, gate_w), BF16)]
    out_specs += [pl.BlockSpec((tm, ssm_w), lambda i: (i, 0)), pl.BlockSpec((tm, gate_w), lambda i: (i, 0))]
    return pl.pallas_call(
        _in_proj_kernel,
        grid=(t // tm,),
        in_specs=[
            pl.BlockSpec((tm, d), lambda i: (i, 0)),
            _const_spec((1, d)),
            _const_spec((d, in_w)),
            _const_spec((1, ATTN_WIDTH)),
            _const_spec((1, ATTN_WIDTH)),
            pl.BlockSpec((tm, GROUP_WIDTH // 2), lambda i: (i % tiles_per_seq, 0)),
            pl.BlockSpec((tm, GROUP_WIDTH // 2), lambda i: (i % tiles_per_seq, 0)),
            _const_spec((GROUP_WIDTH, GROUP_WIDTH)),
        ],
        out_specs=out_specs,
        out_shape=out_shape,
        scratch_shapes=[pltpu.VMEM((d, in_w), BF16)],
        compiler_params=_params("arbitrary"),
    )(x2, norm1.reshape(1, d), w_in, gq, gk, cos, sin, seg)


def _attn_kernel(c_ref, kl_ref, vl_ref, kr_ref, vr_ref, o_ref, lse_ref, kcat_ref, vcat_ref, *, length):
    tq = c_ref.shape[1]
    gw = GROUP_WIDTH
    j = pl.program_id(1)
    kcat_ref[0:BAND] = kl_ref[0]
    kcat_ref[BAND:BAND + tq] = c_ref[0, :, gw:2 * gw]
    kcat_ref[BAND + tq:] = kr_ref[0]
    vcat_ref[0:BAND] = vl_ref[0]
    vcat_ref[BAND:BAND + tq] = c_ref[0, :, 2 * gw:]
    vcat_ref[BAND + tq:] = vr_ref[0]
    sub = min(ATTN_Q_SUB, tq)
    nk = sub + 2 * BAND
    nh = HEADS_PER_GROUP
    row = lax.broadcasted_iota(jnp.int32, (nh * sub, nk), 0) & (sub - 1)
    col = lax.broadcasted_iota(jnp.int32, (nh * sub, nk), 1)
    in_band = jnp.abs(col - BAND - row) <= BAND
    q_lane = lax.broadcasted_iota(jnp.int32, (1, gw), 1)
    q_head = (q_lane % (gw // 2)) // (HEAD_DIM // 2)
    v_head = q_lane // HEAD_DIM
    for s in range(tq // sub):
        q_s = c_ref[0, s * sub:(s + 1) * sub, 0:gw]
        k_s = kcat_ref[s * sub:s * sub + nk]
        v_s = vcat_ref[s * sub:s * sub + nk]
        kpos = j * tq + s * sub - BAND + col
        valid = in_band & (kpos >= 0) & (kpos < length)
        q4 = jnp.concatenate([jnp.where(q_head == hh, q_s, jnp.zeros_like(q_s)) for hh in range(nh)], axis=0)
        sc = lax.dot_general(q4, k_s, (((1,), (1,)), ((), ())), preferred_element_type=F32)
        sc = jnp.where(valid, sc, MASK_VALUE)
        m = jnp.max(sc, axis=-1, keepdims=True)
        p = jnp.exp(sc - m)
        den = jnp.sum(p, axis=-1, keepdims=True)
        o4 = jnp.dot(p.astype(BF16), v_s, preferred_element_type=F32) * (1.0 / den)
        l4 = m + jnp.log(den)
        out = o4[0:sub]
        lse = jnp.broadcast_to(l4[0:sub], (sub, gw))
        for hh in range(1, nh):
            out = jnp.where(v_head == hh, o4[hh * sub:(hh + 1) * sub], out)
            lse = jnp.where(v_head == hh, l4[hh * sub:(hh + 1) * sub], lse)
        o_ref[0, s * sub:(s + 1) * sub] = out.astype(BF16)
        lse_ref[0, s * sub:(s + 1) * sub] = lse


def _attention(qkv):
    n, length, _ = qkv.shape
    tq = min(ATTN_Q_TILE, length)
    gw = GROUP_WIDTH
    hb = tq // BAND
    last = length // BAND - 1

    def left(lane_block):
        return pl.BlockSpec((1, BAND, gw), lambda b, j: (b, jnp.maximum(j * hb - 1, 0), lane_block))

    def right(lane_block):
        return pl.BlockSpec((1, BAND, gw), lambda b, j: (b, jnp.minimum((j + 1) * hb, last), lane_block))

    return pl.pallas_call(
        functools.partial(_attn_kernel, length=length),
        grid=(n, length // tq),
        in_specs=[pl.BlockSpec((1, tq, 3 * gw), lambda b, j: (b, j, 0)), left(1), left(2), right(1), right(2)],
        out_specs=[pl.BlockSpec((1, tq, gw), lambda b, j: (b, j, 0)),
                   pl.BlockSpec((1, tq, gw), lambda b, j: (b, j, 0))],
        out_shape=[jax.ShapeDtypeStruct((n, length, gw), BF16), jax.ShapeDtypeStruct((n, length, gw), F32)],
        scratch_shapes=[pltpu.VMEM((tq + 2 * BAND, gw), BF16), pltpu.VMEM((tq + 2 * BAND, gw), BF16)],
        compiler_params=_params("parallel", "parallel"),
    )(qkv, qkv, qkv, qkv, qkv)


def _ssm_operators(lam_re, lam_im, log_dt, b_re, b_im, c_re, c_im, d_skip):
    L = SSM_CHUNK
    G, P = lam_re.shape[1:]
    C = d_skip.shape[-1]
    hp = lax.Precision.HIGHEST
    n = jnp.arange(L + 1, dtype=F32)[:, None, None]
    pad = SSM_POWER_ROWS - (L + 1)
    kern, pw, fbt, ct, consts = [], [], [], [], []
    for direction in range(2):
        lr, li = lam_re[direction].astype(F32), lam_im[direction].astype(F32)
        dt = jnp.exp(log_dt[direction].astype(F32))[:, None]
        mag = jnp.exp(lr * dt)
        ab_re, ab_im = mag * jnp.cos(li * dt), mag * jnp.sin(li * dt)
        den = lr * lr + li * li
        nr = ab_re - 1.0
        f_re = (nr * lr + ab_im * li) / den
        f_im = (ab_im * lr - nr * li) / den
        pmag = jnp.exp(n * (lr * dt)[None])
        p_re, p_im = pmag * jnp.cos(n * (li * dt)[None]), pmag * jnp.sin(n * (li * dt)[None])
        br, bi = b_re[direction].astype(F32), b_im[direction].astype(F32)
        fb_re = f_re[..., None] * br - f_im[..., None] * bi
        fb_im = f_re[..., None] * bi + f_im[..., None] * br
        cr, ci = c_re[direction].astype(F32), c_im[direction].astype(F32)
        pr, pi = p_re[:L].transpose(1, 2, 0)[..., None], p_im[:L].transpose(1, 2, 0)[..., None]
        crt, cit = cr.transpose(0, 2, 1)[:, :, None, :], ci.transpose(0, 2, 1)[:, :, None, :]
        e_re = (pr * crt - pi * cit).reshape(G, P, L * C)
        e_im = (pr * cit + pi * crt).reshape(G, P, L * C)
        kern.append((jnp.einsum('gpc,gpx->gcx', fb_re, e_re, precision=hp)
                     - jnp.einsum('gpc,gpx->gcx', fb_im, e_im, precision=hp)).reshape(G, C, L, C))
        for p in (p_re, p_im):
            p = jnp.pad(p.transpose(1, 0, 2), ((0, 0), (0, pad), (0, 0)))
            pw.append(jnp.concatenate([p, p, p, p], axis=-1))
        tr, ti = fb_re.transpose(0, 2, 1), fb_im.transpose(0, 2, 1)
        fbt.append(jnp.concatenate([tr, ti, ti, tr], axis=-1))
        fbt.append(jnp.concatenate([-ti, tr, tr, -ti], axis=-1))
        ct.append(jnp.concatenate([cr, -ci], axis=-1))
        ct.append(jnp.concatenate([-ci, -cr], axis=-1))
        a_re, a_im = p_re[L], p_im[L]
        consts.append(jnp.concatenate([a_re, a_re, a_re, a_re], axis=-1))
        consts.append(jnp.concatenate([-a_im, a_im, a_im, -a_im], axis=-1))
    kf, kb = kern
    zero = kf[:, :, 0] + kb[:, :, 0] + jnp.eye(C, dtype=F32)[None] * d_skip.astype(F32)[:, None, :]
    kfull = jnp.concatenate([kb[:, :, :0:-1], zero[:, :, None], kf[:, :, 1:], jnp.zeros_like(zero)[:, :, None]],
                            axis=2)
    flat = kfull.astype(BF16).reshape(G, C, 2 * L * C)
    return (flat, jnp.stack(pw, axis=1), jnp.stack(fbt, axis=1), jnp.stack(ct, axis=1),
            jnp.stack(consts + consts, axis=1))


def _ssm_kernel(u_ref, flat_ref, pw_ref, fbt_ref, ct_ref, cst_ref, z_ref,
                toe_ref, wst_ref, vt_ref, f_ref, s_ref, *, bsz):
    u = u_ref[0]
    nc = u.shape[0]
    sw = cst_ref.shape[-1]
    hw = sw // 2
    C = fbt_ref.shape[2]
    L = toe_ref.shape[0] // C
    lc = L * C

    per_tile = LANES // C
    span = (2 * L - per_tile) * C
    for b in range(per_tile):
        shifted = flat_ref[0, :, (per_tile - 1 - b) * C:(per_tile - 1 - b) * C + span]
        for a in range(L // per_tile):
            t_in = a * per_tile + b
            off = (L // per_tile - 1 - a) * LANES
            toe_ref[t_in * C:(t_in + 1) * C, :] = shifted[:, off:off + lc]

    fa_f, fb_f, fa_b, fb_b = fbt_ref[0, 0], fbt_ref[0, 1], fbt_ref[0, 2], fbt_ref[0, 3]
    ca_f, cb_f, ca_b, cb_b = ct_ref[0, 0], ct_ref[0, 1], ct_ref[0, 2], ct_ref[0, 3]

    def power(k, n):
        return pw_ref[0, k, n:n + 1, :]

    for t in range(L):
        rows = slice(t * C, (t + 1) * C)
        wst_ref[rows, 0:sw] = (fa_f * power(0, L - 1 - t) + fb_f * power(1, L - 1 - t)).astype(BF16)
        wst_ref[rows, sw:2 * sw] = (fa_b * power(2, t) + fb_b * power(3, t)).astype(BF16)
        vt_ref[rows, 0:hw] = (ca_f * power(0, t + 1)[:, :hw] + cb_f * power(1, t + 1)[:, :hw]).astype(BF16)
        vt_ref[rows, hw:sw] = (ca_b * power(2, L - t)[:, :hw] + cb_b * power(3, L - t)[:, :hw]).astype(BF16)

    f_ref[...] = jnp.dot(u, wst_ref[...], preferred_element_type=F32)
    cst = cst_ref[0]
    a1f, a2f, a1b, a2b = cst[0:1], cst[1:2], cst[2:3], cst[3:4]
    per = nc // bsz

    def swap(v):
        return jnp.concatenate([v[:, hw:], v[:, :hw]], axis=-1)

    rows = 8

    def step(i, carry):
        new = []
        for b in range(bsz):
            sf, sb = carry[2 * b], carry[2 * b + 1]
            base_f = pl.multiple_of(b * per + i * rows, rows)
            base_b = pl.multiple_of(b * per + per - rows - i * rows, rows)
            f_blk = f_ref[pl.ds(base_f, rows), 0:sw]
            b_blk = f_ref[pl.ds(base_b, rows), sw:2 * sw]
            f_rows, b_rows = [], []
            for r in range(rows):
                f_rows.append(sf[:, :hw])
                sf = a1f * sf + a2f * swap(sf) + f_blk[r:r + 1]
                b_rows.append(sb[:, :hw])
                sb = a1b * sb + a2b * swap(sb) + b_blk[rows - 1 - r:rows - r]
            s_ref[pl.ds(base_f, rows), 0:hw] = jnp.concatenate(f_rows, axis=0)
            s_ref[pl.ds(base_b, rows), hw:sw] = jnp.concatenate(b_rows[::-1], axis=0)
            new += [sf, sb]
        return tuple(new)

    lax.fori_loop(0, per // rows, step, tuple(jnp.zeros((1, sw), F32) for _ in range(2 * bsz)))
    y = jnp.dot(u, toe_ref[...], preferred_element_type=F32)
    y = y + lax.dot_general(s_ref[...].astype(BF16), vt_ref[...], (((1,), (1,)), ((), ())),
                            preferred_element_type=F32)
    z_ref[0] = jax.nn.gelu(y).astype(BF16)


def _ssm(u, bsz, lam_re, lam_im, log_dt, b_re, b_im, c_re, c_im, d_skip):
    t, width = u.shape
    L, C = SSM_CHUNK, d_skip.shape[-1]
    G = width // C
    nc = t // L
    flat, pw, fbt, ct, consts = _ssm_operators(lam_re, lam_im, log_dt, b_re, b_im, c_re, c_im, d_skip)
    sw = consts.shape[-1]
    ug = u.reshape(nc, L, G, C).transpose(2, 0, 1, 3).reshape(G, nc, L * C)

    def group_spec(a):
        return pl.BlockSpec((1,) + a.shape[1:], lambda g: (g,) + (0,) * (a.ndim - 1))

    z = pl.pallas_call(
        functools.partial(_ssm_kernel, bsz=bsz),
        grid=(G,),
        in_specs=[group_spec(a) for a in (ug, flat, pw, fbt, ct, consts)],
        out_specs=pl.BlockSpec((1, nc, L * C), lambda g: (g, 0, 0)),
        out_shape=jax.ShapeDtypeStruct((G, nc, L * C), BF16),
        scratch_shapes=[pltpu.VMEM((L * C, L * C), BF16), pltpu.VMEM((L * C, 2 * sw), BF16),
                        pltpu.VMEM((L * C, sw), BF16), pltpu.VMEM((nc, 2 * sw), F32), pltpu.VMEM((nc, sw), F32)],
        compiler_params=_params("parallel"),
    )(ug, flat, pw, fbt, ct, consts)
    return z.reshape(G, nc, L, C).transpose(1, 2, 0, 3).reshape(t, width)


def _post_kernel(o0_ref, l0_ref, o1_ref, l1_ref, o2_ref, l2_ref, z_ref, gate_ref, x_ref,
                 wa_ref, wglu_ref, bglu_ref, ws_ref, wo_ref, g2_ref, wr_ref, wrlo_ref, br_ref, tri_ref,
                 x1_ref, hp_ref, idx_ref, rank_ref, wgt_ref, cnt_ref, o_scr, l_scr, carry_ref):
    tm, d = x_ref.shape

    @pl.when(pl.program_id(0) == 0)
    def _():
        carry_ref[...] = jnp.zeros_like(carry_ref)

    outs = [o0_ref[...].astype(F32)]
    lses = [l0_ref[...]]
    for (o_ref, l_ref), (_, dil) in zip(((o1_ref, l1_ref), (o2_ref, l2_ref)), ATTN_GROUPS[1:]):
        nlb = o_scr.shape[0]
        for r in range(dil):
            for c in range(nlb):
                o_scr[c, pl.ds(r, tm // dil, stride=dil), :] = o_ref[r, :, c * LANES:(c + 1) * LANES].astype(F32)
                l_scr[c, pl.ds(r, tm // dil, stride=dil), :] = l_ref[r, :, c * LANES:(c + 1) * LANES]
        outs.append(jnp.concatenate([o_scr[c] for c in range(nlb)], axis=-1))
        lses.append(jnp.concatenate([l_scr[c] for c in range(nlb)], axis=-1))
    top = jnp.maximum(jnp.maximum(lses[0], lses[1]), lses[2])
    es = [jnp.exp(l - top) for l in lses]
    y_attn = (es[0] * outs[0] + es[1] * outs[1] + es[2] * outs[2]) / (es[0] + es[1] + es[2])
    ya = jnp.dot(y_attn.astype(BF16), wa_ref[...], preferred_element_type=F32)
    z = z_ref[...]
    glu = jnp.dot(z, wglu_ref[...], preferred_element_type=F32) + bglu_ref[...]
    zz = z.astype(F32) * jax.nn.sigmoid(glu)
    ys = jnp.dot(zz.astype(BF16), ws_ref[...], preferred_element_type=F32)
    mixed = gate_ref[:, :d].astype(F32) * ya + gate_ref[:, d:].astype(F32) * ys
    x1 = x_ref[...] + jnp.dot(mixed.astype(BF16), wo_ref[...], preferred_element_type=F32)
    x1_ref[...] = x1
    ms = jnp.mean(x1 * x1, axis=-1, keepdims=True)
    h2 = x1 * lax.rsqrt(ms + NORM_EPS) * g2_ref[...]
    hb = h2.astype(BF16)
    hp_ref[...] = h2.reshape(hp_ref.shape)

    h_lo = (h2 - hb.astype(F32)).astype(BF16)
    nt = (((1,), (1,)), ((), ()))
    logits = (lax.dot_general(wr_ref[...], hb, nt, preferred_element_type=F32)
              + lax.dot_general(wrlo_ref[...], hb, nt, preferred_element_type=F32)
              + lax.dot_general(wr_ref[...], h_lo, nt, preferred_element_type=F32) + br_ref[...])
    ne = logits.shape[0]
    e_iota = lax.broadcasted_iota(jnp.int32, logits.shape, 0)
    vals, idxs = [], []
    work = logits
    for _ in range(TOP_K):
        m = jnp.max(work, axis=0, keepdims=True)
        i = jnp.min(jnp.where(work == m, e_iota, ne), axis=0, keepdims=True)
        vals.append(m)
        idxs.append(i)
        work = jnp.where(e_iota == i, -jnp.inf, work)
    ex = [jnp.exp(v - vals[0]) for v in vals]
    tot = ex[0] + ex[1] + ex[2] + ex[3]
    onehot = jnp.zeros(logits.shape, F32)
    for i in idxs:
        onehot = onehot + (e_iota == i).astype(F32)
    before = jnp.dot(onehot.astype(BF16), tri_ref[...], preferred_element_type=F32) + carry_ref[:, 0:1]
    for k in range(TOP_K):
        idx_ref[k:k + 1, :] = idxs[k]
        rank_ref[k:k + 1, :] = jnp.sum(jnp.where(e_iota == idxs[k], before, 0.0), axis=0,
                                       keepdims=True).astype(jnp.int32)
        wgt_ref[k:k + 1, :] = ex[k] / tot
    carry_ref[...] = carry_ref[...] + jnp.sum(onehot, axis=1, keepdims=True)
    cnt_ref[...] = carry_ref[...].astype(jnp.int32)


def _post_mix(x2, seq, attn, z, gates, w_attn_branch, w_glu, b_glu, w_ssm_branch, w_out, norm2, w_router, b_router):
    t, d = x2.shape
    tm = ROW_TILE
    gw = GROUP_WIDTH
    tiles_per_seq = seq // tm
    ne = w_router.shape[1]
    ssm_w = z.shape[1]
    in_specs, args = [], []
    for (o, lse), (_, dil) in zip(attn, ATTN_GROUPS):
        if dil == 1:
            spec = pl.BlockSpec((tm, gw), lambda i: (i, 0))
            o, lse = o.reshape(t, gw), lse.reshape(t, gw)
        else:
            spec = pl.BlockSpec((dil, tm // dil, gw), lambda i: (i // tiles_per_seq, i % tiles_per_seq, 0))
        in_specs += [spec, spec]
        args += [o, lse]
    tri = jnp.asarray(np.arange(tm)[:, None] < np.arange(tm)[None, :], BF16)
    in_specs += [pl.BlockSpec((tm, ssm_w), lambda i: (i, 0)),
                 pl.BlockSpec((tm, 2 * d), lambda i: (i, 0)),
                 pl.BlockSpec((tm, d), lambda i: (i, 0)),
                 _const_spec((gw, d)), _const_spec((ssm_w, ssm_w)), _const_spec((1, ssm_w)),
                 _const_spec((ssm_w, d)), _const_spec((d, d)), _const_spec((1, d)),
                 _const_spec((ne, d)), _const_spec((ne, d)), _const_spec((ne, 1)), _const_spec((tm, tm))]
    wr_t = w_router.T.astype(F32)
    wr_hi = wr_t.astype(BF16)
    wr_lo = (wr_t - wr_hi.astype(F32)).astype(BF16)
    args += [z, gates, x2, w_attn_branch.astype(BF16), w_glu.astype(BF16), b_glu.reshape(1, ssm_w),
             w_ssm_branch.astype(BF16), w_out.astype(BF16), norm2.reshape(1, d),
             wr_hi, wr_lo, b_router.reshape(ne, 1), tri]
    tok_spec = pl.BlockSpec((TOP_K, tm), lambda i: (0, i))
    return pl.pallas_call(
        _post_kernel,
        grid=(t // tm,),
        in_specs=in_specs,
        out_specs=[pl.BlockSpec((tm, d), lambda i: (i, 0)),
                   pl.BlockSpec((tm, d // LANES, LANES), lambda i: (i, 0, 0)),
                   tok_spec, tok_spec, tok_spec, pl.BlockSpec((ne, LANES), lambda i: (0, 0))],
        out_shape=[jax.ShapeDtypeStruct((t, d), F32), jax.ShapeDtypeStruct((t, d // LANES, LANES), F32),
                   jax.ShapeDtypeStruct((TOP_K, t), jnp.int32), jax.ShapeDtypeStruct((TOP_K, t), jnp.int32),
                   jax.ShapeDtypeStruct((TOP_K, t), F32), jax.ShapeDtypeStruct((ne, LANES), jnp.int32)],
        scratch_shapes=[pltpu.VMEM((gw // LANES, tm, LANES), F32), pltpu.VMEM((gw // LANES, tm, LANES), F32),
                        pltpu.VMEM((ne, LANES), F32)],
        compiler_params=_params("arbitrary"),
    )(*args)


def _row_copy(src_ref, src_row, dst_ref, dst_row, sem):
    return pltpu.make_async_copy(src_ref.at[pl.ds(src_row, 1)], dst_ref.at[pl.ds(dst_row, 1)], sem)


def _tile_major(dest, tm):
    k, t = dest.shape
    return dest.reshape(k, t // tm, tm).transpose(1, 0, 2).reshape(-1)


def _rows_wait(ref, n_rows, sem):
    pltpu.make_async_copy(ref.at[pl.ds(0, n_rows)], ref.at[pl.ds(0, n_rows)], sem).wait()


def _dispatch_kernel(be_ref, nu_ref, dest_ref, h_ref, xs_ref, zero_ref, sems):
    i = pl.program_id(0)
    tm = h_ref.shape[0]
    tb = zero_ref.shape[0]
    n_blocks = be_ref.shape[0]

    @pl.when(i == 0)
    def _():
        zero_ref[...] = jnp.zeros_like(zero_ref)

        def holds_padding(b):
            return (b >= nu_ref[0] - 1) | (be_ref[b] != be_ref[jnp.minimum(b + 1, n_blocks - 1)])

        def zero_copy(b):
            return pltpu.make_async_copy(zero_ref, xs_ref.at[pl.ds(pl.multiple_of(b * tb, tb), tb)], sems.at[1])

        def start(b, c):
            @pl.when(holds_padding(b))
            def _():
                zero_copy(b).start()
            return c

        def wait(b, c):
            @pl.when(holds_padding(b))
            def _():
                zero_copy(b).wait()
            return c

        lax.fori_loop(0, n_blocks, start, 0)
        lax.fori_loop(0, n_blocks, wait, 0)

    def issue(t, c):
        for k in range(TOP_K):
            _row_copy(h_ref, t, xs_ref, dest_ref[k * tm + t], sems.at[0]).start(priority=k % 2)
        return c

    lax.fori_loop(0, tm, issue, 0, unroll=ROW_DMA_UNROLL)
    _rows_wait(xs_ref, TOP_K * tm, sems.at[0])


def _dispatch(block_exp, n_used, dest, hp, n_rows):
    t = hp.shape[0]
    slab = hp.shape[1:]
    tm = DISPATCH_TILE
    grid_spec = pltpu.PrefetchScalarGridSpec(
        num_scalar_prefetch=2,
        grid=(t // tm,),
        in_specs=[pl.BlockSpec((TOP_K * tm,), lambda i, be, nu: (i,), memory_space=pltpu.SMEM),
                  pl.BlockSpec((tm,) + slab, lambda i, be, nu: (i,) + (0,) * len(slab))],
        out_specs=pl.BlockSpec(memory_space=pl.ANY),
        scratch_shapes=[pltpu.VMEM((EXPERT_ROWS,) + slab, hp.dtype), pltpu.SemaphoreType.DMA((2,))],
    )
    return pl.pallas_call(
        _dispatch_kernel,
        grid_spec=grid_spec,
        out_shape=jax.ShapeDtypeStruct((n_rows,) + slab, hp.dtype),
        compiler_params=_params("arbitrary"),
    )(block_exp, n_used, _tile_major(dest, tm), hp)


def _expert_kernel(be_ref, nu_ref, xs_ref, wgu_ref, bgu_ref, wd_ref, bd_ref, ys_ref, wgu_bf, wd_bf):
    i = pl.program_id(0)
    active = i < nu_ref[0]

    @pl.when(active & ((i == 0) | (be_ref[i] != be_ref[jnp.maximum(i - 1, 0)])))
    def _():
        wgu_bf[...] = wgu_ref[0].astype(BF16)
        wd_bf[...] = wd_ref[0].astype(BF16)

    @pl.when(active)
    def _():
        x = xs_ref[...].reshape(xs_ref.shape[0], -1).astype(BF16)
        gu = jnp.dot(x, wgu_bf[...], preferred_element_type=F32) + bgu_ref[0]
        de = gu.shape[1] // 2
        gate = jnp.minimum(gu[:, :de], SWIGLU_LIMIT)
        up = jnp.clip(gu[:, de:], -SWIGLU_LIMIT, SWIGLU_LIMIT)
        act = gate * jax.nn.sigmoid(SWIGLU_ALPHA * gate) * (up + 1.0)
        y = jnp.dot(act.astype(BF16), wd_bf[...], preferred_element_type=F32) + bd_ref[0]
        ys_ref[...] = y.reshape(ys_ref.shape)

    @pl.when(i >= nu_ref[0])
    def _():
        ys_ref[...] = jnp.zeros_like(ys_ref)


def _experts(block_exp, n_used, xs, w_gate_up, b_gate_up, w_down, b_down):
    n_rows = xs.shape[0]
    ne, d, de2 = w_gate_up.shape
    tb = EXPERT_ROWS

    def row_block(i, be, nu):
        return (jnp.minimum(i, nu[0] - 1),) + (0,) * (xs.ndim - 1)

    def expert_block(i, be, nu):
        return (be[jnp.minimum(i, nu[0] - 1)], 0, 0)

    grid_spec = pltpu.PrefetchScalarGridSpec(
        num_scalar_prefetch=2,
        grid=(n_rows // tb,),
        in_specs=[pl.BlockSpec((tb,) + xs.shape[1:], row_block),
                  pl.BlockSpec((1, d, de2), expert_block),
                  pl.BlockSpec((1, 1, de2), expert_block),
                  pl.BlockSpec((1, de2 // 2, d), expert_block),
                  pl.BlockSpec((1, 1, d), expert_block)],
        out_specs=pl.BlockSpec((tb,) + xs.shape[1:], lambda i, be, nu: (i,) + (0,) * (xs.ndim - 1)),
        scratch_shapes=[pltpu.VMEM((d, de2), BF16), pltpu.VMEM((de2 // 2, d), BF16)],
    )
    return pl.pallas_call(
        _expert_kernel,
        grid_spec=grid_spec,
        out_shape=jax.ShapeDtypeStruct(xs.shape, F32),
        compiler_params=_params("arbitrary"),
    )(block_exp, n_used, xs, w_gate_up, b_gate_up.reshape(ne, 1, de2), w_down, b_down.reshape(ne, 1, d))


def _combine_kernel(dest_ref, dest_next_ref, wgt_ref, x1_ref, ys_ref, out_ref, buf_ref, sems):
    i = pl.program_id(0)
    n = pl.num_programs(0)
    tm = x1_ref.shape[0]
    slot = i % 2

    def gather(d_ref, into):
        def issue(t, c):
            for k in range(TOP_K):
                _row_copy(ys_ref, d_ref[k * tm + t], buf_ref.at[into, k], t, sems.at[into]).start(priority=k % 2)
            return c

        lax.fori_loop(0, tm, issue, 0, unroll=ROW_DMA_UNROLL)

    @pl.when(i == 0)
    def _():
        gather(dest_ref, 0)

    @pl.when(i + 1 < n)
    def _():
        gather(dest_next_ref, 1 - slot)

    for k in range(TOP_K):
        _rows_wait(buf_ref.at[slot, k], tm, sems.at[slot])
    acc = x1_ref[...]
    for k in range(TOP_K):
        acc = acc + wgt_ref[:, k:k + 1] * buf_ref[slot, k].reshape(acc.shape)
    out_ref[...] = acc


def _combine(dest, wgt_t, x1, ys):
    t, d = x1.shape
    tm = ROUTE_TILE
    n = t // tm
    return pl.pallas_call(
        _combine_kernel,
        grid=(n,),
        in_specs=[pl.BlockSpec((TOP_K * tm,), lambda i: (i,), memory_space=pltpu.SMEM),
                  pl.BlockSpec((TOP_K * tm,), lambda i: (jnp.minimum(i + 1, n - 1),), memory_space=pltpu.SMEM),
                  pl.BlockSpec((tm, TOP_K), lambda i: (i, 0)),
                  pl.BlockSpec((tm, d), lambda i: (i, 0)),
                  pl.BlockSpec(memory_space=pl.ANY)],
        out_specs=pl.BlockSpec((tm, d), lambda i: (i, 0)),
        out_shape=jax.ShapeDtypeStruct((t, d), F32),
        scratch_shapes=[pltpu.VMEM((2, TOP_K, tm) + ys.shape[1:], F32), pltpu.SemaphoreType.DMA((2,))],
        compiler_params=_params("arbitrary"),
    )(_tile_major(dest, tm), _tile_major(dest, tm), wgt_t, x1, ys)


def _moe(x1, hp, idx, rank, wgt, counts, w_gate_up, b_gate_up, w_down, b_down):
    t = x1.shape[0]
    tb = EXPERT_ROWS
    ne = w_gate_up.shape[0]
    n_blocks = -(-(t * TOP_K) // tb) + ne
    padded = ((counts + tb - 1) // tb) * tb
    pad_end = jnp.cumsum(padded)
    pad_start = pad_end - padded
    dest = rank + jnp.sum(jnp.where(idx[None] == jnp.arange(ne, dtype=jnp.int32)[:, None, None],
                                    pad_start.astype(jnp.int32)[:, None, None], 0), axis=0)
    block_start = jnp.arange(n_blocks, dtype=jnp.int32) * tb
    block_exp = jnp.minimum(jnp.sum((pad_end[None, :] <= block_start[:, None]).astype(jnp.int32), axis=1), ne - 1)
    n_used = (pad_end[-1:] // tb).astype(jnp.int32)
    xs = _dispatch(block_exp, n_used, dest, hp, n_blocks * tb)
    ys = _experts(block_exp, n_used, xs, w_gate_up, b_gate_up, w_down, b_down)
    return _combine(dest, wgt.T, x1, ys)


def kernel(x, norm1, w_in, q_norm, k_norm, lam_re, lam_im, log_dt, b_re, b_im, c_re, c_im, d_skip, w_glu, b_glu, w_attn_branch, w_ssm_branch, w_out, norm2, w_router, b_router, w_gate_up, b_gate_up, w_down, b_down):
    bsz, seq, d = x.shape
    t = bsz * seq
    x2 = x.reshape(t, d)
    for l in range(norm1.shape[0]):
        qkv0, qkv1, qkv2, u, gates = _in_proj(x2, norm1[l], w_in[l], q_norm[l], k_norm[l], seq)
        attn = [_attention(qkv0.reshape(bsz, seq, -1)), _attention(qkv1), _attention(qkv2)]
        z = _ssm(u, bsz, lam_re[l], lam_im[l], log_dt[l], b_re[l], b_im[l], c_re[l], c_im[l], d_skip[l])
        x1, hp, idx, rank, wgt, counts = _post_mix(
            x2, seq, attn, z, gates, w_attn_branch[l], w_glu[l], b_glu[l], w_ssm_branch[l], w_out[l], norm2[l],
            w_router[l], b_router[l])
        x2 = _moe(x1, hp, idx, rank, wgt, counts[:, 0], w_gate_up[l], b_gate_up[l], w_down[l], b_down[l])
    return x2.reshape(bsz, seq, d)
```
